```python
import math
import jax, jax.numpy as jnp
from jax import lax
import numpy as np

D_MODEL = 1024
BATCH = 2
SEQ = 8192
DEPTH = 1

N_META = 16
ATTN_WIDTH = D_MODEL // 2
SSM_WIDTH = D_MODEL - ATTN_WIDTH
N_ATTN_HEADS = 4
QK_DIM = ATTN_WIDTH // N_ATTN_HEADS // 2
V_DIM = 2 * QK_DIM
ROT_DIM = QK_DIM // 4
ROPE_THETA = 500000.0
SSM_GROUP = 16
N_SSM_GROUPS = SSM_WIDTH // SSM_GROUP
SSM_STATE = 64
D_FF = -(-8 * D_MODEL // (3 * 256)) * 256
Q_BLOCK = 128
EPS = 1e-6
QK_COLS = N_ATTN_HEADS * QK_DIM
V_COLS = N_ATTN_HEADS * V_DIM
IN_COLS = 4 * QK_COLS + V_COLS + SSM_WIDTH

kernel_name = "hybrid_diffattn_s5_parallel_heads"


def rms_norm(x, g):
    x32 = x.astype(jnp.float32)
    y = x32 * lax.rsqrt(jnp.mean(x32 * x32, axis=-1, keepdims=True) + EPS)
    return (y * g.astype(jnp.float32)).astype(x.dtype)


def lambda_init(layer_idx):
    return 0.8 - 0.6 * math.exp(-0.3 * layer_idx)


def rope_tables(length):
    pos = jnp.arange(length, dtype=jnp.float32)
    inv_freq = ROPE_THETA ** (-jnp.arange(0, ROT_DIM, 2, dtype=jnp.float32) / ROT_DIM)
    ang = pos[:, None] * inv_freq[None, :]
    ang = jnp.concatenate([ang, ang], axis=-1)
    return jnp.cos(ang), jnp.sin(ang)


def partial_rope(t, cos, sin):
    t_rot, t_pass = t[..., :ROT_DIM], t[..., ROT_DIM:]
    half = ROT_DIM // 2
    rot_half = jnp.concatenate([-t_rot[..., half:], t_rot[..., :half]], axis=-1)
    t_rot = (t_rot * cos + rot_half * sin).astype(t.dtype)
    return jnp.concatenate([t_rot, t_pass], axis=-1)


def diff_attention(q1, q2, k1, k2, v, lam):
    b, h, lp, d = q1.shape
    nb = lp // Q_BLOCK
    scale = 1.0 / math.sqrt(d)
    k_pos = jnp.arange(lp)
    v32 = v.astype(jnp.float32)

    def to_blocks(t):
        return t.reshape(b, h, nb, Q_BLOCK, t.shape[-1]).transpose(2, 0, 1, 3, 4)

    def block(args):
        q1b, q2b, start = args
        q_pos = start + jnp.arange(Q_BLOCK)
        causal = k_pos[None, :] <= q_pos[:, None]

        def probs(qb, k):
            s = jnp.einsum('bhqd,bhkd->bhqk', qb, k, preferred_element_type=jnp.float32) * scale
            return jax.nn.softmax(jnp.where(causal, s, -jnp.inf), axis=-1)

        attn = probs(q1b, k1) - lam * probs(q2b, k2)
        return jnp.einsum('bhqk,bhkd->bhqd', attn, v32)

    starts = jnp.arange(nb, dtype=jnp.int32) * Q_BLOCK
    out = lax.map(block, (to_blocks(q1), to_blocks(q2), starts))
    return out.transpose(1, 2, 0, 3, 4).reshape(b, h, lp, v.shape[-1])


def s5_ssm(u, a_re, a_im, log_dt, b_re, b_im, c_re, c_im, d_skip):
    u = u.astype(jnp.float32)
    dt = jnp.exp(log_dt.astype(jnp.float32))[:, None]
    a_re = a_re.astype(jnp.float32)
    a_im = a_im.astype(jnp.float32)
    mag = jnp.exp(a_re * dt)
    ang = a_im * dt
    ab_re, ab_im = mag * jnp.cos(ang), mag * jnp.sin(ang)
    den = a_re * a_re + a_im * a_im
    nr, ni = ab_re - 1.0, ab_im
    f_re = (nr * a_re + ni * a_im) / den
    f_im = (ni * a_re - nr * a_im) / den
    b_re = b_re.astype(jnp.float32)
    b_im = b_im.astype(jnp.float32)
    bb_re = f_re[..., None] * b_re - f_im[..., None] * b_im
    bb_im = f_re[..., None] * b_im + f_im[..., None] * b_re
    bu_re = jnp.einsum('blgh,gph->blgp', u, bb_re)
    bu_im = jnp.einsum('blgh,gph->blgp', u, bb_im)
    ar = jnp.broadcast_to(ab_re, bu_re.shape)
    ai = jnp.broadcast_to(ab_im, bu_re.shape)

    def combine(e1, e2):
        ar1, ai1, br1, bi1 = e1
        ar2, ai2, br2, bi2 = e2
        return (ar2 * ar1 - ai2 * ai1,
                ar2 * ai1 + ai2 * ar1,
                ar2 * br1 - ai2 * bi1 + br2,
                ar2 * bi1 + ai2 * br1 + bi2)

    _, _, x_re, x_im = lax.associative_scan(combine, (ar, ai, bu_re, bu_im), axis=1)
    y = (jnp.einsum('blgp,ghp->blgh', x_re, c_re.astype(jnp.float32))
         - jnp.einsum('blgp,ghp->blgh', x_im, c_im.astype(jnp.float32)))
    y = y + d_skip.astype(jnp.float32).reshape(N_SSM_GROUPS, SSM_GROUP) * u
    bsz, length = u.shape[0], u.shape[1]
    return y.reshape(bsz, length, N_SSM_GROUPS * SSM_GROUP)


def setup_inputs(seed: int = 0) -> dict:
    key = jax.random.key(seed)
    ks = jax.random.split(key, 32)
    f32 = jnp.float32

    def nrm(k, shape, scale):
        return jax.random.normal(k, shape, f32) * scale

    def gain(k, shape):
        return 1.0 + 0.02 * jax.random.normal(k, shape, f32)

    n_idx = jnp.arange(SSM_STATE, dtype=f32)
    a_re = -0.5 + 0.01 * jax.random.normal(ks[8], (DEPTH, N_SSM_GROUPS, SSM_STATE), f32)
    a_im = math.pi * n_idx[None, None, :] + 0.01 * jax.random.normal(ks[9], (DEPTH, N_SSM_GROUPS, SSM_STATE), f32)
    log_dt = jax.random.uniform(ks[10], (DEPTH, N_SSM_GROUPS), f32, math.log(1e-3), math.log(1e-1))
    return {
        "x": nrm(ks[0], (BATCH, SEQ, D_MODEL), 1.0),
        "meta": nrm(ks[1], (N_META, D_MODEL), 1.0),
        "pre_mix_g": gain(ks[2], (DEPTH, D_MODEL)),
        "w_in": nrm(ks[3], (DEPTH, D_MODEL, IN_COLS), D_MODEL ** -0.5),
        "lambda_q1": nrm(ks[4], (DEPTH, QK_DIM), 0.1),
        "lambda_k1": nrm(ks[5], (DEPTH, QK_DIM), 0.1),
        "lambda_q2": nrm(ks[6], (DEPTH, QK_DIM), 0.1),
        "lambda_k2": nrm(ks[7], (DEPTH, QK_DIM), 0.1),
        "subln_g": gain(ks[11], (DEPTH, V_DIM)),
        "a_re": a_re,
        "a_im": a_im,
        "log_dt": log_dt,
        "b_re": nrm(ks[12], (DEPTH, N_SSM_GROUPS, SSM_STATE, SSM_GROUP), (2.0 * SSM_GROUP) ** -0.5),
        "b_im": nrm(ks[13], (DEPTH, N_SSM_GROUPS, SSM_STATE, SSM_GROUP), (2.0 * SSM_GROUP) ** -0.5),
        "c_re": nrm(ks[14], (DEPTH, N_SSM_GROUPS, SSM_GROUP, SSM_STATE), (2.0 * SSM_STATE) ** -0.5),
        "c_im": nrm(ks[15], (DEPTH, N_SSM_GROUPS, SSM_GROUP, SSM_STATE), (2.0 * SSM_STATE) ** -0.5),
        "d_skip": nrm(ks[16], (DEPTH, SSM_WIDTH), 1.0),
        "w_glu": nrm(ks[17], (DEPTH, SSM_WIDTH, SSM_WIDTH), SSM_WIDTH ** -0.5),
        "b_glu": nrm(ks[18], (DEPTH, SSM_WIDTH), 0.02),
        "ssm_out_g": gain(ks[19], (DEPTH, SSM_WIDTH)),
        "w_out": nrm(ks[20], (DEPTH, D_MODEL, D_MODEL), D_MODEL ** -0.5),
        "post_mix_g": gain(ks[21], (DEPTH, D_MODEL)),
        "pre_ffn_g": gain(ks[22], (DEPTH, D_MODEL)),
        "w_gate": nrm(ks[23], (DEPTH, D_MODEL, D_FF), D_MODEL ** -0.5),
        "w_up": nrm(ks[24], (DEPTH, D_MODEL, D_FF), D_MODEL ** -0.5),
        "w_down": nrm(ks[25], (DEPTH, D_FF, D_MODEL), D_FF ** -0.5),
        "post_ffn_g": gain(ks[26], (DEPTH, D_MODEL)),
    }


def reference(x, meta, pre_mix_g, w_in, lambda_q1, lambda_k1, lambda_q2, lambda_k2, subln_g,
              a_re, a_im, log_dt, b_re, b_im, c_re, c_im, d_skip, w_glu, b_glu, ssm_out_g,
              w_out, post_mix_g, pre_ffn_g, w_gate, w_up, w_down, post_ffn_g):
    bsz = x.shape[0]
    meta_b = jnp.broadcast_to(meta.astype(x.dtype)[None], (bsz, N_META, x.shape[-1]))
    h_res = jnp.concatenate([meta_b, x], axis=1)
    length = h_res.shape[1]
    l_pad = -(-length // Q_BLOCK) * Q_BLOCK
    cos, sin = rope_tables(length)

    def heads(t, dim):
        t = t.reshape(bsz, length, N_ATTN_HEADS, dim).transpose(0, 2, 1, 3)
        return t

    def pad(t):
        return jnp.pad(t, ((0, 0), (0, 0), (0, l_pad - length), (0, 0)))

    for l in range(DEPTH):
        lam_init = lambda_init(l)
        h = rms_norm(h_res, pre_mix_g[l])
        proj = jnp.einsum('bld,dc->blc', h, w_in[l])
        q1, q2, k1, k2, v, u = jnp.split(
            proj, [QK_COLS, 2 * QK_COLS, 3 * QK_COLS, 4 * QK_COLS, 4 * QK_COLS + V_COLS], axis=-1)

        q1 = pad(partial_rope(heads(q1, QK_DIM), cos, sin))
        q2 = pad(partial_rope(heads(q2, QK_DIM), cos, sin))
        k1 = pad(partial_rope(heads(k1, QK_DIM), cos, sin))
        k2 = pad(partial_rope(heads(k2, QK_DIM), cos, sin))
        vh = pad(heads(v, V_DIM))
        lam = (jnp.exp(jnp.sum(lambda_q1[l].astype(jnp.float32) * lambda_k1[l].astype(jnp.float32)))
               - jnp.exp(jnp.sum(lambda_q2[l].astype(jnp.float32) * lambda_k2[l].astype(jnp.float32)))
               + lam_init)
        o = diff_attention(q1, q2, k1, k2, vh, lam)[:, :, :length]
        o = rms_norm(o, subln_g[l]) * (1.0 - lam_init)
        o = o.transpose(0, 2, 1, 3).reshape(bsz, length, V_COLS).astype(h.dtype)

        y = s5_ssm(u.reshape(bsz, length, N_SSM_GROUPS, SSM_GROUP), a_re[l], a_im[l], log_dt[l],
                   b_re[l], b_im[l], c_re[l], c_im[l], d_skip[l])
        y = jax.nn.gelu(y, approximate=False)
        y = y * jax.nn.sigmoid(jnp.einsum('blc,cd->bld', y, w_glu[l].astype(jnp.float32))
                               + b_glu[l].astype(jnp.float32))
        y = rms_norm(y, ssm_out_g[l]).astype(h.dtype)

        mix = jnp.einsum('blc,cd->bld', jnp.concatenate([o, y], axis=-1), w_out[l])
        h_res = h_res + rms_norm(mix, post_mix_g[l])

        h = rms_norm(h_res, pre_ffn_g[l])
        f = jax.nn.silu(jnp.einsum('bld,df->blf', h, w_gate[l])) * jnp.einsum('bld,df->blf', h, w_up[l])
        f = jnp.einsum('blf,fd->bld', f, w_down[l])
        h_res = h_res + rms_norm(f, post_ffn_g[l])

    return h_res[:, N_META:]
```

```python
import functools
import math

import jax
import jax.numpy as jnp
from jax import lax
from jax.experimental import pallas as pl
from jax.experimental.pallas import tpu as pltpu

D_MODEL = 1024
N_META = 16
N_HEADS = 4
QK_DIM = 64
V_DIM = 128
ROT_DIM = 16
ROPE_THETA = 500000.0
SSM_GROUP = 16
N_GROUPS = 32
SSM_STATE = 64
SSM_WIDTH = 512
ATTN_WIDTH = 512
D_FF = 2816
EPS = 1e-6
LAM_INIT = 0.8 - 0.6 * math.exp(-0.3 * 0)

LANES = 128
CHUNK = 16
N_PAIRS = N_GROUPS // 2
PAIR_W = 2 * CHUNK * SSM_GROUP
SCAN_PAD = 256

ROW_TILE = 512
Q_TILE = 512
KV_TILE = 512
VMEM_LIMIT = 56 * 1024 * 1024

F32 = jnp.float32
BF16 = jnp.bfloat16


def _rms(x, g):
    return x * lax.rsqrt(jnp.mean(x * x, axis=-1, keepdims=True) + EPS) * g


def _in_proj_kernel(x_ref, g_ref, w_ref, cos_ref, sa_ref, sb_ref, q_ref, k_ref, v_ref, u_ref):
    h = _rms(x_ref[...], g_ref[...]).astype(BF16)
    proj = jnp.dot(h, w_ref[...], preferred_element_type=F32)
    cos, sa, sb = cos_ref[...], sa_ref[...], sb_ref[...]

    def rope(t):
        return (t * cos + pltpu.roll(t, LANES - ROT_DIM // 2, axis=1) * sa
                + pltpu.roll(t, ROT_DIM // 2, axis=1) * sb)

    for c in range(ATTN_WIDTH // LANES):
        sl = slice(c * LANES, (c + 1) * LANES)
        q_ref[:, sl] = rope(proj[:, c * LANES:(c + 1) * LANES]).astype(BF16)
        k_ref[:, sl] = rope(proj[:, ATTN_WIDTH + c * LANES:ATTN_WIDTH + (c + 1) * LANES]).astype(BF16)
    v_ref[...] = proj[:, 2 * ATTN_WIDTH:3 * ATTN_WIDTH].astype(BF16)
    u_ref[...] = proj[:, 3 * ATTN_WIDTH:]


def _in_proj(x2d, g, w, cos, sa, sb, tm):
    n = x2d.shape[0]
    n_pos = cos.shape[0] // tm
    row = lambda i: (i, 0)
    pos = lambda i: (i % n_pos, 0)
    const = lambda i: (0, 0)
    return pl.pallas_call(
        _in_proj_kernel,
        grid=(n // tm,),
        in_specs=[
            pl.BlockSpec((tm, D_MODEL), row),
            pl.BlockSpec((1, D_MODEL), const),
            pl.BlockSpec((D_MODEL, 4 * ATTN_WIDTH), const),
            pl.BlockSpec((tm, LANES), pos),
            pl.BlockSpec((tm, LANES), pos),
            pl.BlockSpec((tm, LANES), pos),
        ],
        out_specs=[pl.BlockSpec((tm, ATTN_WIDTH), row)] * 4,
        out_shape=[jax.ShapeDtypeStruct((n, ATTN_WIDTH), BF16)] * 3
        + [jax.ShapeDtypeStruct((n, SSM_WIDTH), F32)],
        compiler_params=pltpu.CompilerParams(
            dimension_semantics=("arbitrary",), vmem_limit_bytes=VMEM_LIMIT),
        name="in_proj",
    )(x2d, g, w, cos, sa, sb)


def _attn_kernel(q_ref, k_ref, v_ref, km_ref, vm_ref, lq1_ref, lk1_ref, lq2_ref, lk2_ref, sg_ref,
                 o_ref, m_ref, l_ref, acc_ref):
    qi = pl.program_id(2)
    q = q_ref[...]
    lane = lax.broadcasted_iota(jnp.int32, q.shape, 1)
    zero = jnp.zeros_like(q)
    qs = (jnp.where(lane < QK_DIM, q, zero), jnp.where(lane >= QK_DIM, q, zero))
    nt = (((1,), (1,)), ((), ()))

    meta_mask = lax.broadcasted_iota(jnp.int32, (Q_TILE, LANES), 1) < N_META
    for i in range(2):
        s = lax.dot_general(qs[i], km_ref[...], nt, preferred_element_type=F32)
        s = jnp.where(meta_mask, s, -jnp.inf)
        m = jnp.max(s, axis=1, keepdims=True)
        p = jnp.exp(s - m)
        m_ref[i] = m
        l_ref[i] = jnp.sum(p, axis=1, keepdims=True)
        acc_ref[i] = jnp.dot(p.astype(BF16), vm_ref[...], preferred_element_type=F32)

    def step(k, v, mask):
        for i in range(2):
            s = lax.dot_general(qs[i], k, nt, preferred_element_type=F32)
            if mask is not None:
                s = jnp.where(mask, s, -jnp.inf)
            m_old = m_ref[i]
            m_new = jnp.maximum(m_old, jnp.max(s, axis=1, keepdims=True))
            alpha = jnp.exp(m_old - m_new)
            p = jnp.exp(s - m_new)
            m_ref[i] = m_new
            l_ref[i] = alpha * l_ref[i] + jnp.sum(p, axis=1, keepdims=True)
            acc_ref[i] = alpha * acc_ref[i] + jnp.dot(p.astype(BF16), v, preferred_element_type=F32)

    def body(j, carry):
        start = pl.multiple_of(j * KV_TILE, KV_TILE)
        step(k_ref[pl.ds(start, KV_TILE), :], v_ref[pl.ds(start, KV_TILE), :], None)
        return carry

    lax.fori_loop(0, qi, body, 0)

    start = pl.multiple_of(qi * KV_TILE, KV_TILE)
    causal = (lax.broadcasted_iota(jnp.int32, (Q_TILE, KV_TILE), 1)
              <= lax.broadcasted_iota(jnp.int32, (Q_TILE, KV_TILE), 0))
    step(k_ref[pl.ds(start, KV_TILE), :], v_ref[pl.ds(start, KV_TILE), :], causal)

    lam = (jnp.exp(jnp.sum(lq1_ref[...] * lk1_ref[...], axis=1, keepdims=True))
           - jnp.exp(jnp.sum(lq2_ref[...] * lk2_ref[...], axis=1, keepdims=True)) + LAM_INIT)
    o = acc_ref[0] / l_ref[0] - lam * (acc_ref[1] / l_ref[1])
    o_ref[...] = (_rms(o, sg_ref[...]) * (1.0 - LAM_INIT)).astype(BF16)


def _attention(q, k, v, km, vm, lq1, lk1, lq2, lk2, sg, bsz, seq):
    nq = seq // Q_TILE
    assert Q_TILE == KV_TILE
    qmap = lambda b, h, i: (b * nq + i, h)
    kvmap = lambda b, h, i: (b, h)
    metamap = lambda b, h, i: (0, h)
    const = lambda b, h, i: (0, 0)
    vec = pl.BlockSpec((1, QK_DIM), const)
    return pl.pallas_call(
        _attn_kernel,
        grid=(bsz, N_HEADS, nq),
        in_specs=[
            pl.BlockSpec((Q_TILE, LANES), qmap),
            pl.BlockSpec((seq, LANES), kvmap),
            pl.BlockSpec((seq, LANES), kvmap),
            pl.BlockSpec((LANES, LANES), metamap),
            pl.BlockSpec((LANES, LANES), metamap),
            vec, vec, vec, vec,
            pl.BlockSpec((1, V_DIM), const),
        ],
        out_specs=pl.BlockSpec((Q_TILE, LANES), qmap),
        out_shape=jax.ShapeDtypeStruct((bsz * seq, ATTN_WIDTH), BF16),
        scratch_shapes=[
            pltpu.VMEM((2, Q_TILE, 1), F32),
            pltpu.VMEM((2, Q_TILE, 1), F32),
            pltpu.VMEM((2, Q_TILE, V_DIM), F32),
        ],
        compiler_params=pltpu.CompilerParams(
            dimension_semantics=("arbitrary", "arbitrary", "arbitrary"), vmem_limit_bytes=VMEM_LIMIT),
        name="diff_attention",
    )(q, k, v, km, vm, lq1, lk1, lq2, lk2, sg)


def _ssm_kernel(ucur_ref, uprev_ref, tz_ref, bs_ref, cs_ref, ap_ref, d_ref, y_ref,
                re_a, im_a, re_b, im_b, *, n_chunks, bsz):
    @pl.when(pl.program_id(0) == 0)
    def _():
        for buf in (re_a, im_a, re_b, im_b):
            buf[:, :SCAN_PAD, :] = jnp.zeros((bsz, SCAN_PAD, LANES), F32)

    w = jnp.dot(uprev_ref[0], bs_ref[0], preferred_element_type=F32)
    for b in range(bsz):
        re_a[b, SCAN_PAD:, :] = w[b * n_chunks:(b + 1) * n_chunks, :LANES]
        im_a[b, SCAN_PAD:, :] = w[b * n_chunks:(b + 1) * n_chunks, LANES:]

    src, dst = (re_a, im_a), (re_b, im_b)
    n_steps = n_chunks.bit_length() - 1
    for step in range(n_steps):
        shift = 1 << step
        ar = ap_ref[0, step:step + 1, :]
        ai = ap_ref[0, n_steps + step:n_steps + step + 1, :]
        for b in range(bsz):
            xr = src[0][b, SCAN_PAD:, :]
            xi = src[1][b, SCAN_PAD:, :]
            pr = src[0][b, SCAN_PAD - shift:SCAN_PAD - shift + n_chunks, :]
            pi = src[1][b, SCAN_PAD - shift:SCAN_PAD - shift + n_chunks, :]
            dst[0][b, SCAN_PAD:, :] = xr + ar * pr - ai * pi
            dst[1][b, SCAN_PAD:, :] = xi + ar * pi + ai * pr
        src, dst = dst, src

    state = jnp.concatenate(
        [jnp.concatenate([src[0][b, SCAN_PAD:, :], src[1][b, SCAN_PAD:, :]], axis=1) for b in range(bsz)],
        axis=0).astype(BF16)
    u = ucur_ref[0]
    ub = u.astype(BF16)
    half = PAIR_W // 2
    y = jnp.dot(state, cs_ref[0], preferred_element_type=F32)
    y = y + jnp.concatenate(
        [jnp.dot(ub[:, :half], tz_ref[0, 0], preferred_element_type=F32),
         jnp.dot(ub[:, half:], tz_ref[0, 1], preferred_element_type=F32)], axis=1)
    y = y + d_ref[0] * u
    y_ref[0] = 0.5 * y * (1.0 + lax.erf(y * (1.0 / math.sqrt(2.0))))


def _ssm(ucur, uprev, tz, bs, cs, ap, dp, bsz):
    rows = ucur.shape[1]
    n_chunks = rows // bsz
    assert n_chunks & (n_chunks - 1) == 0 and n_chunks // 2 <= SCAN_PAD
    n_steps = n_chunks.bit_length() - 1
    pair = lambda g: (g, 0, 0)
    scan_buf = pltpu.VMEM((bsz, SCAN_PAD + n_chunks, LANES), F32)
    return pl.pallas_call(
        functools.partial(_ssm_kernel, n_chunks=n_chunks, bsz=bsz),
        grid=(N_PAIRS,),
        in_specs=[
            pl.BlockSpec((1, rows, PAIR_W), pair),
            pl.BlockSpec((1, rows, PAIR_W), pair),
            pl.BlockSpec((1, 2, PAIR_W // 2, PAIR_W // 2), lambda g: (g, 0, 0, 0)),
            pl.BlockSpec((1, PAIR_W, 2 * LANES), pair),
            pl.BlockSpec((1, 2 * LANES, PAIR_W), pair),
            pl.BlockSpec((1, 2 * n_steps, LANES), pair),
            pl.BlockSpec((1, 1, PAIR_W), pair),
        ],
        out_specs=pl.BlockSpec((1, rows, PAIR_W), pair),
        out_shape=jax.ShapeDtypeStruct((N_PAIRS, rows, PAIR_W), F32),
        scratch_shapes=[scan_buf] * 4,
        compiler_params=pltpu.CompilerParams(
            dimension_semantics=("arbitrary",), vmem_limit_bytes=VMEM_LIMIT),
        name="s5_chunked",
    )(ucur, uprev, tz, bs, cs, ap, dp)


def _ssm_weights(a_re, a_im, log_dt, b_re, b_im, c_re, c_im, d_skip, n_steps):
    hp = lax.Precision.HIGHEST
    dt = jnp.exp(log_dt)[:, None]
    lam_re, lam_im = a_re * dt, a_im * dt

    def power(n):
        n = n[:, None, None]
        mag = jnp.exp(n * lam_re)
        return mag * jnp.cos(n * lam_im), mag * jnp.sin(n * lam_im)

    ab_re, ab_im = jnp.exp(lam_re) * jnp.cos(lam_im), jnp.exp(lam_re) * jnp.sin(lam_im)
    den = a_re * a_re + a_im * a_im
    nr, ni = ab_re - 1.0, ab_im
    f_re = (nr * a_re + ni * a_im) / den
    f_im = (ni * a_re - nr * a_im) / den
    bb_re = f_re[..., None] * b_re - f_im[..., None] * b_im
    bb_im = f_re[..., None] * b_im + f_im[..., None] * b_re

    p_re, p_im = power(jnp.arange(CHUNK + 1, dtype=F32))
    ca_re = c_re[None] * p_re[:, :, None, :] - c_im[None] * p_im[:, :, None, :]
    ca_im = c_re[None] * p_im[:, :, None, :] + c_im[None] * p_re[:, :, None, :]
    kern = (jnp.einsum('ngip,gpj->ngij', ca_re[:CHUNK], bb_re, precision=hp)
            - jnp.einsum('ngip,gpj->ngij', ca_im[:CHUNK], bb_im, precision=hp))
    t_in = jnp.arange(CHUNK)[:, None]
    t_out = jnp.arange(CHUNK)[None, :]
    lag = t_out - t_in
    tz = jnp.where((lag >= 0)[:, :, None, None, None], kern[jnp.maximum(lag, 0)], 0.0)
    tz = tz.transpose(2, 0, 4, 1, 3).reshape(N_PAIRS, 2, CHUNK * SSM_GROUP, CHUNK * SSM_GROUP)

    r_re, r_im = p_re[CHUNK - 1::-1][:CHUNK], p_im[CHUNK - 1::-1][:CHUNK]
    bs_re = r_re[..., None] * bb_re[None] - r_im[..., None] * bb_im[None]
    bs_im = r_re[..., None] * bb_im[None] + r_im[..., None] * bb_re[None]

    def pack_bs(m):
        return m.transpose(1, 0, 3, 2).reshape(N_PAIRS, 2, CHUNK * SSM_GROUP, SSM_STATE)

    zeros = jnp.zeros((N_PAIRS, CHUNK * SSM_GROUP, SSM_STATE), F32)
    b_r, b_i = pack_bs(bs_re), pack_bs(bs_im)
    bs = jnp.concatenate([
        jnp.concatenate([b_r[:, 0], zeros, b_i[:, 0], zeros], axis=2),
        jnp.concatenate([zeros, b_r[:, 1], zeros, b_i[:, 1]], axis=2)], axis=1)

    def pack_cs(m):
        return m.transpose(1, 3, 0, 2).reshape(N_PAIRS, 2, SSM_STATE, CHUNK * SSM_GROUP)

    c_r, c_i = pack_cs(ca_re[1:]), pack_cs(-ca_im[1:])
    zc = jnp.zeros((N_PAIRS, SSM_STATE, CHUNK * SSM_GROUP), F32)
    cs = jnp.concatenate([
        jnp.concatenate([c_r[:, 0], zc], axis=2),
        jnp.concatenate([zc, c_r[:, 1]], axis=2),
        jnp.concatenate([c_i[:, 0], zc], axis=2),
        jnp.concatenate([zc, c_i[:, 1]], axis=2)], axis=1)

    s_re, s_im = power(CHUNK * (2.0 ** jnp.arange(n_steps, dtype=F32)))
    pack_ap = lambda m: m.transpose(1, 0, 2).reshape(N_PAIRS, 2, n_steps, SSM_STATE).transpose(0, 2, 1, 3) \
        .reshape(N_PAIRS, n_steps, LANES)
    ap = jnp.concatenate([pack_ap(s_re), pack_ap(s_im)], axis=1)

    dp = jnp.broadcast_to(d_skip.reshape(N_PAIRS, 2, 1, SSM_GROUP), (N_PAIRS, 2, CHUNK, SSM_GROUP)) \
        .reshape(N_PAIRS, 1, PAIR_W)
    return tz.astype(BF16), bs.astype(BF16), cs.astype(BF16), ap, dp


def _out_ffn_kernel(x_ref, o_ref, y_ref, wglu_ref, bglu_ref, sg_ref, wout_ref, pmg_ref, pfg_ref,
                    wg_ref, wu_ref, wd_ref, pog_ref, out_ref):
    y = y_ref[...]
    gate = jnp.dot(y.astype(BF16), wglu_ref[...], preferred_element_type=F32) + bglu_ref[...]
    y = _rms(y * jax.nn.sigmoid(gate), sg_ref[...]).astype(BF16)
    mix = (jnp.dot(o_ref[...], wout_ref[:ATTN_WIDTH, :], preferred_element_type=F32)
           + jnp.dot(y, wout_ref[ATTN_WIDTH:, :], preferred_element_type=F32))
    h1 = x_ref[...] + _rms(mix, pmg_ref[...])
    h2 = _rms(h1, pfg_ref[...]).astype(BF16)
    g = jnp.dot(h2, wg_ref[...], preferred_element_type=F32)
    up = jnp.dot(h2, wu_ref[...], preferred_element_type=F32)
    f = (g * jax.nn.sigmoid(g) * up).astype(BF16)
    f = jnp.dot(f, wd_ref[...], preferred_element_type=F32)
    out_ref[...] = h1 + _rms(f, pog_ref[...])


def _out_ffn(x2d, o, y, wglu, bglu, sg, wout, pmg, pfg, wg, wu, wd, pog, tm):
    n = x2d.shape[0]
    row = lambda i: (i, 0)
    const = lambda i: (0, 0)

    def resident(shape):
        return pl.BlockSpec(shape, const, pipeline_mode=pl.Buffered(1))

    return pl.pallas_call(
        _out_ffn_kernel,
        grid=(n // tm,),
        in_specs=[
            pl.BlockSpec((tm, D_MODEL), row),
            pl.BlockSpec((tm, ATTN_WIDTH), row),
            pl.BlockSpec((tm, SSM_WIDTH), row),
            resident((SSM_WIDTH, SSM_WIDTH)),
            resident((1, SSM_WIDTH)),
            resident((1, SSM_WIDTH)),
            resident((D_MODEL, D_MODEL)),
            resident((1, D_MODEL)),
            resident((1, D_MODEL)),
            resident((D_MODEL, D_FF)),
            resident((D_MODEL, D_FF)),
            resident((D_FF, D_MODEL)),
            resident((1, D_MODEL)),
        ],
        out_specs=pl.BlockSpec((tm, D_MODEL), row),
        out_shape=jax.ShapeDtypeStruct((n, D_MODEL), F32),
        compiler_params=pltpu.CompilerParams(
            dimension_semantics=("arbitrary",), vmem_limit_bytes=VMEM_LIMIT),
        name="out_ffn",
    )(x2d, o, y, wglu, bglu, sg, wout, pmg, pfg, wg, wu, wd, pog)


def _rope_tables(length):
    pos = jnp.arange(length, dtype=F32)
    inv_freq = ROPE_THETA ** (-jnp.arange(0, ROT_DIM, 2, dtype=F32) / ROT_DIM)
    ang = pos[:, None] * inv_freq[None, :]
    cos, sin = jnp.cos(ang), jnp.sin(ang)
    ones = jnp.ones((length, QK_DIM - ROT_DIM), F32)
    zeros = jnp.zeros((length, QK_DIM - ROT_DIM), F32)
    z8 = jnp.zeros_like(sin)
    cos_t = jnp.concatenate([cos, cos, ones], axis=1)
    sa_t = jnp.concatenate([-sin, z8, zeros], axis=1)
    sb_t = jnp.concatenate([z8, sin, zeros], axis=1)
    two = lambda t: jnp.concatenate([t, t], axis=1)
    return two(cos_t), two(sa_t), two(sb_t)


def _permute_w_in(w):
    qk = QK_DIM * N_HEADS
    scale = 1.0 / math.sqrt(QK_DIM)
    q1, q2, k1, k2 = (w[:, i * qk:(i + 1) * qk].reshape(D_MODEL, N_HEADS, QK_DIM) for i in range(4))
    qp = (jnp.concatenate([q1, q2], axis=2) * scale).reshape(D_MODEL, 2 * qk)
    kp = jnp.concatenate([k1, k2], axis=2).reshape(D_MODEL, 2 * qk)
    return jnp.concatenate([qp, kp, w[:, 4 * qk:]], axis=1).astype(BF16)


def _chunk_major(u, n_rows):
    t = u.reshape(n_rows, CHUNK, N_PAIRS, 2, SSM_GROUP).transpose(2, 0, 3, 1, 4)
    return t.reshape(N_PAIRS, n_rows, PAIR_W)


def kernel(x, meta, pre_mix_g, w_in, lambda_q1, lambda_k1, lambda_q2, lambda_k2, subln_g, a_re, a_im, log_dt,
           b_re, b_im, c_re, c_im, d_skip, w_glu, b_glu, ssm_out_g, w_out, post_mix_g, pre_ffn_g, w_gate,
           w_up, w_down, post_ffn_g):
    bsz, seq, _ = x.shape
    n = bsz * seq
    n_chunks = seq // CHUNK
    x2d = x.reshape(n, D_MODEL)
    row = lambda t: t[0].reshape(1, -1)

    cos, sa, sb = _rope_tables(N_META + seq)
    w_in_p = _permute_w_in(w_in[0])
    g0 = row(pre_mix_g)
    q, k, v, u = _in_proj(x2d, g0, w_in_p, cos[N_META:], sa[N_META:], sb[N_META:], ROW_TILE)
    _, k_m, v_m, u_m = _in_proj(meta, g0, w_in_p, cos[:N_META], sa[:N_META], sb[:N_META], N_META)

    pad_meta = lambda t: jnp.pad(t, ((0, LANES - N_META), (0, 0)))
    o = _attention(q, k, v, pad_meta(k_m), pad_meta(v_m), lambda_q1, lambda_k1, lambda_q2, lambda_k2,
                   row(subln_g), bsz, seq)

    n_steps = n_chunks.bit_length() - 1
    tz, bs, cs, ap, dp = _ssm_weights(a_re[0], a_im[0], log_dt[0], b_re[0], b_im[0], c_re[0], c_im[0],
                                      d_skip[0], n_steps)
    ucur = _chunk_major(u, bsz * n_chunks)
    um = _chunk_major(u_m, 1)
    ucur_b = ucur.reshape(N_PAIRS, bsz, n_chunks, PAIR_W)
    uprev = jnp.concatenate(
        [jnp.broadcast_to(um[:, None], (N_PAIRS, bsz, 1, PAIR_W)), ucur_b[:, :, :-1]], axis=2)
    uprev = uprev.reshape(N_PAIRS, bsz * n_chunks, PAIR_W).astype(BF16)
    yc = _ssm(ucur, uprev, tz, bs, cs, ap, dp, bsz)
    y = yc.reshape(N_PAIRS, bsz * n_chunks, 2, CHUNK, SSM_GROUP).transpose(1, 3, 0, 2, 4).reshape(n, SSM_WIDTH)

    out = _out_ffn(x2d, o, y, w_glu[0].astype(BF16), row(b_glu), row(ssm_out_g), w_out[0].astype(BF16),
                   row(post_mix_g), row(pre_ffn_g), w_gate[0].astype(BF16), w_up[0].astype(BF16),
                   w_down[0].astype(BF16), row(post_ffn_g), ROW_TILE)
    return out.reshape(bsz, seq, D_MODEL)
```

```python
import functools
import math

import jax
import jax.numpy as jnp
from jax import lax
from jax.experimental import pallas as pl
from jax.experimental.pallas import tpu as pltpu

D_MODEL = 1024
N_META = 16
N_HEADS = 4
QK_DIM = 64
V_DIM = 128
ROT_DIM = 16
ROPE_THETA = 500000.0
SSM_GROUP = 16
N_GROUPS = 32
SSM_STATE = 64
SSM_WIDTH = 512
ATTN_WIDTH = 512
D_FF = 2816
EPS = 1e-6
LAM_INIT = 0.8 - 0.6 * math.exp(-0.3 * 0)

LANES = 128
MXU_DIM = 256
CHUNK = 16
OCT = LANES // SSM_GROUP
N_OCT = N_GROUPS // OCT
OCT_IN = CHUNK * LANES
OCT_STATE = OCT * 2 * SSM_STATE
SCAN_PAD = 256

ROW_TILE = 512
Q_TILE = 512
KV_TILE = 512
VMEM_LIMIT = 56 * 1024 * 1024

F32 = jnp.float32
BF16 = jnp.bfloat16


def _rms(x, g):
    return x * lax.rsqrt(jnp.mean(x * x, axis=-1, keepdims=True) + EPS) * g


def _in_proj_kernel(x_ref, g_ref, w_ref, cos_ref, sa_ref, sb_ref, q_ref, k_ref, v_ref, u_ref):
    h = _rms(x_ref[...], g_ref[...]).astype(BF16)
    proj = jnp.dot(h, w_ref[...], preferred_element_type=F32)
    cos, sa, sb = cos_ref[...], sa_ref[...], sb_ref[...]

    def rope(t):
        return (t * cos + pltpu.roll(t, LANES - ROT_DIM // 2, axis=1) * sa
                + pltpu.roll(t, ROT_DIM // 2, axis=1) * sb)

    for c in range(ATTN_WIDTH // LANES):
        sl = slice(c * LANES, (c + 1) * LANES)
        q_ref[:, sl] = rope(proj[:, c * LANES:(c + 1) * LANES]).astype(BF16)
        k_ref[:, sl] = rope(proj[:, ATTN_WIDTH + c * LANES:ATTN_WIDTH + (c + 1) * LANES]).astype(BF16)
    v_ref[...] = proj[:, 2 * ATTN_WIDTH:3 * ATTN_WIDTH].astype(BF16)
    u_ref[...] = proj[:, 3 * ATTN_WIDTH:]


def _in_proj(x2d, g, w, cos, sa, sb, tm):
    n = x2d.shape[0]
    n_pos = cos.shape[0] // tm
    row = lambda i: (i, 0)
    pos = lambda i: (i % n_pos, 0)
    const = lambda i: (0, 0)
    return pl.pallas_call(
        _in_proj_kernel,
        grid=(n // tm,),
        in_specs=[
            pl.BlockSpec((tm, D_MODEL), row),
            pl.BlockSpec((1, D_MODEL), const),
            pl.BlockSpec((D_MODEL, 4 * ATTN_WIDTH), const),
            pl.BlockSpec((tm, LANES), pos),
            pl.BlockSpec((tm, LANES), pos),
            pl.BlockSpec((tm, LANES), pos),
        ],
        out_specs=[pl.BlockSpec((tm, ATTN_WIDTH), row)] * 4,
        out_shape=[jax.ShapeDtypeStruct((n, ATTN_WIDTH), BF16)] * 3
        + [jax.ShapeDtypeStruct((n, SSM_WIDTH), F32)],
        compiler_params=pltpu.CompilerParams(
            dimension_semantics=("arbitrary",), vmem_limit_bytes=VMEM_LIMIT),
        name="in_proj",
    )(x2d, g, w, cos, sa, sb)


def _attn_kernel(q_ref, k_ref, v_ref, km_ref, vm_ref, lq1_ref, lk1_ref, lq2_ref, lk2_ref, sg_ref,
                 o_ref, m_ref, l_ref, acc_ref):
    qi = pl.program_id(2)
    q = q_ref[...]
    lane = lax.broadcasted_iota(jnp.int32, q.shape, 1)
    zero = jnp.zeros_like(q)
    qs = (jnp.where(lane < QK_DIM, q, zero), jnp.where(lane >= QK_DIM, q, zero))
    nt = (((1,), (1,)), ((), ()))
    wide = (Q_TILE, LANES)

    def step(k, v, mask, first):
        n_col = k.shape[0] // LANES
        for i in range(2):
            s = lax.dot_general(qs[i], k, nt, preferred_element_type=F32)
            if mask is not None:
                s = jnp.where(mask, s, -jnp.inf)
            cols = [s[:, c * LANES:(c + 1) * LANES] for c in range(n_col)]
            m_tile = functools.reduce(jnp.maximum, cols)
            m_new = jnp.broadcast_to(jnp.max(m_tile, axis=1, keepdims=True), wide)
            if not first:
                m_old = m_ref[i]
                m_new = jnp.maximum(m_old, m_new)
                alpha = jnp.exp2(m_old - m_new)
            m_ref[i] = m_new
            ps = [jnp.exp2(c - m_new) for c in cols]
            l_tile = functools.reduce(jnp.add, ps)
            p = jnp.concatenate([t.astype(BF16) for t in ps], axis=1)
            pv = jnp.dot(p, v, preferred_element_type=F32)
            if first:
                l_ref[i] = l_tile
                acc_ref[i] = pv
            else:
                l_ref[i] = alpha * l_ref[i] + l_tile
                acc_ref[i] = alpha * acc_ref[i] + pv

    meta_mask = lax.broadcasted_iota(jnp.int32, wide, 1) < N_META
    step(km_ref[...], vm_ref[...], meta_mask, True)

    def body(j, carry):
        start = pl.multiple_of(j * KV_TILE, KV_TILE)
        step(k_ref[pl.ds(start, KV_TILE), :], v_ref[pl.ds(start, KV_TILE), :], None, False)
        return carry

    lax.fori_loop(0, qi, body, 0)

    start = pl.multiple_of(qi * KV_TILE, KV_TILE)
    causal = (lax.broadcasted_iota(jnp.int32, (Q_TILE, KV_TILE), 1)
              <= lax.broadcasted_iota(jnp.int32, (Q_TILE, KV_TILE), 0))
    step(k_ref[pl.ds(start, KV_TILE), :], v_ref[pl.ds(start, KV_TILE), :], causal, False)

    lam = (jnp.exp(jnp.sum(lq1_ref[...] * lk1_ref[...], axis=1, keepdims=True))
           - jnp.exp(jnp.sum(lq2_ref[...] * lk2_ref[...], axis=1, keepdims=True)) + LAM_INIT)
    l1 = jnp.sum(l_ref[0], axis=1, keepdims=True)
    l2 = jnp.sum(l_ref[1], axis=1, keepdims=True)
    o = acc_ref[0] / l1 - lam * (acc_ref[1] / l2)
    o_ref[...] = (_rms(o, sg_ref[...]) * (1.0 - LAM_INIT)).astype(BF16)


def _attention(q, k, v, km, vm, lq1, lk1, lq2, lk2, sg, bsz, seq):
    nq = seq // Q_TILE
    assert Q_TILE == KV_TILE
    qmap = lambda b, h, i: (b * nq + i, h)
    kvmap = lambda b, h, i: (b, h)
    metamap = lambda b, h, i: (0, h)
    const = lambda b, h, i: (0, 0)
    vec = pl.BlockSpec((1, QK_DIM), const)
    return pl.pallas_call(
        _attn_kernel,
        grid=(bsz, N_HEADS, nq),
        in_specs=[
            pl.BlockSpec((Q_TILE, LANES), qmap),
            pl.BlockSpec((seq, LANES), kvmap),
            pl.BlockSpec((seq, LANES), kvmap),
            pl.BlockSpec((LANES, LANES), metamap),
            pl.BlockSpec((LANES, LANES), metamap),
            vec, vec, vec, vec,
            pl.BlockSpec((1, V_DIM), const),
        ],
        out_specs=pl.BlockSpec((Q_TILE, LANES), qmap),
        out_shape=jax.ShapeDtypeStruct((bsz * seq, ATTN_WIDTH), BF16),
        scratch_shapes=[
            pltpu.VMEM((2, Q_TILE, LANES), F32),
            pltpu.VMEM((2, Q_TILE, LANES), F32),
            pltpu.VMEM((2, Q_TILE, V_DIM), F32),
        ],
        compiler_params=pltpu.CompilerParams(
            dimension_semantics=("arbitrary", "arbitrary", "arbitrary"), vmem_limit_bytes=VMEM_LIMIT),
        name="diff_attention",
    )(q, k, v, km, vm, lq1, lk1, lq2, lk2, sg)


def _ssm_kernel(u_ref, um_ref, tz_ref, bs_ref, cs_ref, ap_ref, d_ref, y_ref, ub_ref, buf_a, buf_b,
                *, n_chunks):
    n_steps = n_chunks.bit_length() - 1

    @pl.when((pl.program_id(0) == 0) & (pl.program_id(1) == 0))
    def _():
        buf_a[:SCAN_PAD, :] = jnp.zeros((SCAN_PAD, OCT_STATE), F32)
        buf_b[:SCAN_PAD, :] = jnp.zeros((SCAN_PAD, OCT_STATE), F32)

    def slab(ref, t, rows):
        return ref[pl.ds(t, rows, stride=CHUNK), :]

    for t in range(CHUNK):
        ub_ref[t // 2, :, (t % 2) * LANES:(t % 2 + 1) * LANES] = slab(u_ref, t, n_chunks).astype(BF16)

    z = jnp.dot(ub_ref[0], bs_ref[0, :MXU_DIM, :], preferred_element_type=F32)
    for tt in range(1, CHUNK // 2):
        z = z + jnp.dot(ub_ref[tt], bs_ref[0, tt * MXU_DIM:(tt + 1) * MXU_DIM, :], preferred_element_type=F32)
    buf_a[SCAN_PAD:, :] = z
    um = jnp.concatenate([jnp.broadcast_to(um_ref[t:t + 1, :], (8, LANES)) for t in range(CHUNK)], axis=1)
    zm = jnp.dot(um.astype(BF16), bs_ref[0], preferred_element_type=F32)[0:1, :]
    buf_a[SCAN_PAD - 1:SCAN_PAD, :] = zm
    buf_b[SCAN_PAD - 1:SCAN_PAD, :] = zm

    src, dst = buf_a, buf_b
    for step in range(n_steps):
        shift = 1 << step
        for pair in range(OCT // 2):
            re = slice(pair * 2 * LANES, pair * 2 * LANES + LANES)
            im = slice(pair * 2 * LANES + LANES, (pair + 1) * 2 * LANES)
            ar = ap_ref[0, step:step + 1, pair * LANES:(pair + 1) * LANES]
            ai = ap_ref[0, n_steps + step:n_steps + step + 1, pair * LANES:(pair + 1) * LANES]
            xr = src[SCAN_PAD:, re]
            xi = src[SCAN_PAD:, im]
            pr = src[SCAN_PAD - shift:SCAN_PAD - shift + n_chunks, re]
            pi = src[SCAN_PAD - shift:SCAN_PAD - shift + n_chunks, im]
            dst[SCAN_PAD:, re] = xr + ar * pr - ai * pi
            dst[SCAN_PAD:, im] = xi + ar * pi + ai * pr
        src, dst = dst, src

    start_state = src[SCAN_PAD - 1:SCAN_PAD - 1 + n_chunks, :].astype(BF16)
    for ss in range(CHUNK // 2):
        cols = slice(ss * MXU_DIM, (ss + 1) * MXU_DIM)
        acc = jnp.dot(start_state, cs_ref[0, :, cols], preferred_element_type=F32)
        for tt in range(ss + 1):
            acc = acc + jnp.dot(ub_ref[tt], tz_ref[0, ss - tt], preferred_element_type=F32)
        for half in range(2):
            t = 2 * ss + half
            y = acc[:, half * LANES:(half + 1) * LANES] + d_ref[0] * slab(u_ref, t, n_chunks)
            y_ref[pl.ds(t, n_chunks, stride=CHUNK), :] = 0.5 * y * (1.0 + lax.erf(y * (1.0 / math.sqrt(2.0))))


def _ssm(u, u_meta, tz, bs, cs, ap, dp, bsz, seq):
    n_chunks = seq // CHUNK
    assert n_chunks & (n_chunks - 1) == 0 and n_chunks // 2 <= SCAN_PAD and N_META == CHUNK
    n_steps = n_chunks.bit_length() - 1
    octet = lambda o, b: (o, 0, 0)
    scan_buf = pltpu.VMEM((SCAN_PAD + n_chunks, OCT_STATE), F32)
    return pl.pallas_call(
        functools.partial(_ssm_kernel, n_chunks=n_chunks),
        grid=(N_OCT, bsz),
        in_specs=[
            pl.BlockSpec((seq, LANES), lambda o, b: (b, o)),
            pl.BlockSpec((N_META, LANES), lambda o, b: (0, o)),
            pl.BlockSpec((1, CHUNK // 2, MXU_DIM, MXU_DIM), lambda o, b: (o, 0, 0, 0)),
            pl.BlockSpec((1, OCT_IN, OCT_STATE), octet),
            pl.BlockSpec((1, OCT_STATE, OCT_IN), octet),
            pl.BlockSpec((1, 2 * n_steps, OCT_STATE // 2), octet),
            pl.BlockSpec((1, 1, LANES), octet),
        ],
        out_specs=pl.BlockSpec((seq, LANES), lambda o, b: (b, o)),
        out_shape=jax.ShapeDtypeStruct((bsz * seq, SSM_WIDTH), F32),
        scratch_shapes=[pltpu.VMEM((CHUNK // 2, n_chunks, MXU_DIM), BF16), scan_buf, scan_buf],
        compiler_params=pltpu.CompilerParams(
            dimension_semantics=("arbitrary", "arbitrary"), vmem_limit_bytes=VMEM_LIMIT),
        name="s5_chunked",
    )(u, u_meta, tz, bs, cs, ap, dp)


def _ssm_weights(a_re, a_im, log_dt, b_re, b_im, c_re, c_im, d_skip, n_steps):
    dt = jnp.exp(log_dt)[:, None]
    lam_re, lam_im = a_re * dt, a_im * dt

    def power(n):
        n = n[:, None, None]
        mag = jnp.exp(n * lam_re)
        return mag * jnp.cos(n * lam_im), mag * jnp.sin(n * lam_im)

    ab_re, ab_im = jnp.exp(lam_re) * jnp.cos(lam_im), jnp.exp(lam_re) * jnp.sin(lam_im)
    den = a_re * a_re + a_im * a_im
    nr, ni = ab_re - 1.0, ab_im
    f_re = (nr * a_re + ni * a_im) / den
    f_im = (ni * a_re - nr * a_im) / den
    bb_re = f_re[..., None] * b_re - f_im[..., None] * b_im
    bb_im = f_re[..., None] * b_im + f_im[..., None] * b_re

    p_re, p_im = power(jnp.arange(CHUNK + 1, dtype=F32))
    ca_re = c_re[None] * p_re[:, :, None, :] - c_im[None] * p_im[:, :, None, :]
    ca_im = c_re[None] * p_im[:, :, None, :] + c_im[None] * p_re[:, :, None, :]
    hp = lax.Precision.HIGHEST
    kern = (jnp.einsum('ngip,gpj->ngij', ca_re[:CHUNK], bb_re, precision=hp)
            - jnp.einsum('ngip,gpj->ngij', ca_im[:CHUNK], bb_im, precision=hp))

    eye_o = jnp.eye(OCT, dtype=F32)
    eye_q = jnp.eye(OCT // 2, dtype=F32)
    eye_w = jnp.eye(2, dtype=F32)

    m_i = jnp.arange(CHUNK // 2)[:, None, None]
    t_i = jnp.arange(2)[None, :, None]
    s_i = jnp.arange(2)[None, None, :]
    lag = 2 * m_i + s_i - t_i
    k_o = kern.reshape(CHUNK, N_OCT, OCT, SSM_GROUP, SSM_GROUP)[jnp.maximum(lag, 0)]
    k_o = jnp.where((lag >= 0)[..., None, None, None, None], k_o, 0.0)
    tz = k_o.transpose(3, 0, 1, 4, 6, 2, 5)[:, :, :, :, :, :, None, :] * eye_o[None, None, None, :, None, None, :, None]
    tz = tz.reshape(N_OCT, CHUNK // 2, MXU_DIM, MXU_DIM)

    r_re, r_im = p_re[CHUNK - 1::-1], p_im[CHUNK - 1::-1]
    bs_re = r_re[..., None] * bb_re[None] - r_im[..., None] * bb_im[None]
    bs_im = r_re[..., None] * bb_im[None] + r_im[..., None] * bb_re[None]
    b5 = jnp.stack([bs_re, bs_im]).reshape(2, CHUNK, N_OCT, OCT // 2, 2, SSM_STATE, SSM_GROUP)
    bs = (b5.transpose(2, 1, 3, 4, 6, 0, 5)[:, :, :, :, :, None, :, None, :]
          * eye_q[None, None, :, None, None, :, None, None, None]
          * eye_w[None, None, None, :, None, None, None, :, None])
    bs = bs.reshape(N_OCT, OCT_IN, OCT_STATE)

    c5 = jnp.stack([ca_re[1:], -ca_im[1:]]).reshape(2, CHUNK, N_OCT, OCT // 2, 2, SSM_GROUP, SSM_STATE)
    cs = (c5.transpose(2, 3, 0, 4, 6, 1, 5)[:, :, :, :, :, :, None, None, :]
          * eye_q[None, :, None, None, None, None, :, None, None]
          * eye_w[None, None, None, :, None, None, None, :, None])
    cs = cs.reshape(N_OCT, OCT_STATE, OCT_IN)

    s_re, s_im = power(CHUNK * (2.0 ** jnp.arange(n_steps, dtype=F32)))
    pack_ap = lambda m: m.reshape(n_steps, N_OCT, OCT_STATE // 2).transpose(1, 0, 2)
    ap = jnp.concatenate([pack_ap(s_re), pack_ap(s_im)], axis=1)

    dp = d_skip.reshape(N_OCT, 1, LANES)
    return tz.astype(BF16), bs.astype(BF16), cs.astype(BF16), ap, dp


def _out_ffn_kernel(x_ref, o_ref, y_ref, wglu_ref, bglu_ref, sg_ref, wout_ref, pmg_ref, pfg_ref,
                    wg_ref, wu_ref, wd_ref, pog_ref, out_ref):
    y = y_ref[...]
    gate = jnp.dot(y.astype(BF16), wglu_ref[...], preferred_element_type=F32) + bglu_ref[...]
    y = _rms(y * jax.nn.sigmoid(gate), sg_ref[...]).astype(BF16)
    mix = (jnp.dot(o_ref[...], wout_ref[:ATTN_WIDTH, :], preferred_element_type=F32)
           + jnp.dot(y, wout_ref[ATTN_WIDTH:, :], preferred_element_type=F32))
    h1 = x_ref[...] + _rms(mix, pmg_ref[...])
    h2 = _rms(h1, pfg_ref[...]).astype(BF16)
    g = jnp.dot(h2, wg_ref[...], preferred_element_type=F32)
    up = jnp.dot(h2, wu_ref[...], preferred_element_type=F32)
    f = (g * jax.nn.sigmoid(g) * up).astype(BF16)
    f = jnp.dot(f, wd_ref[...], preferred_element_type=F32)
    out_ref[...] = h1 + _rms(f, pog_ref[...])


def _out_ffn(x2d, o, y, wglu, bglu, sg, wout, pmg, pfg, wg, wu, wd, pog, tm):
    n = x2d.shape[0]
    row = lambda i: (i, 0)
    const = lambda i: (0, 0)

    def resident(shape):
        return pl.BlockSpec(shape, const, pipeline_mode=pl.Buffered(1))

    return pl.pallas_call(
        _out_ffn_kernel,
        grid=(n // tm,),
        in_specs=[
            pl.BlockSpec((tm, D_MODEL), row),
            pl.BlockSpec((tm, ATTN_WIDTH), row),
            pl.BlockSpec((tm, SSM_WIDTH), row),
            resident((SSM_WIDTH, SSM_WIDTH)),
            resident((1, SSM_WIDTH)),
            resident((1, SSM_WIDTH)),
            resident((D_MODEL, D_MODEL)),
            resident((1, D_MODEL)),
            resident((1, D_MODEL)),
            resident((D_MODEL, D_FF)),
            resident((D_MODEL, D_FF)),
            resident((D_FF, D_MODEL)),
            resident((1, D_MODEL)),
        ],
        out_specs=pl.BlockSpec((tm, D_MODEL), row),
        out_shape=jax.ShapeDtypeStruct((n, D_MODEL), F32),
        compiler_params=pltpu.CompilerParams(
            dimension_semantics=("arbitrary",), vmem_limit_bytes=VMEM_LIMIT),
        name="out_ffn",
    )(x2d, o, y, wglu, bglu, sg, wout, pmg, pfg, wg, wu, wd, pog)


def _rope_tables(length):
    pos = jnp.arange(length, dtype=F32)
    inv_freq = ROPE_THETA ** (-jnp.arange(0, ROT_DIM, 2, dtype=F32) / ROT_DIM)
    ang = pos[:, None] * inv_freq[None, :]
    cos, sin = jnp.cos(ang), jnp.sin(ang)
    ones = jnp.ones((length, QK_DIM - ROT_DIM), F32)
    zeros = jnp.zeros((length, QK_DIM - ROT_DIM), F32)
    z8 = jnp.zeros_like(sin)
    cos_t = jnp.concatenate([cos, cos, ones], axis=1)
    sa_t = jnp.concatenate([-sin, z8, zeros], axis=1)
    sb_t = jnp.concatenate([z8, sin, zeros], axis=1)
    two = lambda t: jnp.concatenate([t, t], axis=1)
    return two(cos_t), two(sa_t), two(sb_t)


def _permute_w_in(w):
    qk = QK_DIM * N_HEADS
    scale = math.log2(math.e) / math.sqrt(QK_DIM)
    q1, q2, k1, k2 = (w[:, i * qk:(i + 1) * qk].reshape(D_MODEL, N_HEADS, QK_DIM) for i in range(4))
    qp = (jnp.concatenate([q1, q2], axis=2) * scale).reshape(D_MODEL, 2 * qk)
    kp = jnp.concatenate([k1, k2], axis=2).reshape(D_MODEL, 2 * qk)
    return jnp.concatenate([qp, kp, w[:, 4 * qk:]], axis=1).astype(BF16)


def kernel(x, meta, pre_mix_g, w_in, lambda_q1, lambda_k1, lambda_q2, lambda_k2, subln_g, a_re, a_im, log_dt,
           b_re, b_im, c_re, c_im, d_skip, w_glu, b_glu, ssm_out_g, w_out, post_mix_g, pre_ffn_g, w_gate,
           w_up, w_down, post_ffn_g):
    bsz, seq, _ = x.shape
    n = bsz * seq
    x2d = x.reshape(n, D_MODEL)
    row = lambda t: t[0].reshape(1, -1)

    cos, sa, sb = _rope_tables(N_META + seq)
    w_in_p = _permute_w_in(w_in[0])
    g0 = row(pre_mix_g)
    q, k, v, u = _in_proj(x2d, g0, w_in_p, cos[N_META:], sa[N_META:], sb[N_META:], ROW_TILE)
    _, k_m, v_m, u_m = _in_proj(meta, g0, w_in_p, cos[:N_META], sa[:N_META], sb[:N_META], N_META)

    pad_meta = lambda t: jnp.pad(t, ((0, LANES - N_META), (0, 0)))
    o = _attention(q, k, v, pad_meta(k_m), pad_meta(v_m), lambda_q1, lambda_k1, lambda_q2, lambda_k2,
                   row(subln_g), bsz, seq)

    n_steps = (seq // CHUNK).bit_length() - 1
    tz, bs, cs, ap, dp = _ssm_weights(a_re[0], a_im[0], log_dt[0], b_re[0], b_im[0], c_re[0], c_im[0],
                                      d_skip[0], n_steps)
    y = _ssm(u, u_m, tz, bs, cs, ap, dp, bsz, seq)

    out = _out_ffn(x2d, o, y, w_glu[0].astype(BF16), row(b_glu), row(ssm_out_g), w_out[0].astype(BF16),
                   row(post_mix_g), row(pre_ffn_g), w_gate[0].astype(BF16), w_up[0].astype(BF16),
                   w_down[0].astype(BF16), row(post_ffn_g), ROW_TILE)
    return out.reshape(bsz, seq, D_MODEL)
```

```python
import functools
import math

import jax
import jax.numpy as jnp
from jax import lax
from jax.experimental import pallas as pl
from jax.experimental.pallas import tpu as pltpu

D_MODEL = 1024
N_META = 16
N_HEADS = 4
QK_DIM = 64
V_DIM = 128
ROT_DIM = 16
ROPE_THETA = 500000.0
SSM_GROUP = 16
N_GROUPS = 32
SSM_STATE = 64
SSM_WIDTH = 512
ATTN_WIDTH = 512
D_FF = 2816
EPS = 1e-6
LAM_INIT = 0.8 - 0.6 * math.exp(-0.3 * 0)

LANES = 128
MXU_DIM = 256
CHUNK = 16
OCT = LANES // SSM_GROUP
N_OCT = N_GROUPS // OCT
OCT_IN = CHUNK * LANES
OCT_STATE = OCT * 2 * SSM_STATE
SCAN_PAD = 256

ROW_TILE = 512
Q_TILE = 512
KV_TILE = 512
VMEM_LIMIT = 56 * 1024 * 1024

F32 = jnp.float32
BF16 = jnp.bfloat16


def _rms(x, g):
    return x * lax.rsqrt(jnp.mean(x * x, axis=-1, keepdims=True) + EPS) * g


def _in_proj_kernel(x_ref, g_ref, w_ref, cos_ref, sa_ref, sb_ref, q_ref, k_ref, v_ref, u_ref):
    h = _rms(x_ref[...], g_ref[...]).astype(BF16)
    proj = jnp.dot(h, w_ref[...], preferred_element_type=F32)
    cos, sa, sb = cos_ref[...], sa_ref[...], sb_ref[...]

    def rope(t):
        return (t * cos + pltpu.roll(t, LANES - ROT_DIM // 2, axis=1) * sa
                + pltpu.roll(t, ROT_DIM // 2, axis=1) * sb)

    for c in range(ATTN_WIDTH // LANES):
        sl = slice(c * LANES, (c + 1) * LANES)
        q_ref[:, sl] = rope(proj[:, c * LANES:(c + 1) * LANES]).astype(BF16)
        k_ref[:, sl] = rope(proj[:, ATTN_WIDTH + c * LANES:ATTN_WIDTH + (c + 1) * LANES]).astype(BF16)
    v_ref[...] = proj[:, 2 * ATTN_WIDTH:3 * ATTN_WIDTH].astype(BF16)
    u_ref[...] = proj[:, 3 * ATTN_WIDTH:]


def _in_proj(x2d, g, w, cos, sa, sb, tm):
    n = x2d.shape[0]
    n_pos = cos.shape[0] // tm
    row = lambda i: (i, 0)
    pos = lambda i: (i % n_pos, 0)
    const = lambda i: (0, 0)
    return pl.pallas_call(
        _in_proj_kernel,
        grid=(n // tm,),
        in_specs=[
            pl.BlockSpec((tm, D_MODEL), row),
            pl.BlockSpec((1, D_MODEL), const),
            pl.BlockSpec((D_MODEL, 4 * ATTN_WIDTH), const),
            pl.BlockSpec((tm, LANES), pos),
            pl.BlockSpec((tm, LANES), pos),
            pl.BlockSpec((tm, LANES), pos),
        ],
        out_specs=[pl.BlockSpec((tm, ATTN_WIDTH), row)] * 4,
        out_shape=[jax.ShapeDtypeStruct((n, ATTN_WIDTH), BF16)] * 3
        + [jax.ShapeDtypeStruct((n, SSM_WIDTH), F32)],
        compiler_params=pltpu.CompilerParams(
            dimension_semantics=("arbitrary",), vmem_limit_bytes=VMEM_LIMIT),
        name="in_proj",
    )(x2d, g, w, cos, sa, sb)


def _attn_kernel(q_ref, k_ref, v_ref, km_ref, vm_ref, lq1_ref, lk1_ref, lq2_ref, lk2_ref, sg_ref,
                 o_ref, m_ref, l_ref, acc_ref, pa_ref, pb_ref):
    qi = pl.program_id(2)
    q = q_ref[...]
    lane = lax.broadcasted_iota(jnp.int32, q.shape, 1)
    zero = jnp.zeros_like(q)
    qs = (jnp.where(lane < QK_DIM, q, zero), jnp.where(lane >= QK_DIM, q, zero))
    nt = (((1,), (1,)), ((), ()))
    wide = (Q_TILE, LANES)

    def block(k, mask, p_out, prev):
        n_col = k.shape[0] // LANES
        for i in range(2):
            s = lax.dot_general(qs[i], k, nt, preferred_element_type=F32)
            if mask is not None:
                s = jnp.where(mask, s, -jnp.inf)
            cols = [s[:, c * LANES:(c + 1) * LANES] for c in range(n_col)]
            m_tile = functools.reduce(jnp.maximum, cols)
            m_new = jnp.broadcast_to(jnp.max(m_tile, axis=1, keepdims=True), wide)
            if prev is not None:
                m_old = m_ref[i]
                m_new = jnp.maximum(m_old, m_new)
                alpha = jnp.exp2(m_old - m_new)
            m_ref[i] = m_new
            ps = [jnp.exp2(c - m_new) for c in cols]
            l_tile = functools.reduce(jnp.add, ps)
            l_ref[i] = l_tile if prev is None else alpha * l_ref[i] + l_tile
            for c in range(n_col):
                p_out[i, :, c * LANES:(c + 1) * LANES] = ps[c].astype(BF16)
            if prev is not None:
                p_prev, v_prev = prev
                pv = jnp.dot(p_prev[i], v_prev, preferred_element_type=F32)
                acc_ref[i] = (acc_ref[i] + pv) * alpha

    def tile(ref, t):
        return ref[pl.ds(pl.multiple_of(t * KV_TILE, KV_TILE), KV_TILE), :]

    def finish(pending, p_meta):
        meta_mask = lax.broadcasted_iota(jnp.int32, wide, 1) < N_META
        block(km_ref[...], meta_mask, p_meta, pending)
        for i in range(2):
            acc_ref[i] = acc_ref[i] + jnp.dot(p_meta[i, :, :LANES], vm_ref[...], preferred_element_type=F32)

    acc_ref[...] = jnp.zeros(acc_ref.shape, F32)
    causal = (lax.broadcasted_iota(jnp.int32, (Q_TILE, KV_TILE), 1)
              <= lax.broadcasted_iota(jnp.int32, (Q_TILE, KV_TILE), 0))
    block(tile(k_ref, qi), causal, pa_ref, None)

    def pair(jj, carry):
        t0 = 2 * jj
        block(tile(k_ref, t0), None, pb_ref, (pa_ref, tile(v_ref, jnp.where(jj == 0, qi, t0 - 1))))
        block(tile(k_ref, t0 + 1), None, pa_ref, (pb_ref, tile(v_ref, t0)))
        return carry

    n_pairs = qi // 2
    lax.fori_loop(0, n_pairs, pair, 0)
    pend_a = jnp.where(n_pairs == 0, qi, 2 * n_pairs - 1)

    @pl.when(qi % 2 == 1)
    def _():
        block(tile(k_ref, qi - 1), None, pb_ref, (pa_ref, tile(v_ref, pend_a)))
        finish((pb_ref, tile(v_ref, qi - 1)), pa_ref)

    @pl.when(qi % 2 == 0)
    def _():
        finish((pa_ref, tile(v_ref, pend_a)), pb_ref)

    lam = (jnp.exp(jnp.sum(lq1_ref[...] * lk1_ref[...], axis=1, keepdims=True))
           - jnp.exp(jnp.sum(lq2_ref[...] * lk2_ref[...], axis=1, keepdims=True)) + LAM_INIT)
    l1 = jnp.sum(l_ref[0], axis=1, keepdims=True)
    l2 = jnp.sum(l_ref[1], axis=1, keepdims=True)
    o = acc_ref[0] / l1 - lam * (acc_ref[1] / l2)
    o_ref[...] = (_rms(o, sg_ref[...]) * (1.0 - LAM_INIT)).astype(BF16)


def _attention(q, k, v, km, vm, lq1, lk1, lq2, lk2, sg, bsz, seq):
    nq = seq // Q_TILE
    assert Q_TILE == KV_TILE
    qmap = lambda b, h, i: (b * nq + i, h)
    kvmap = lambda b, h, i: (b, h)
    metamap = lambda b, h, i: (0, h)
    const = lambda b, h, i: (0, 0)
    vec = pl.BlockSpec((1, QK_DIM), const)
    return pl.pallas_call(
        _attn_kernel,
        grid=(bsz, N_HEADS, nq),
        in_specs=[
            pl.BlockSpec((Q_TILE, LANES), qmap),
            pl.BlockSpec((seq, LANES), kvmap),
            pl.BlockSpec((seq, LANES), kvmap),
            pl.BlockSpec((LANES, LANES), metamap),
            pl.BlockSpec((LANES, LANES), metamap),
            vec, vec, vec, vec,
            pl.BlockSpec((1, V_DIM), const),
        ],
        out_specs=pl.BlockSpec((Q_TILE, LANES), qmap),
        out_shape=jax.ShapeDtypeStruct((bsz * seq, ATTN_WIDTH), BF16),
        scratch_shapes=[
            pltpu.VMEM((2, Q_TILE, LANES), F32),
            pltpu.VMEM((2, Q_TILE, LANES), F32),
            pltpu.VMEM((2, Q_TILE, V_DIM), F32),
            pltpu.VMEM((2, Q_TILE, KV_TILE), BF16),
            pltpu.VMEM((2, Q_TILE, KV_TILE), BF16),
        ],
        compiler_params=pltpu.CompilerParams(
            dimension_semantics=("arbitrary", "arbitrary", "arbitrary"), vmem_limit_bytes=VMEM_LIMIT),
        name="diff_attention",
    )(q, k, v, km, vm, lq1, lk1, lq2, lk2, sg)


def _ssm_kernel(u_ref, um_ref, kc_ref, bc_ref, cc_ref, ap_ref, d_ref, y_ref,
                tz_ref, bs_ref, cs_ref, ub_ref, buf_a, buf_b, *, n_chunks):
    n_steps = n_chunks.bit_length() - 1

    @pl.when((pl.program_id(0) == 0) & (pl.program_id(1) == 0))
    def _():
        buf_a[:SCAN_PAD, :] = jnp.zeros((SCAN_PAD, OCT_STATE), F32)
        buf_b[:SCAN_PAD, :] = jnp.zeros((SCAN_PAD, OCT_STATE), F32)

    @pl.when(pl.program_id(1) == 0)
    def _():
        def group_of(shape, axis):
            return (lax.broadcasted_iota(jnp.int32, shape, axis) // SSM_GROUP) % OCT

        r = lax.broadcasted_iota(jnp.int32, (LANES, MXU_DIM), 0)
        c = lax.broadcasted_iota(jnp.int32, (LANES, MXU_DIM), 1)
        spread = jnp.where((r < 2 * SSM_GROUP) & (r // SSM_GROUP == c // LANES)
                           & (r % SSM_GROUP == c % SSM_GROUP), 1.0, 0.0).astype(BF16)
        same = group_of((MXU_DIM, MXU_DIM), 0) == group_of((MXU_DIM, MXU_DIM), 1)
        for m in range(CHUNK // 2):
            full = jnp.dot(kc_ref[0, m].astype(BF16), spread, preferred_element_type=F32)
            tz_ref[m] = jnp.where(same, full, 0.0).astype(BF16)
        bc = bc_ref[:, 0].reshape(OCT_IN, 4 * SSM_STATE)
        g_row = group_of(bc.shape, 0)
        w_col = (lax.broadcasted_iota(jnp.int32, bc.shape, 1) // SSM_STATE) % 2
        for qq in range(OCT // 2):
            bs_ref[:, qq * MXU_DIM:(qq + 1) * MXU_DIM] = jnp.where(g_row == 2 * qq + w_col, bc, 0.0).astype(BF16)
        g_col = group_of((SSM_STATE, OCT_IN), 1)
        for qq in range(OCT // 2):
            for part in range(2):
                for w in range(2):
                    r0 = qq * MXU_DIM + part * LANES + w * SSM_STATE
                    cs_ref[r0:r0 + SSM_STATE, :] = jnp.where(
                        g_col == 2 * qq + w, cc_ref[0, part * SSM_STATE:(part + 1) * SSM_STATE, :], 0.0).astype(BF16)

    def slab(ref, t, rows):
        return ref[pl.ds(t, rows, stride=CHUNK), :]

    for t in range(CHUNK):
        ub_ref[t // 2, :, (t % 2) * LANES:(t % 2 + 1) * LANES] = slab(u_ref, t, n_chunks).astype(BF16)

    z = jnp.dot(ub_ref[0], bs_ref[:MXU_DIM, :], preferred_element_type=F32)
    for tt in range(1, CHUNK // 2):
        z = z + jnp.dot(ub_ref[tt], bs_ref[tt * MXU_DIM:(tt + 1) * MXU_DIM, :], preferred_element_type=F32)
    buf_a[SCAN_PAD:, :] = z
    um = jnp.concatenate([jnp.broadcast_to(um_ref[t:t + 1, :], (8, LANES)) for t in range(CHUNK)], axis=1)
    zm = jnp.dot(um.astype(BF16), bs_ref[...], preferred_element_type=F32)[0:1, :]
    buf_a[SCAN_PAD - 1:SCAN_PAD, :] = zm
    buf_b[SCAN_PAD - 1:SCAN_PAD, :] = zm

    src, dst = buf_a, buf_b
    for step in range(n_steps):
        shift = 1 << step
        for pair in range(OCT // 2):
            re = slice(pair * 2 * LANES, pair * 2 * LANES + LANES)
            im = slice(pair * 2 * LANES + LANES, (pair + 1) * 2 * LANES)
            ar = ap_ref[0, step:step + 1, pair * LANES:(pair + 1) * LANES]
            ai = ap_ref[0, n_steps + step:n_steps + step + 1, pair * LANES:(pair + 1) * LANES]
            xr = src[SCAN_PAD:, re]
            xi = src[SCAN_PAD:, im]
            pr = src[SCAN_PAD - shift:SCAN_PAD - shift + n_chunks, re]
            pi = src[SCAN_PAD - shift:SCAN_PAD - shift + n_chunks, im]
            dst[SCAN_PAD:, re] = xr + ar * pr - ai * pi
            dst[SCAN_PAD:, im] = xi + ar * pi + ai * pr
        src, dst = dst, src

    start_state = src[SCAN_PAD - 1:SCAN_PAD - 1 + n_chunks, :].astype(BF16)
    for ss in range(CHUNK // 2):
        cols = slice(ss * MXU_DIM, (ss + 1) * MXU_DIM)
        acc = jnp.dot(start_state, cs_ref[:, cols], preferred_element_type=F32)
        for tt in range(ss + 1):
            acc = acc + jnp.dot(ub_ref[tt], tz_ref[ss - tt], preferred_element_type=F32)
        for half in range(2):
            t = 2 * ss + half
            y = acc[:, half * LANES:(half + 1) * LANES] + d_ref[0] * slab(u_ref, t, n_chunks)
            y_ref[pl.ds(t, n_chunks, stride=CHUNK), :] = 0.5 * y * (1.0 + lax.erf(y * (1.0 / math.sqrt(2.0))))


def _ssm(u, u_meta, kc, bc, cc, ap, dp, bsz, seq):
    n_chunks = seq // CHUNK
    assert n_chunks & (n_chunks - 1) == 0 and n_chunks // 2 <= SCAN_PAD and N_META == CHUNK
    n_steps = n_chunks.bit_length() - 1
    octet = lambda o, b: (o, 0, 0)
    scan_buf = pltpu.VMEM((SCAN_PAD + n_chunks, OCT_STATE), F32)
    return pl.pallas_call(
        functools.partial(_ssm_kernel, n_chunks=n_chunks),
        grid=(N_OCT, bsz),
        in_specs=[
            pl.BlockSpec((seq, LANES), lambda o, b: (b, o)),
            pl.BlockSpec((N_META, LANES), lambda o, b: (0, o)),
            pl.BlockSpec((1, CHUNK // 2, MXU_DIM, LANES), lambda o, b: (o, 0, 0, 0)),
            pl.BlockSpec((CHUNK, 1, LANES, 4 * SSM_STATE), lambda o, b: (0, o, 0, 0)),
            pl.BlockSpec((1, 2 * SSM_STATE, OCT_IN), octet),
            pl.BlockSpec((1, 2 * n_steps, OCT_STATE // 2), octet),
            pl.BlockSpec((1, 1, LANES), octet),
        ],
        out_specs=pl.BlockSpec((seq, LANES), lambda o, b: (b, o)),
        out_shape=jax.ShapeDtypeStruct((bsz * seq, SSM_WIDTH), F32),
        scratch_shapes=[
            pltpu.VMEM((CHUNK // 2, MXU_DIM, MXU_DIM), BF16),
            pltpu.VMEM((OCT_IN, OCT_STATE), BF16),
            pltpu.VMEM((OCT_STATE, OCT_IN), BF16),
            pltpu.VMEM((CHUNK // 2, n_chunks, MXU_DIM), BF16), scan_buf, scan_buf],
        compiler_params=pltpu.CompilerParams(
            dimension_semantics=("arbitrary", "arbitrary"), vmem_limit_bytes=VMEM_LIMIT),
        name="s5_chunked",
    )(u, u_meta, kc, bc, cc, ap, dp)


def _ssm_weights(a_re, a_im, log_dt, b_re, b_im, c_re, c_im, d_skip, n_steps):
    dt = jnp.exp(log_dt)[:, None]
    lam_re, lam_im = a_re * dt, a_im * dt

    def power(n):
        n = n[:, None, None]
        mag = jnp.exp(n * lam_re)
        return mag * jnp.cos(n * lam_im), mag * jnp.sin(n * lam_im)

    ab_re, ab_im = jnp.exp(lam_re) * jnp.cos(lam_im), jnp.exp(lam_re) * jnp.sin(lam_im)
    den = a_re * a_re + a_im * a_im
    nr, ni = ab_re - 1.0, ab_im
    f_re = (nr * a_re + ni * a_im) / den
    f_im = (ni * a_re - nr * a_im) / den
    bb_re = f_re[..., None] * b_re - f_im[..., None] * b_im
    bb_im = f_re[..., None] * b_im + f_im[..., None] * b_re

    p_re, p_im = power(jnp.arange(CHUNK + 1, dtype=F32))

    ct_re, ct_im = c_re.transpose(2, 0, 1)[:, None], c_im.transpose(2, 0, 1)[:, None]
    pt_re, pt_im = p_re.transpose(2, 0, 1)[..., None], p_im.transpose(2, 0, 1)[..., None]
    ca_re = ct_re * pt_re - ct_im * pt_im
    ca_im = ct_re * pt_im + ct_im * pt_re
    cc = jnp.stack([ca_re[:, 1:], -ca_im[:, 1:]]).reshape(2 * SSM_STATE, CHUNK, N_OCT, LANES)
    cc = cc.transpose(2, 0, 1, 3).reshape(N_OCT, 2 * SSM_STATE, OCT_IN)

    hp = lax.Precision.HIGHEST
    kern = (jnp.einsum('pngi,gpj->ngji', ca_re[:, :CHUNK], bb_re, precision=hp)
            - jnp.einsum('pngi,gpj->ngji', ca_im[:, :CHUNK], bb_im, precision=hp))
    m_i = jnp.arange(CHUNK // 2)[:, None, None]
    t_i = jnp.arange(2)[None, :, None]
    s_i = jnp.arange(2)[None, None, :]
    lag = 2 * m_i + s_i - t_i
    k_o = kern.reshape(CHUNK, N_OCT, OCT, SSM_GROUP, SSM_GROUP)[jnp.maximum(lag, 0)]
    k_o = jnp.where((lag >= 0)[..., None, None, None, None], k_o, 0.0)
    kc = k_o.transpose(3, 0, 1, 4, 5, 2, 6).reshape(N_OCT, CHUNK // 2, MXU_DIM, 2 * SSM_GROUP)
    kc = jnp.pad(kc, ((0, 0), (0, 0), (0, 0), (0, LANES - 2 * SSM_GROUP)))

    r_re, r_im = p_re[CHUNK - 1::-1][:, :, None, :], p_im[CHUNK - 1::-1][:, :, None, :]
    bt_re, bt_im = bb_re.transpose(0, 2, 1)[None], bb_im.transpose(0, 2, 1)[None]
    bs_re = r_re * bt_re - r_im * bt_im
    bs_im = r_re * bt_im + r_im * bt_re
    bc = jnp.concatenate([bs_re, bs_re, bs_im, bs_im], axis=-1).reshape(CHUNK, N_OCT, LANES, 4 * SSM_STATE)

    s_re, s_im = power(CHUNK * (2.0 ** jnp.arange(n_steps, dtype=F32)))
    pack_ap = lambda m: m.reshape(n_steps, N_OCT, OCT_STATE // 2).transpose(1, 0, 2)
    ap = jnp.concatenate([pack_ap(s_re), pack_ap(s_im)], axis=1)

    dp = d_skip.reshape(N_OCT, 1, LANES)
    return kc, bc, cc, ap, dp


def _out_ffn_kernel(x_ref, o_ref, y_ref, wglu_ref, bglu_ref, sg_ref, wout_ref, pmg_ref, pfg_ref,
                    wg_ref, wu_ref, wd_ref, pog_ref, out_ref):
    y = y_ref[...]
    gate = jnp.dot(y.astype(BF16), wglu_ref[...], preferred_element_type=F32) + bglu_ref[...]
    y = _rms(y * jax.nn.sigmoid(gate), sg_ref[...]).astype(BF16)
    mix = (jnp.dot(o_ref[...], wout_ref[:ATTN_WIDTH, :], preferred_element_type=F32)
           + jnp.dot(y, wout_ref[ATTN_WIDTH:, :], preferred_element_type=F32))
    h1 = x_ref[...] + _rms(mix, pmg_ref[...])
    h2 = _rms(h1, pfg_ref[...]).astype(BF16)
    g = jnp.dot(h2, wg_ref[...], preferred_element_type=F32)
    up = jnp.dot(h2, wu_ref[...], preferred_element_type=F32)
    f = (g * jax.nn.sigmoid(g) * up).astype(BF16)
    f = jnp.dot(f, wd_ref[...], preferred_element_type=F32)
    out_ref[...] = h1 + _rms(f, pog_ref[...])


def _out_ffn(x2d, o, y, wglu, bglu, sg, wout, pmg, pfg, wg, wu, wd, pog, tm):
    n = x2d.shape[0]
    row = lambda i: (i, 0)
    const = lambda i: (0, 0)

    def resident(shape):
        return pl.BlockSpec(shape, const, pipeline_mode=pl.Buffered(1))

    return pl.pallas_call(
        _out_ffn_kernel,
        grid=(n // tm,),
        in_specs=[
            pl.BlockSpec((tm, D_MODEL), row),
            pl.BlockSpec((tm, ATTN_WIDTH), row),
            pl.BlockSpec((tm, SSM_WIDTH), row),
            resident((SSM_WIDTH, SSM_WIDTH)),
            resident((1, SSM_WIDTH)),
            resident((1, SSM_WIDTH)),
            resident((D_MODEL, D_MODEL)),
            resident((1, D_MODEL)),
            resident((1, D_MODEL)),
            resident((D_MODEL, D_FF)),
            resident((D_MODEL, D_FF)),
            resident((D_FF, D_MODEL)),
            resident((1, D_MODEL)),
        ],
        out_specs=pl.BlockSpec((tm, D_MODEL), row),
        out_shape=jax.ShapeDtypeStruct((n, D_MODEL), F32),
        compiler_params=pltpu.CompilerParams(
            dimension_semantics=("arbitrary",), vmem_limit_bytes=VMEM_LIMIT),
        name="out_ffn",
    )(x2d, o, y, wglu, bglu, sg, wout, pmg, pfg, wg, wu, wd, pog)


def _rope_tables(start, length):
    half = ROT_DIM // 2
    pos = jnp.arange(start, start + length, dtype=F32)
    inv_freq = ROPE_THETA ** (-jnp.arange(0, ROT_DIM, 2, dtype=F32) / ROT_DIM)
    d = jnp.arange(LANES) % QK_DIM
    freq = jnp.where(d < ROT_DIM, inv_freq[d % half], 0.0)
    ang = pos[:, None] * freq[None, :]
    sin = jnp.sin(ang)
    sa = jnp.where(d < half, -1.0, 0.0)
    sb = jnp.where((d >= half) & (d < ROT_DIM), 1.0, 0.0)
    return jnp.cos(ang), sin * sa[None, :], sin * sb[None, :]


def _permute_w_in(w):
    qk = QK_DIM * N_HEADS
    scale = math.log2(math.e) / math.sqrt(QK_DIM)
    q1, q2, k1, k2 = (w[:, i * qk:(i + 1) * qk].reshape(D_MODEL, N_HEADS, QK_DIM) for i in range(4))
    qp = (jnp.concatenate([q1, q2], axis=2) * scale).reshape(D_MODEL, 2 * qk)
    kp = jnp.concatenate([k1, k2], axis=2).reshape(D_MODEL, 2 * qk)
    return jnp.concatenate([qp, kp, w[:, 4 * qk:]], axis=1).astype(BF16)


def kernel(x, meta, pre_mix_g, w_in, lambda_q1, lambda_k1, lambda_q2, lambda_k2, subln_g, a_re, a_im, log_dt,
           b_re, b_im, c_re, c_im, d_skip, w_glu, b_glu, ssm_out_g, w_out, post_mix_g, pre_ffn_g, w_gate,
           w_up, w_down, post_ffn_g):
    bsz, seq, _ = x.shape
    n = bsz * seq
    x2d = x.reshape(n, D_MODEL)
    row = lambda t: t[0].reshape(1, -1)

    w_in_p = _permute_w_in(w_in[0])
    g0 = row(pre_mix_g)
    q, k, v, u = _in_proj(x2d, g0, w_in_p, *_rope_tables(N_META, seq), ROW_TILE)
    _, k_m, v_m, u_m = _in_proj(meta, g0, w_in_p, *_rope_tables(0, N_META), N_META)

    pad_meta = lambda t: jnp.pad(t, ((0, LANES - N_META), (0, 0)))
    o = _attention(q, k, v, pad_meta(k_m), pad_meta(v_m), lambda_q1, lambda_k1, lambda_q2, lambda_k2,
                   row(subln_g), bsz, seq)

    n_steps = (seq // CHUNK).bit_length() - 1
    ssm_w = _ssm_weights(a_re[0], a_im[0], log_dt[0], b_re[0], b_im[0], c_re[0], c_im[0], d_skip[0], n_steps)
    y = _ssm(u, u_m, *ssm_w, bsz, seq)

    out = _out_ffn(x2d, o, y, w_glu[0].astype(BF16), row(b_glu), row(ssm_out_g), w_out[0].astype(BF16),
                   row(post_mix_g), row(pre_ffn_g), w_gate[0].astype(BF16), w_up[0].astype(BF16),
                   w_down[0].astype(BF16), row(post_ffn_g), ROW_TILE)
    return out.reshape(bsz, seq, D_MODEL)
```

```python
import functools
import math

import jax
import jax.numpy as jnp
from jax import lax
from jax.experimental import pallas as pl
from jax.experimental.pallas import tpu as pltpu

D_MODEL = 1024
N_META = 16
N_HEADS = 4
QK_DIM = 64
V_DIM = 128
ROT_DIM = 16
ROPE_THETA = 500000.0
SSM_GROUP = 16
N_GROUPS = 32
SSM_STATE = 64
SSM_WIDTH = 512
ATTN_WIDTH = 512
D_FF = 2816
EPS = 1e-6
LAM_INIT = 0.8 - 0.6 * math.exp(-0.3 * 0)

LANES = 128
MXU_DIM = 256
CHUNK = 16
OCT = LANES // SSM_GROUP
N_OCT = N_GROUPS // OCT
OCT_IN = CHUNK * LANES
OCT_STATE = OCT * 2 * SSM_STATE
SCAN_PAD = 256

ROW_TILE = 512
Q_TILE = 512
KV_TILE = 512
VMEM_LIMIT = 56 * 1024 * 1024

F32 = jnp.float32
BF16 = jnp.bfloat16


def _rms(x, g):
    return x * lax.rsqrt(jnp.mean(x * x, axis=-1, keepdims=True) + EPS) * g


def _in_proj_kernel(x_ref, g_ref, w_ref, cos_ref, sa_ref, sb_ref, q_ref, k_ref, v_ref, u_ref):
    h = _rms(x_ref[...], g_ref[...]).astype(BF16)
    proj = jnp.dot(h, w_ref[...], preferred_element_type=F32)
    cos, sa, sb = cos_ref[...], sa_ref[...], sb_ref[...]

    def rope(t):
        return (t * cos + pltpu.roll(t, LANES - ROT_DIM // 2, axis=1) * sa
                + pltpu.roll(t, ROT_DIM // 2, axis=1) * sb)

    for c in range(ATTN_WIDTH // LANES):
        sl = slice(c * LANES, (c + 1) * LANES)
        q_ref[:, sl] = rope(proj[:, c * LANES:(c + 1) * LANES]).astype(BF16)
        k_ref[:, sl] = rope(proj[:, ATTN_WIDTH + c * LANES:ATTN_WIDTH + (c + 1) * LANES]).astype(BF16)
    v_ref[...] = proj[:, 2 * ATTN_WIDTH:3 * ATTN_WIDTH].astype(BF16)
    u_ref[...] = proj[:, 3 * ATTN_WIDTH:]


def _in_proj(x2d, g, w, cos, sa, sb, tm):
    n = x2d.shape[0]
    n_pos = cos.shape[0] // tm
    row = lambda i: (i, 0)
    pos = lambda i: (i % n_pos, 0)
    const = lambda i: (0, 0)
    return pl.pallas_call(
        _in_proj_kernel,
        grid=(n // tm,),
        in_specs=[
            pl.BlockSpec((tm, D_MODEL), row),
            pl.BlockSpec((1, D_MODEL), const),
            pl.BlockSpec((D_MODEL, 4 * ATTN_WIDTH), const),
            pl.BlockSpec((tm, LANES), pos),
            pl.BlockSpec((tm, LANES), pos),
            pl.BlockSpec((tm, LANES), pos),
        ],
        out_specs=[pl.BlockSpec((tm, ATTN_WIDTH), row)] * 4,
        out_shape=[jax.ShapeDtypeStruct((n, ATTN_WIDTH), BF16)] * 3
        + [jax.ShapeDtypeStruct((n, SSM_WIDTH), F32)],
        compiler_params=pltpu.CompilerParams(
            dimension_semantics=("arbitrary",), vmem_limit_bytes=VMEM_LIMIT),
        name="in_proj",
    )(x2d, g, w, cos, sa, sb)


def _attn_kernel(q_ref, k_ref, v_ref, km_ref, vm_ref, lq1_ref, lk1_ref, lq2_ref, lk2_ref, sg_ref,
                 o_ref, m_ref, l_ref, acc_ref, pa_ref, pb_ref):
    qi = pl.program_id(2)
    q = q_ref[...]
    lane = lax.broadcasted_iota(jnp.int32, q.shape, 1)
    zero = jnp.zeros_like(q)
    qs = (jnp.where(lane < QK_DIM, q, zero), jnp.where(lane >= QK_DIM, q, zero))
    nt = (((1,), (1,)), ((), ()))
    wide = (Q_TILE, LANES)

    def block(k, mask, p_out, prev):
        n_col = k.shape[0] // LANES
        for i in range(2):
            s = lax.dot_general(qs[i], k, nt, preferred_element_type=F32)
            if mask is not None:
                s = jnp.where(mask, s, -jnp.inf)
            cols = [s[:, c * LANES:(c + 1) * LANES] for c in range(n_col)]
            m_tile = functools.reduce(jnp.maximum, cols)
            m_new = jnp.broadcast_to(jnp.max(m_tile, axis=1, keepdims=True), wide)
            if prev is not None:
                m_old = m_ref[i]
                m_new = jnp.maximum(m_old, m_new)
                alpha = jnp.exp2(m_old - m_new)
            m_ref[i] = m_new
            ps = [jnp.exp2(c - m_new) for c in cols]
            l_tile = functools.reduce(jnp.add, ps)
            l_ref[i] = l_tile if prev is None else alpha * l_ref[i] + l_tile
            for c in range(n_col):
                p_out[i, :, c * LANES:(c + 1) * LANES] = ps[c].astype(BF16)
            if prev is not None:
                p_prev, v_prev = prev
                pv = jnp.dot(p_prev[i], v_prev, preferred_element_type=F32)
                acc_ref[i] = (acc_ref[i] + pv) * alpha

    def tile(ref, t):
        return ref[pl.ds(pl.multiple_of(t * KV_TILE, KV_TILE), KV_TILE), :]

    def finish(pending, p_meta):
        meta_mask = lax.broadcasted_iota(jnp.int32, wide, 1) < N_META
        block(km_ref[...], meta_mask, p_meta, pending)
        for i in range(2):
            acc_ref[i] = acc_ref[i] + jnp.dot(p_meta[i, :, :LANES], vm_ref[...], preferred_element_type=F32)

    acc_ref[...] = jnp.zeros(acc_ref.shape, F32)
    causal = (lax.broadcasted_iota(jnp.int32, (Q_TILE, KV_TILE), 1)
              <= lax.broadcasted_iota(jnp.int32, (Q_TILE, KV_TILE), 0))
    block(tile(k_ref, qi), causal, pa_ref, None)

    def pair(jj, carry):
        t0 = 2 * jj
        block(tile(k_ref, t0), None, pb_ref, (pa_ref, tile(v_ref, jnp.where(jj == 0, qi, t0 - 1))))
        block(tile(k_ref, t0 + 1), None, pa_ref, (pb_ref, tile(v_ref, t0)))
        return carry

    n_pairs = qi // 2
    lax.fori_loop(0, n_pairs, pair, 0)
    pend_a = jnp.where(n_pairs == 0, qi, 2 * n_pairs - 1)

    @pl.when(qi % 2 == 1)
    def _():
        block(tile(k_ref, qi - 1), None, pb_ref, (pa_ref, tile(v_ref, pend_a)))
        finish((pb_ref, tile(v_ref, qi - 1)), pa_ref)

    @pl.when(qi % 2 == 0)
    def _():
        finish((pa_ref, tile(v_ref, pend_a)), pb_ref)

    lam = (jnp.exp(jnp.sum(lq1_ref[...] * lk1_ref[...], axis=1, keepdims=True))
           - jnp.exp(jnp.sum(lq2_ref[...] * lk2_ref[...], axis=1, keepdims=True)) + LAM_INIT)
    l1 = jnp.sum(l_ref[0], axis=1, keepdims=True)
    l2 = jnp.sum(l_ref[1], axis=1, keepdims=True)
    o = acc_ref[0] / l1 - lam * (acc_ref[1] / l2)
    o_ref[...] = (_rms(o, sg_ref[...]) * (1.0 - LAM_INIT)).astype(BF16)


def _attention(q, k, v, km, vm, lq1, lk1, lq2, lk2, sg, bsz, seq):
    nq = seq // Q_TILE
    assert Q_TILE == KV_TILE
    qmap = lambda b, h, i: (b * nq + i, h)
    kvmap = lambda b, h, i: (b, h)
    metamap = lambda b, h, i: (0, h)
    const = lambda b, h, i: (0, 0)
    vec = pl.BlockSpec((1, QK_DIM), const)
    return pl.pallas_call(
        _attn_kernel,
        grid=(bsz, N_HEADS, nq),
        in_specs=[
            pl.BlockSpec((Q_TILE, LANES), qmap),
            pl.BlockSpec((seq, LANES), kvmap),
            pl.BlockSpec((seq, LANES), kvmap),
            pl.BlockSpec((LANES, LANES), metamap),
            pl.BlockSpec((LANES, LANES), metamap),
            vec, vec, vec, vec,
            pl.BlockSpec((1, V_DIM), const),
        ],
        out_specs=pl.BlockSpec((Q_TILE, LANES), qmap),
        out_shape=jax.ShapeDtypeStruct((bsz * seq, ATTN_WIDTH), BF16),
        scratch_shapes=[
            pltpu.VMEM((2, Q_TILE, LANES), F32),
            pltpu.VMEM((2, Q_TILE, LANES), F32),
            pltpu.VMEM((2, Q_TILE, V_DIM), F32),
            pltpu.VMEM((2, Q_TILE, KV_TILE), BF16),
            pltpu.VMEM((2, Q_TILE, KV_TILE), BF16),
        ],
        compiler_params=pltpu.CompilerParams(
            dimension_semantics=("arbitrary", "arbitrary", "arbitrary"), vmem_limit_bytes=VMEM_LIMIT),
        name="diff_attention",
    )(q, k, v, km, vm, lq1, lk1, lq2, lk2, sg)


def _ssm_kernel(u_ref, um_ref, kc_ref, bc_ref, ccr_ref, cci_ref, ap_ref, d_ref, y_ref,
                tz_ref, bs_ref, cs_ref, ub_ref, buf_a, buf_b, *, n_chunks):
    n_steps = n_chunks.bit_length() - 1

    @pl.when((pl.program_id(0) == 0) & (pl.program_id(1) == 0))
    def _():
        buf_a[:SCAN_PAD, :] = jnp.zeros((SCAN_PAD, OCT_STATE), F32)
        buf_b[:SCAN_PAD, :] = jnp.zeros((SCAN_PAD, OCT_STATE), F32)

    @pl.when(pl.program_id(1) == 0)
    def _():
        def group_of(shape, axis):
            return (lax.broadcasted_iota(jnp.int32, shape, axis) // SSM_GROUP) % OCT

        same = group_of((LANES, LANES), 0) == group_of((LANES, LANES), 1)
        lag_blk = [jnp.where(same, kc_ref[n, 0], 0.0).astype(BF16) for n in range(CHUNK)]
        for m in range(CHUNK // 2):
            tz_ref[m, :LANES, :LANES] = lag_blk[2 * m]
            tz_ref[m, :LANES, LANES:] = lag_blk[2 * m + 1]
            tz_ref[m, LANES:, :LANES] = lag_blk[2 * m - 1] if m else jnp.zeros((LANES, LANES), BF16)
            tz_ref[m, LANES:, LANES:] = lag_blk[2 * m]
        bc = bc_ref[:, 0].reshape(OCT_IN, 4 * SSM_STATE)
        g_row = group_of(bc.shape, 0)
        w_col = (lax.broadcasted_iota(jnp.int32, bc.shape, 1) // SSM_STATE) % 2
        for qq in range(OCT // 2):
            bs_ref[:, qq * MXU_DIM:(qq + 1) * MXU_DIM] = jnp.where(g_row == 2 * qq + w_col, bc, 0.0).astype(BF16)
        g_col = group_of((SSM_STATE, LANES), 1)
        for s in range(CHUNK):
            c_s = (ccr_ref[s + 1], -cci_ref[s + 1])
            for qq in range(OCT // 2):
                for part in range(2):
                    for w in range(2):
                        r0 = qq * MXU_DIM + part * LANES + w * SSM_STATE
                        cs_ref[r0:r0 + SSM_STATE, s * LANES:(s + 1) * LANES] = jnp.where(
                            g_col == 2 * qq + w, c_s[part], 0.0).astype(BF16)

    def slab(ref, t, rows):
        return ref[pl.ds(t, rows, stride=CHUNK), :]

    for t in range(CHUNK):
        ub_ref[t // 2, :, (t % 2) * LANES:(t % 2 + 1) * LANES] = slab(u_ref, t, n_chunks).astype(BF16)

    z = jnp.dot(ub_ref[0], bs_ref[:MXU_DIM, :], preferred_element_type=F32)
    for tt in range(1, CHUNK // 2):
        z = z + jnp.dot(ub_ref[tt], bs_ref[tt * MXU_DIM:(tt + 1) * MXU_DIM, :], preferred_element_type=F32)
    buf_a[SCAN_PAD:, :] = z
    um = jnp.concatenate([jnp.broadcast_to(um_ref[t:t + 1, :], (8, LANES)) for t in range(CHUNK)], axis=1)
    zm = jnp.dot(um.astype(BF16), bs_ref[...], preferred_element_type=F32)[0:1, :]
    buf_a[SCAN_PAD - 1:SCAN_PAD, :] = zm
    buf_b[SCAN_PAD - 1:SCAN_PAD, :] = zm

    src, dst = buf_a, buf_b
    for step in range(n_steps):
        shift = 1 << step
        for pair in range(OCT // 2):
            re = slice(pair * 2 * LANES, pair * 2 * LANES + LANES)
            im = slice(pair * 2 * LANES + LANES, (pair + 1) * 2 * LANES)
            ar = ap_ref[0, step:step + 1, pair * LANES:(pair + 1) * LANES]
            ai = ap_ref[0, n_steps + step:n_steps + step + 1, pair * LANES:(pair + 1) * LANES]
            xr = src[SCAN_PAD:, re]
            xi = src[SCAN_PAD:, im]
            pr = src[SCAN_PAD - shift:SCAN_PAD - shift + n_chunks, re]
            pi = src[SCAN_PAD - shift:SCAN_PAD - shift + n_chunks, im]
            dst[SCAN_PAD:, re] = xr + ar * pr - ai * pi
            dst[SCAN_PAD:, im] = xi + ar * pi + ai * pr
        src, dst = dst, src

    start_state = src[SCAN_PAD - 1:SCAN_PAD - 1 + n_chunks, :].astype(BF16)
    for ss in range(CHUNK // 2):
        cols = slice(ss * MXU_DIM, (ss + 1) * MXU_DIM)
        acc = jnp.dot(start_state, cs_ref[:, cols], preferred_element_type=F32)
        for tt in range(ss + 1):
            acc = acc + jnp.dot(ub_ref[tt], tz_ref[ss - tt], preferred_element_type=F32)
        for half in range(2):
            t = 2 * ss + half
            y = acc[:, half * LANES:(half + 1) * LANES] + d_ref[0] * slab(u_ref, t, n_chunks)
            y_ref[pl.ds(t, n_chunks, stride=CHUNK), :] = 0.5 * y * (1.0 + lax.erf(y * (1.0 / math.sqrt(2.0))))


def _ssm(u, u_meta, kc, bc, cc_re, cc_im, ap, dp, bsz, seq):
    n_chunks = seq // CHUNK
    assert n_chunks & (n_chunks - 1) == 0 and n_chunks // 2 <= SCAN_PAD and N_META == CHUNK
    n_steps = n_chunks.bit_length() - 1
    octet = lambda o, b: (o, 0, 0)
    scan_buf = pltpu.VMEM((SCAN_PAD + n_chunks, OCT_STATE), F32)
    return pl.pallas_call(
        functools.partial(_ssm_kernel, n_chunks=n_chunks),
        grid=(N_OCT, bsz),
        in_specs=[
            pl.BlockSpec((seq, LANES), lambda o, b: (b, o)),
            pl.BlockSpec((N_META, LANES), lambda o, b: (0, o)),
            pl.BlockSpec((CHUNK, 1, LANES, LANES), lambda o, b: (0, o, 0, 0)),
            pl.BlockSpec((CHUNK, 1, LANES, 4 * SSM_STATE), lambda o, b: (0, o, 0, 0)),
            pl.BlockSpec((CHUNK + 1, SSM_STATE, LANES), lambda o, b: (0, 0, o)),
            pl.BlockSpec((CHUNK + 1, SSM_STATE, LANES), lambda o, b: (0, 0, o)),
            pl.BlockSpec((1, 2 * n_steps, OCT_STATE // 2), octet),
            pl.BlockSpec((1, 1, LANES), octet),
        ],
        out_specs=pl.BlockSpec((seq, LANES), lambda o, b: (b, o)),
        out_shape=jax.ShapeDtypeStruct((bsz * seq, SSM_WIDTH), F32),
        scratch_shapes=[
            pltpu.VMEM((CHUNK // 2, MXU_DIM, MXU_DIM), BF16),
            pltpu.VMEM((OCT_IN, OCT_STATE), BF16),
            pltpu.VMEM((OCT_STATE, OCT_IN), BF16),
            pltpu.VMEM((CHUNK // 2, n_chunks, MXU_DIM), BF16), scan_buf, scan_buf],
        compiler_params=pltpu.CompilerParams(
            dimension_semantics=("arbitrary", "arbitrary"), vmem_limit_bytes=VMEM_LIMIT),
        name="s5_chunked",
    )(u, u_meta, kc, bc, cc_re, cc_im, ap, dp)


def _ssm_weights(a_re, a_im, log_dt, b_re, b_im, c_re, c_im, d_skip, n_steps):
    dt = jnp.exp(log_dt)[:, None]
    lam_re, lam_im = a_re * dt, a_im * dt

    def power(n):
        n = n[:, None, None]
        mag = jnp.exp(n * lam_re)
        return mag * jnp.cos(n * lam_im), mag * jnp.sin(n * lam_im)

    ab_re, ab_im = jnp.exp(lam_re) * jnp.cos(lam_im), jnp.exp(lam_re) * jnp.sin(lam_im)
    den = a_re * a_re + a_im * a_im
    nr, ni = ab_re - 1.0, ab_im
    f_re = (nr * a_re + ni * a_im) / den
    f_im = (ni * a_re - nr * a_im) / den
    bb_re = f_re[..., None] * b_re - f_im[..., None] * b_im
    bb_im = f_re[..., None] * b_im + f_im[..., None] * b_re

    p_re, p_im = power(jnp.arange(CHUNK + 1, dtype=F32))

    ct_re, ct_im = c_re.transpose(2, 0, 1)[None], c_im.transpose(2, 0, 1)[None]
    pt_re, pt_im = p_re.transpose(0, 2, 1)[..., None], p_im.transpose(0, 2, 1)[..., None]
    ca_re = ct_re * pt_re - ct_im * pt_im
    ca_im = ct_re * pt_im + ct_im * pt_re
    cc_re = ca_re.reshape(CHUNK + 1, SSM_STATE, SSM_WIDTH)
    cc_im = ca_im.reshape(CHUNK + 1, SSM_STATE, SSM_WIDTH)

    bp_re = bb_re.transpose(1, 0, 2)[None, :, :, :, None]
    bp_im = bb_im.transpose(1, 0, 2)[None, :, :, :, None]
    kern = jnp.sum(ca_re[:CHUNK, :, :, None, :] * bp_re - ca_im[:CHUNK, :, :, None, :] * bp_im, axis=1)
    kc = jnp.broadcast_to(kern[:, :, :, None, :], (CHUNK, N_GROUPS, SSM_GROUP, OCT, SSM_GROUP))
    kc = kc.reshape(CHUNK, N_OCT, LANES, LANES)

    r_re, r_im = p_re[CHUNK - 1::-1][:, :, None, :], p_im[CHUNK - 1::-1][:, :, None, :]
    bt_re, bt_im = bb_re.transpose(0, 2, 1)[None], bb_im.transpose(0, 2, 1)[None]
    bs_re = r_re * bt_re - r_im * bt_im
    bs_im = r_re * bt_im + r_im * bt_re
    bc = jnp.concatenate([bs_re, bs_re, bs_im, bs_im], axis=-1).reshape(CHUNK, N_OCT, LANES, 4 * SSM_STATE)

    s_re, s_im = power(CHUNK * (2.0 ** jnp.arange(n_steps, dtype=F32)))
    pack_ap = lambda m: m.reshape(n_steps, N_OCT, OCT_STATE // 2).transpose(1, 0, 2)
    ap = jnp.concatenate([pack_ap(s_re), pack_ap(s_im)], axis=1)

    dp = d_skip.reshape(N_OCT, 1, LANES)
    return kc, bc, cc_re, cc_im, ap, dp


def _out_ffn_kernel(x_ref, o_ref, y_ref, wglu_ref, bglu_ref, sg_ref, wout_ref, pmg_ref, pfg_ref,
                    wg_ref, wu_ref, wd_ref, pog_ref, out_ref):
    y = y_ref[...]
    gate = jnp.dot(y.astype(BF16), wglu_ref[...], preferred_element_type=F32) + bglu_ref[...]
    y = _rms(y * jax.nn.sigmoid(gate), sg_ref[...]).astype(BF16)
    mix = (jnp.dot(o_ref[...], wout_ref[:ATTN_WIDTH, :], preferred_element_type=F32)
           + jnp.dot(y, wout_ref[ATTN_WIDTH:, :], preferred_element_type=F32))
    h1 = x_ref[...] + _rms(mix, pmg_ref[...])
    h2 = _rms(h1, pfg_ref[...]).astype(BF16)
    g = jnp.dot(h2, wg_ref[...], preferred_element_type=F32)
    up = jnp.dot(h2, wu_ref[...], preferred_element_type=F32)
    f = (g * jax.nn.sigmoid(g) * up).astype(BF16)
    f = jnp.dot(f, wd_ref[...], preferred_element_type=F32)
    out_ref[...] = h1 + _rms(f, pog_ref[...])


def _out_ffn(x2d, o, y, wglu, bglu, sg, wout, pmg, pfg, wg, wu, wd, pog, tm):
    n = x2d.shape[0]
    row = lambda i: (i, 0)
    const = lambda i: (0, 0)

    def resident(shape):
        return pl.BlockSpec(shape, const, pipeline_mode=pl.Buffered(1))

    return pl.pallas_call(
        _out_ffn_kernel,
        grid=(n // tm,),
        in_specs=[
            pl.BlockSpec((tm, D_MODEL), row),
            pl.BlockSpec((tm, ATTN_WIDTH), row),
            pl.BlockSpec((tm, SSM_WIDTH), row),
            resident((SSM_WIDTH, SSM_WIDTH)),
            resident((1, SSM_WIDTH)),
            resident((1, SSM_WIDTH)),
            resident((D_MODEL, D_MODEL)),
            resident((1, D_MODEL)),
            resident((1, D_MODEL)),
            resident((D_MODEL, D_FF)),
            resident((D_MODEL, D_FF)),
            resident((D_FF, D_MODEL)),
            resident((1, D_MODEL)),
        ],
        out_specs=pl.BlockSpec((tm, D_MODEL), row),
        out_shape=jax.ShapeDtypeStruct((n, D_MODEL), F32),
        compiler_params=pltpu.CompilerParams(
            dimension_semantics=("arbitrary",), vmem_limit_bytes=VMEM_LIMIT),
        name="out_ffn",
    )(x2d, o, y, wglu, bglu, sg, wout, pmg, pfg, wg, wu, wd, pog)


def _rope_tables(start, length):
    half = ROT_DIM // 2
    pos = jnp.arange(start, start + length, dtype=F32)
    inv_freq = ROPE_THETA ** (-jnp.arange(0, ROT_DIM, 2, dtype=F32) / ROT_DIM)
    d = jnp.arange(LANES) % QK_DIM
    freq = jnp.where(d < ROT_DIM, inv_freq[d % half], 0.0)
    ang = pos[:, None] * freq[None, :]
    sin = jnp.sin(ang)
    sa = jnp.where(d < half, -1.0, 0.0)
    sb = jnp.where((d >= half) & (d < ROT_DIM), 1.0, 0.0)
    return jnp.cos(ang), sin * sa[None, :], sin * sb[None, :]


def _permute_w_in(w):
    qk = QK_DIM * N_HEADS
    scale = math.log2(math.e) / math.sqrt(QK_DIM)
    q1, q2, k1, k2 = (w[:, i * qk:(i + 1) * qk].reshape(D_MODEL, N_HEADS, QK_DIM) for i in range(4))
    qp = (jnp.concatenate([q1, q2], axis=2) * scale).reshape(D_MODEL, 2 * qk)
    kp = jnp.concatenate([k1, k2], axis=2).reshape(D_MODEL, 2 * qk)
    return jnp.concatenate([qp, kp, w[:, 4 * qk:]], axis=1).astype(BF16)


def kernel(x, meta, pre_mix_g, w_in, lambda_q1, lambda_k1, lambda_q2, lambda_k2, subln_g, a_re, a_im, log_dt,
           b_re, b_im, c_re, c_im, d_skip, w_glu, b_glu, ssm_out_g, w_out, post_mix_g, pre_ffn_g, w_gate,
           w_up, w_down, post_ffn_g):
    bsz, seq, _ = x.shape
    n = bsz * seq
    x2d = x.reshape(n, D_MODEL)
    row = lambda t: t[0].reshape(1, -1)

    w_in_p = _permute_w_in(w_in[0])
    g0 = row(pre_mix_g)
    q, k, v, u = _in_proj(x2d, g0, w_in_p, *_rope_tables(N_META, seq), ROW_TILE)
    _, k_m, v_m, u_m = _in_proj(meta, g0, w_in_p, *_rope_tables(0, N_META), N_META)

    pad_meta = lambda t: jnp.pad(t, ((0, LANES - N_META), (0, 0)))
    o = _attention(q, k, v, pad_meta(k_m), pad_meta(v_m), lambda_q1, lambda_k1, lambda_q2, lambda_k2,
                   row(subln_g), bsz, seq)

    n_steps = (seq // CHUNK).bit_length() - 1
    ssm_w = _ssm_weights(a_re[0], a_im[0], log_dt[0], b_re[0], b_im[0], c_re[0], c_im[0], d_skip[0], n_steps)
    y = _ssm(u, u_m, *ssm_w, bsz, seq)

    out = _out_ffn(x2d, o, y, w_glu[0].astype(BF16), row(b_glu), row(ssm_out_g), w_out[0].astype(BF16),
                   row(post_mix_g), row(pre_ffn_g), w_gate[0].astype(BF16), w_up[0].astype(BF16),
                   w_down[0].astype(BF16), row(post_ffn_g), ROW_TILE)
    return out.reshape(bsz, seq, D_MODEL)
```

```python
import functools
import math

import jax
import jax.numpy as jnp
from jax import lax
from jax.experimental import pallas as pl
from jax.experimental.pallas import tpu as pltpu

D_MODEL = 1024
N_META = 16
N_HEADS = 4
QK_DIM = 64
V_DIM = 128
ROT_DIM = 16
ROPE_THETA = 500000.0
SSM_GROUP = 16
N_GROUPS = 32
SSM_STATE = 64
SSM_WIDTH = 512
ATTN_WIDTH = 512
D_FF = 2816
EPS = 1e-6
LAM_INIT = 0.8 - 0.6 * math.exp(-0.3 * 0)

LANES = 128
MXU_DIM = 256
CHUNK = 16
OCT = LANES // SSM_GROUP
N_OCT = N_GROUPS // OCT
OCT_IN = CHUNK * LANES
OCT_STATE = OCT * 2 * SSM_STATE
SCAN_PAD = 256

ROW_TILE = 512
Q_TILE = 512
KV_TILE = 512
ATTN_UNROLL = 4
VMEM_LIMIT = 56 * 1024 * 1024

F32 = jnp.float32
BF16 = jnp.bfloat16


def _rms(x, g):
    return x * lax.rsqrt(jnp.mean(x * x, axis=-1, keepdims=True) + EPS) * g


def _in_proj_kernel(x_ref, g_ref, w_ref, cos_ref, sa_ref, sb_ref, q_ref, k_ref, v_ref, u_ref):
    h = _rms(x_ref[...], g_ref[...]).astype(BF16)
    proj = jnp.dot(h, w_ref[...], preferred_element_type=F32)
    cos, sa, sb = cos_ref[...], sa_ref[...], sb_ref[...]

    def rope(t):
        return (t * cos + pltpu.roll(t, LANES - ROT_DIM // 2, axis=1) * sa
                + pltpu.roll(t, ROT_DIM // 2, axis=1) * sb)

    for c in range(ATTN_WIDTH // LANES):
        sl = slice(c * LANES, (c + 1) * LANES)
        q_ref[:, sl] = rope(proj[:, c * LANES:(c + 1) * LANES]).astype(BF16)
        k_ref[:, sl] = rope(proj[:, ATTN_WIDTH + c * LANES:ATTN_WIDTH + (c + 1) * LANES]).astype(BF16)
    v_ref[...] = proj[:, 2 * ATTN_WIDTH:3 * ATTN_WIDTH].astype(BF16)
    u_ref[...] = proj[:, 3 * ATTN_WIDTH:]


def _in_proj(x2d, g, w, cos, sa, sb, tm):
    n = x2d.shape[0]
    n_pos = cos.shape[0] // tm
    row = lambda i: (i, 0)
    pos = lambda i: (i % n_pos, 0)
    const = lambda i: (0, 0)
    return pl.pallas_call(
        _in_proj_kernel,
        grid=(n // tm,),
        in_specs=[
            pl.BlockSpec((tm, D_MODEL), row),
            pl.BlockSpec((1, D_MODEL), const),
            pl.BlockSpec((D_MODEL, 4 * ATTN_WIDTH), const),
            pl.BlockSpec((tm, LANES), pos),
            pl.BlockSpec((tm, LANES), pos),
            pl.BlockSpec((tm, LANES), pos),
        ],
        out_specs=[pl.BlockSpec((tm, ATTN_WIDTH), row)] * 4,
        out_shape=[jax.ShapeDtypeStruct((n, ATTN_WIDTH), BF16)] * 3
        + [jax.ShapeDtypeStruct((n, SSM_WIDTH), F32)],
        compiler_params=pltpu.CompilerParams(
            dimension_semantics=("arbitrary",), vmem_limit_bytes=VMEM_LIMIT),
        name="in_proj",
    )(x2d, g, w, cos, sa, sb)


def _attn_kernel(q_ref, k_ref, v_ref, km_ref, vm_ref, lq1_ref, lk1_ref, lq2_ref, lk2_ref, sg_ref,
                 o_ref, m_ref, l_ref, acc_ref, pa_ref, pb_ref):
    qi = pl.program_id(2)
    q = q_ref[...]
    lane = lax.broadcasted_iota(jnp.int32, q.shape, 1)
    zero = jnp.zeros_like(q)
    qs = (jnp.where(lane < QK_DIM, q, zero), jnp.where(lane >= QK_DIM, q, zero))
    nt = (((1,), (1,)), ((), ()))
    wide = (Q_TILE, LANES)

    def block(k, mask, p_out, prev):
        n_col = k.shape[0] // LANES
        for i in range(2):
            s = lax.dot_general(qs[i], k, nt, preferred_element_type=F32)
            if mask is not None:
                s = jnp.where(mask, s, -jnp.inf)
            cols = [s[:, c * LANES:(c + 1) * LANES] for c in range(n_col)]
            m_tile = functools.reduce(jnp.maximum, cols)
            m_new = jnp.broadcast_to(jnp.max(m_tile, axis=1, keepdims=True), wide)
            if prev is not None:
                m_old = m_ref[i]
                m_new = jnp.maximum(m_old, m_new)
                alpha = jnp.exp2(m_old - m_new)
            m_ref[i] = m_new
            ps = [jnp.exp2(c - m_new) for c in cols]
            l_tile = functools.reduce(jnp.add, ps)
            l_ref[i] = l_tile if prev is None else alpha * l_ref[i] + l_tile
            for c in range(n_col):
                p_out[i, :, c * LANES:(c + 1) * LANES] = ps[c].astype(BF16)
            if prev is not None:
                p_prev, v_prev = prev
                pv = jnp.dot(p_prev[i], v_prev, preferred_element_type=F32)
                acc_ref[i] = (acc_ref[i] + pv) * alpha

    def tile(ref, t):
        return ref[pl.ds(pl.multiple_of(t * KV_TILE, KV_TILE), KV_TILE), :]

    def finish(pending, p_meta):
        meta_mask = lax.broadcasted_iota(jnp.int32, wide, 1) < N_META
        block(km_ref[...], meta_mask, p_meta, pending)
        for i in range(2):
            acc_ref[i] = acc_ref[i] + jnp.dot(p_meta[i, :, :LANES], vm_ref[...], preferred_element_type=F32)

    acc_ref[...] = jnp.zeros(acc_ref.shape, F32)
    causal = (lax.broadcasted_iota(jnp.int32, (Q_TILE, KV_TILE), 1)
              <= lax.broadcasted_iota(jnp.int32, (Q_TILE, KV_TILE), 0))
    block(tile(k_ref, qi), causal, pa_ref, None)

    def run(t0, n):
        for d in range(n):
            before = jnp.where(t0 == 0, qi, t0 - 1) if d == 0 else t0 + d - 1
            src, dst = (pa_ref, pb_ref) if d % 2 == 0 else (pb_ref, pa_ref)
            block(tile(k_ref, t0 + d), None, dst, (src, tile(v_ref, before)))

    def unrolled(jj, carry):
        run(ATTN_UNROLL * jj, ATTN_UNROLL)
        return carry

    lax.fori_loop(0, qi // ATTN_UNROLL, unrolled, 0)
    for n in (ATTN_UNROLL // 2, ATTN_UNROLL // 4):
        if n >= 2:
            done = (qi // (2 * n)) * (2 * n)
            pl.when(qi - done >= n)(functools.partial(run, done, n))
    n_even = (qi // 2) * 2
    pend_a = jnp.where(n_even == 0, qi, n_even - 1)

    @pl.when(qi % 2 == 1)
    def _():
        block(tile(k_ref, qi - 1), None, pb_ref, (pa_ref, tile(v_ref, pend_a)))
        finish((pb_ref, tile(v_ref, qi - 1)), pa_ref)

    @pl.when(qi % 2 == 0)
    def _():
        finish((pa_ref, tile(v_ref, pend_a)), pb_ref)

    lam = (jnp.exp(jnp.sum(lq1_ref[...] * lk1_ref[...], axis=1, keepdims=True))
           - jnp.exp(jnp.sum(lq2_ref[...] * lk2_ref[...], axis=1, keepdims=True)) + LAM_INIT)
    l1 = jnp.sum(l_ref[0], axis=1, keepdims=True)
    l2 = jnp.sum(l_ref[1], axis=1, keepdims=True)
    o = acc_ref[0] / l1 - lam * (acc_ref[1] / l2)
    o_ref[...] = (_rms(o, sg_ref[...]) * (1.0 - LAM_INIT)).astype(BF16)


def _attention(q, k, v, km, vm, lq1, lk1, lq2, lk2, sg, bsz, seq):
    nq = seq // Q_TILE
    assert Q_TILE == KV_TILE
    qmap = lambda b, h, i: (b * nq + i, h)
    kvmap = lambda b, h, i: (b, h)
    metamap = lambda b, h, i: (0, h)
    const = lambda b, h, i: (0, 0)
    vec = pl.BlockSpec((1, QK_DIM), const)
    return pl.pallas_call(
        _attn_kernel,
        grid=(bsz, N_HEADS, nq),
        in_specs=[
            pl.BlockSpec((Q_TILE, LANES), qmap),
            pl.BlockSpec((seq, LANES), kvmap),
            pl.BlockSpec((seq, LANES), kvmap),
            pl.BlockSpec((LANES, LANES), metamap),
            pl.BlockSpec((LANES, LANES), metamap),
            vec, vec, vec, vec,
            pl.BlockSpec((1, V_DIM), const),
        ],
        out_specs=pl.BlockSpec((Q_TILE, LANES), qmap),
        out_shape=jax.ShapeDtypeStruct((bsz * seq, ATTN_WIDTH), BF16),
        scratch_shapes=[
            pltpu.VMEM((2, Q_TILE, LANES), F32),
            pltpu.VMEM((2, Q_TILE, LANES), F32),
            pltpu.VMEM((2, Q_TILE, V_DIM), F32),
            pltpu.VMEM((2, Q_TILE, KV_TILE), BF16),
            pltpu.VMEM((2, Q_TILE, KV_TILE), BF16),
        ],
        compiler_params=pltpu.CompilerParams(
            dimension_semantics=("arbitrary", "arbitrary", "arbitrary"), vmem_limit_bytes=VMEM_LIMIT),
        name="diff_attention",
    )(q, k, v, km, vm, lq1, lk1, lq2, lk2, sg)


def _ssm_kernel(u_ref, um_ref, kc_ref, bc_ref, ccr_ref, cci_ref, ap_ref, d_ref, y_ref,
                tz_ref, bs_ref, cs_ref, ub_ref, buf_a, buf_b, *, n_chunks):
    n_steps = n_chunks.bit_length() - 1

    @pl.when((pl.program_id(0) == 0) & (pl.program_id(1) == 0))
    def _():
        buf_a[:SCAN_PAD, :] = jnp.zeros((SCAN_PAD, OCT_STATE), F32)
        buf_b[:SCAN_PAD, :] = jnp.zeros((SCAN_PAD, OCT_STATE), F32)

    @pl.when(pl.program_id(1) == 0)
    def _():
        def group_of(shape, axis):
            return (lax.broadcasted_iota(jnp.int32, shape, axis) // SSM_GROUP) % OCT

        same = group_of((LANES, LANES), 0) == group_of((LANES, LANES), 1)
        lag_blk = [jnp.where(same, kc_ref[n, 0], 0.0).astype(BF16) for n in range(CHUNK)]
        for m in range(CHUNK // 2):
            tz_ref[m, :LANES, :LANES] = lag_blk[2 * m]
            tz_ref[m, :LANES, LANES:] = lag_blk[2 * m + 1]
            tz_ref[m, LANES:, :LANES] = lag_blk[2 * m - 1] if m else jnp.zeros((LANES, LANES), BF16)
            tz_ref[m, LANES:, LANES:] = lag_blk[2 * m]
        bc = bc_ref[:, 0].reshape(OCT_IN, 4 * SSM_STATE)
        g_row = group_of(bc.shape, 0)
        w_col = (lax.broadcasted_iota(jnp.int32, bc.shape, 1) // SSM_STATE) % 2
        for qq in range(OCT // 2):
            bs_ref[:, qq * MXU_DIM:(qq + 1) * MXU_DIM] = jnp.where(g_row == 2 * qq + w_col, bc, 0.0).astype(BF16)
        g_col = group_of((SSM_STATE, LANES), 1)
        for s in range(CHUNK):
            c_s = (ccr_ref[s + 1], -cci_ref[s + 1])
            for qq in range(OCT // 2):
                for part in range(2):
                    for w in range(2):
                        r0 = qq * MXU_DIM + part * LANES + w * SSM_STATE
                        cs_ref[r0:r0 + SSM_STATE, s * LANES:(s + 1) * LANES] = jnp.where(
                            g_col == 2 * qq + w, c_s[part], 0.0).astype(BF16)

    def slab(ref, t, rows):
        return ref[pl.ds(t, rows, stride=CHUNK), :]

    for t in range(CHUNK):
        ub_ref[t // 2, :, (t % 2) * LANES:(t % 2 + 1) * LANES] = slab(u_ref, t, n_chunks).astype(BF16)

    z = jnp.dot(ub_ref[0], bs_ref[:MXU_DIM, :], preferred_element_type=F32)
    for tt in range(1, CHUNK // 2):
        z = z + jnp.dot(ub_ref[tt], bs_ref[tt * MXU_DIM:(tt + 1) * MXU_DIM, :], preferred_element_type=F32)
    buf_a[SCAN_PAD:, :] = z
    um = jnp.concatenate([jnp.broadcast_to(um_ref[t:t + 1, :], (8, LANES)) for t in range(CHUNK)], axis=1)
    zm = jnp.dot(um.astype(BF16), bs_ref[...], preferred_element_type=F32)[0:1, :]
    buf_a[SCAN_PAD - 1:SCAN_PAD, :] = zm
    buf_b[SCAN_PAD - 1:SCAN_PAD, :] = zm

    src, dst = buf_a, buf_b
    for step in range(n_steps):
        shift = 1 << step
        for pair in range(OCT // 2):
            re = slice(pair * 2 * LANES, pair * 2 * LANES + LANES)
            im = slice(pair * 2 * LANES + LANES, (pair + 1) * 2 * LANES)
            ar = ap_ref[0, step:step + 1, pair * LANES:(pair + 1) * LANES]
            ai = ap_ref[0, n_steps + step:n_steps + step + 1, pair * LANES:(pair + 1) * LANES]
            xr = src[SCAN_PAD:, re]
            xi = src[SCAN_PAD:, im]
            pr = src[SCAN_PAD - shift:SCAN_PAD - shift + n_chunks, re]
            pi = src[SCAN_PAD - shift:SCAN_PAD - shift + n_chunks, im]
            dst[SCAN_PAD:, re] = xr + ar * pr - ai * pi
            dst[SCAN_PAD:, im] = xi + ar * pi + ai * pr
        src, dst = dst, src

    start_state = src[SCAN_PAD - 1:SCAN_PAD - 1 + n_chunks, :].astype(BF16)
    for ss in range(CHUNK // 2):
        cols = slice(ss * MXU_DIM, (ss + 1) * MXU_DIM)
        acc = jnp.dot(start_state, cs_ref[:, cols], preferred_element_type=F32)
        for tt in range(ss + 1):
            acc = acc + jnp.dot(ub_ref[tt], tz_ref[ss - tt], preferred_element_type=F32)
        for half in range(2):
            t = 2 * ss + half
            y = acc[:, half * LANES:(half + 1) * LANES] + d_ref[0] * slab(u_ref, t, n_chunks)
            y_ref[pl.ds(t, n_chunks, stride=CHUNK), :] = 0.5 * y * (1.0 + lax.erf(y * (1.0 / math.sqrt(2.0))))


def _ssm(u, u_meta, kc, bc, cc_re, cc_im, ap, dp, bsz, seq):
    n_chunks = seq // CHUNK
    assert n_chunks & (n_chunks - 1) == 0 and n_chunks // 2 <= SCAN_PAD and N_META == CHUNK
    n_steps = n_chunks.bit_length() - 1
    octet = lambda o, b: (o, 0, 0)
    scan_buf = pltpu.VMEM((SCAN_PAD + n_chunks, OCT_STATE), F32)
    return pl.pallas_call(
        functools.partial(_ssm_kernel, n_chunks=n_chunks),
        grid=(N_OCT, bsz),
        in_specs=[
            pl.BlockSpec((seq, LANES), lambda o, b: (b, o)),
            pl.BlockSpec((N_META, LANES), lambda o, b: (0, o)),
            pl.BlockSpec((CHUNK, 1, LANES, LANES), lambda o, b: (0, o, 0, 0)),
            pl.BlockSpec((CHUNK, 1, LANES, 4 * SSM_STATE), lambda o, b: (0, o, 0, 0)),
            pl.BlockSpec((CHUNK + 1, SSM_STATE, LANES), lambda o, b: (0, 0, o)),
            pl.BlockSpec((CHUNK + 1, SSM_STATE, LANES), lambda o, b: (0, 0, o)),
            pl.BlockSpec((1, 2 * n_steps, OCT_STATE // 2), octet),
            pl.BlockSpec((1, 1, LANES), octet),
        ],
        out_specs=pl.BlockSpec((seq, LANES), lambda o, b: (b, o)),
        out_shape=jax.ShapeDtypeStruct((bsz * seq, SSM_WIDTH), F32),
        scratch_shapes=[
            pltpu.VMEM((CHUNK // 2, MXU_DIM, MXU_DIM), BF16),
            pltpu.VMEM((OCT_IN, OCT_STATE), BF16),
            pltpu.VMEM((OCT_STATE, OCT_IN), BF16),
            pltpu.VMEM((CHUNK // 2, n_chunks, MXU_DIM), BF16), scan_buf, scan_buf],
        compiler_params=pltpu.CompilerParams(
            dimension_semantics=("arbitrary", "arbitrary"), vmem_limit_bytes=VMEM_LIMIT),
        name="s5_chunked",
    )(u, u_meta, kc, bc, cc_re, cc_im, ap, dp)


def _ssm_weights(a_re, a_im, log_dt, b_re, b_im, c_re, c_im, d_skip, n_steps):
    dt = jnp.exp(log_dt)[:, None]
    lam_re, lam_im = a_re * dt, a_im * dt

    def power(n):
        n = n[:, None, None]
        mag = jnp.exp(n * lam_re)
        return mag * jnp.cos(n * lam_im), mag * jnp.sin(n * lam_im)

    ab_re, ab_im = jnp.exp(lam_re) * jnp.cos(lam_im), jnp.exp(lam_re) * jnp.sin(lam_im)
    den = a_re * a_re + a_im * a_im
    nr, ni = ab_re - 1.0, ab_im
    f_re = (nr * a_re + ni * a_im) / den
    f_im = (ni * a_re - nr * a_im) / den
    bb_re = f_re[..., None] * b_re - f_im[..., None] * b_im
    bb_im = f_re[..., None] * b_im + f_im[..., None] * b_re

    p_re, p_im = power(jnp.arange(CHUNK + 1, dtype=F32))

    ct_re, ct_im = c_re.transpose(2, 0, 1)[None], c_im.transpose(2, 0, 1)[None]
    pt_re, pt_im = p_re.transpose(0, 2, 1)[..., None], p_im.transpose(0, 2, 1)[..., None]
    ca_re = ct_re * pt_re - ct_im * pt_im
    ca_im = ct_re * pt_im + ct_im * pt_re
    cc_re = ca_re.reshape(CHUNK + 1, SSM_STATE, SSM_WIDTH)
    cc_im = ca_im.reshape(CHUNK + 1, SSM_STATE, SSM_WIDTH)

    bp_re = bb_re.transpose(1, 0, 2)[None, :, :, :, None]
    bp_im = bb_im.transpose(1, 0, 2)[None, :, :, :, None]
    kern = jnp.sum(ca_re[:CHUNK, :, :, None, :] * bp_re - ca_im[:CHUNK, :, :, None, :] * bp_im, axis=1)
    repeat = jnp.tile(jnp.eye(SSM_GROUP, dtype=F32), (1, OCT))
    kc = jnp.dot(kern.reshape(-1, SSM_GROUP), repeat, precision=lax.Precision.HIGHEST)
    kc = kc.reshape(CHUNK, N_OCT, LANES, LANES)

    r_re, r_im = p_re[CHUNK - 1::-1][:, :, None, :], p_im[CHUNK - 1::-1][:, :, None, :]
    bt_re, bt_im = bb_re.transpose(0, 2, 1)[None], bb_im.transpose(0, 2, 1)[None]
    bs_re = r_re * bt_re - r_im * bt_im
    bs_im = r_re * bt_im + r_im * bt_re
    bc = jnp.concatenate([bs_re, bs_re, bs_im, bs_im], axis=-1).reshape(CHUNK, N_OCT, LANES, 4 * SSM_STATE)

    s_re, s_im = power(CHUNK * (2.0 ** jnp.arange(n_steps, dtype=F32)))
    pack_ap = lambda m: m.reshape(n_steps, N_OCT, OCT_STATE // 2).transpose(1, 0, 2)
    ap = jnp.concatenate([pack_ap(s_re), pack_ap(s_im)], axis=1)

    dp = d_skip.reshape(N_OCT, 1, LANES)
    return kc, bc, cc_re, cc_im, ap, dp


def _out_ffn_kernel(x_ref, o_ref, y_ref, wglu_ref, bglu_ref, sg_ref, wout_ref, pmg_ref, pfg_ref,
                    wg_ref, wu_ref, wd_ref, pog_ref, out_ref):
    y = y_ref[...]
    gate = jnp.dot(y.astype(BF16), wglu_ref[...], preferred_element_type=F32) + bglu_ref[...]
    y = _rms(y * jax.nn.sigmoid(gate), sg_ref[...]).astype(BF16)
    mix = (jnp.dot(o_ref[...], wout_ref[:ATTN_WIDTH, :], preferred_element_type=F32)
           + jnp.dot(y, wout_ref[ATTN_WIDTH:, :], preferred_element_type=F32))
    h1 = x_ref[...] + _rms(mix, pmg_ref[...])
    h2 = _rms(h1, pfg_ref[...]).astype(BF16)
    g = jnp.dot(h2, wg_ref[...], preferred_element_type=F32)
    up = jnp.dot(h2, wu_ref[...], preferred_element_type=F32)
    f = (g * jax.nn.sigmoid(g) * up).astype(BF16)
    f = jnp.dot(f, wd_ref[...], preferred_element_type=F32)
    out_ref[...] = h1 + _rms(f, pog_ref[...])


def _out_ffn(x2d, o, y, wglu, bglu, sg, wout, pmg, pfg, wg, wu, wd, pog, tm):
    n = x2d.shape[0]
    row = lambda i: (i, 0)
    const = lambda i: (0, 0)

    def resident(shape):
        return pl.BlockSpec(shape, const, pipeline_mode=pl.Buffered(1))

    return pl.pallas_call(
        _out_ffn_kernel,
        grid=(n // tm,),
        in_specs=[
            pl.BlockSpec((tm, D_MODEL), row),
            pl.BlockSpec((tm, ATTN_WIDTH), row),
            pl.BlockSpec((tm, SSM_WIDTH), row),
            resident((SSM_WIDTH, SSM_WIDTH)),
            resident((1, SSM_WIDTH)),
            resident((1, SSM_WIDTH)),
            resident((D_MODEL, D_MODEL)),
            resident((1, D_MODEL)),
            resident((1, D_MODEL)),
            resident((D_MODEL, D_FF)),
            resident((D_MODEL, D_FF)),
            resident((D_FF, D_MODEL)),
            resident((1, D_MODEL)),
        ],
        out_specs=pl.BlockSpec((tm, D_MODEL), row),
        out_shape=jax.ShapeDtypeStruct((n, D_MODEL), F32),
        compiler_params=pltpu.CompilerParams(
            dimension_semantics=("arbitrary",), vmem_limit_bytes=VMEM_LIMIT),
        name="out_ffn",
    )(x2d, o, y, wglu, bglu, sg, wout, pmg, pfg, wg, wu, wd, pog)


def _rope_tables(start, length):
    half = ROT_DIM // 2
    pos = jnp.arange(start, start + length, dtype=F32)
    inv_freq = ROPE_THETA ** (-jnp.arange(0, ROT_DIM, 2, dtype=F32) / ROT_DIM)
    ang = pos[:, None] * inv_freq[None, :]
    d = jnp.arange(LANES)[None, :] % QK_DIM
    f = jnp.arange(half)[:, None]
    hit = (d % half == f).astype(F32)
    place_cos = hit * (d < ROT_DIM)
    place_sa = -hit * (d < half)
    place_sb = hit * ((d >= half) & (d < ROT_DIM))
    hp = lax.Precision.HIGHEST
    sin = jnp.sin(ang)
    cos_t = jnp.dot(jnp.cos(ang), place_cos, precision=hp) + (d >= ROT_DIM).astype(F32)
    return cos_t, jnp.dot(sin, place_sa, precision=hp), jnp.dot(sin, place_sb, precision=hp)


def _permute_w_in(w):
    qk = QK_DIM * N_HEADS
    scale = math.log2(math.e) / math.sqrt(QK_DIM)
    q1, q2, k1, k2 = (w[:, i * qk:(i + 1) * qk].reshape(D_MODEL, N_HEADS, QK_DIM) for i in range(4))
    qp = (jnp.concatenate([q1, q2], axis=2) * scale).reshape(D_MODEL, 2 * qk)
    kp = jnp.concatenate([k1, k2], axis=2).reshape(D_MODEL, 2 * qk)
    return jnp.concatenate([qp, kp, w[:, 4 * qk:]], axis=1).astype(BF16)


def kernel(x, meta, pre_mix_g, w_in, lambda_q1, lambda_k1, lambda_q2, lambda_k2, subln_g, a_re, a_im, log_dt,
           b_re, b_im, c_re, c_im, d_skip, w_glu, b_glu, ssm_out_g, w_out, post_mix_g, pre_ffn_g, w_gate,
           w_up, w_down, post_ffn_g):
    bsz, seq, _ = x.shape
    n = bsz * seq
    x2d = x.reshape(n, D_MODEL)
    row = lambda t: t[0].reshape(1, -1)

    w_in_p = _permute_w_in(w_in[0])
    g0 = row(pre_mix_g)
    q, k, v, u = _in_proj(x2d, g0, w_in_p, *_rope_tables(N_META, seq), ROW_TILE)
    _, k_m, v_m, u_m = _in_proj(meta, g0, w_in_p, *_rope_tables(0, N_META), N_META)

    pad_meta = lambda t: jnp.pad(t, ((0, LANES - N_META), (0, 0)))
    o = _attention(q, k, v, pad_meta(k_m), pad_meta(v_m), lambda_q1, lambda_k1, lambda_q2, lambda_k2,
                   row(subln_g), bsz, seq)

    n_steps = (seq // CHUNK).bit_length() - 1
    ssm_w = _ssm_weights(a_re[0], a_im[0], log_dt[0], b_re[0], b_im[0], c_re[0], c_im[0], d_skip[0], n_steps)
    y = _ssm(u, u_m, *ssm_w, bsz, seq)

    out = _out_ffn(x2d, o, y, w_glu[0].astype(BF16), row(b_glu), row(ssm_out_g), w_out[0].astype(BF16),
                   row(post_mix_g), row(pre_ffn_g), w_gate[0].astype(BF16), w_up[0].astype(BF16),
                   w_down[0].astype(BF16), row(post_ffn_g), ROW_TILE)
    return out.reshape(bsz, seq, D_MODEL)
```

```python
import functools
import math

import jax
import jax.numpy as jnp
from jax import lax
from jax.experimental import pallas as pl
from jax.experimental.pallas import tpu as pltpu

D_MODEL = 1024
N_META = 16
N_HEADS = 4
QK_DIM = 64
V_DIM = 128
ROT_DIM = 16
ROPE_THETA = 500000.0
SSM_GROUP = 16
N_GROUPS = 32
SSM_STATE = 64
SSM_WIDTH = 512
ATTN_WIDTH = 512
D_FF = 2816
EPS = 1e-6
LAM_INIT = 0.8 - 0.6 * math.exp(-0.3 * 0)

LANES = 128
MXU_DIM = 256
CHUNK = 16
OCT = LANES // SSM_GROUP
N_OCT = N_GROUPS // OCT
OCT_IN = CHUNK * LANES
OCT_STATE = OCT * 2 * SSM_STATE
SCAN_PAD = 256

ROW_TILE = 512
Q_TILE = 512
KV_TILE = 512
ATTN_UNROLL = 4
VMEM_LIMIT = 56 * 1024 * 1024

F32 = jnp.float32
BF16 = jnp.bfloat16


def _rms(x, g):
    return x * lax.rsqrt(jnp.mean(x * x, axis=-1, keepdims=True) + EPS) * g


def _in_proj_kernel(x_ref, g_ref, w_ref, cos_ref, sa_ref, sb_ref, q_ref, k_ref, v_ref, u_ref):
    h = _rms(x_ref[...], g_ref[...]).astype(BF16)
    proj = jnp.dot(h, w_ref[...], preferred_element_type=F32)
    cos, sa, sb = cos_ref[...], sa_ref[...], sb_ref[...]

    def rope(t):
        return (t * cos + pltpu.roll(t, LANES - ROT_DIM // 2, axis=1) * sa
                + pltpu.roll(t, ROT_DIM // 2, axis=1) * sb)

    for c in range(ATTN_WIDTH // LANES):
        sl = slice(c * LANES, (c + 1) * LANES)
        q_ref[:, sl] = rope(proj[:, c * LANES:(c + 1) * LANES]).astype(BF16)
        k_ref[:, sl] = rope(proj[:, ATTN_WIDTH + c * LANES:ATTN_WIDTH + (c + 1) * LANES]).astype(BF16)
    v_ref[...] = proj[:, 2 * ATTN_WIDTH:3 * ATTN_WIDTH].astype(BF16)
    u_ref[...] = proj[:, 3 * ATTN_WIDTH:]


def _in_proj(x2d, g, w, cos, sa, sb, tm):
    n = x2d.shape[0]
    n_pos = cos.shape[0] // tm
    row = lambda i: (i, 0)
    pos = lambda i: (i % n_pos, 0)
    const = lambda i: (0, 0)
    return pl.pallas_call(
        _in_proj_kernel,
        grid=(n // tm,),
        in_specs=[
            pl.BlockSpec((tm, D_MODEL), row),
            pl.BlockSpec((1, D_MODEL), const),
            pl.BlockSpec((D_MODEL, 4 * ATTN_WIDTH), const),
            pl.BlockSpec((tm, LANES), pos),
            pl.BlockSpec((tm, LANES), pos),
            pl.BlockSpec((tm, LANES), pos),
        ],
        out_specs=[pl.BlockSpec((tm, ATTN_WIDTH), row)] * 4,
        out_shape=[jax.ShapeDtypeStruct((n, ATTN_WIDTH), BF16)] * 3
        + [jax.ShapeDtypeStruct((n, SSM_WIDTH), F32)],
        compiler_params=pltpu.CompilerParams(
            dimension_semantics=("arbitrary",), vmem_limit_bytes=VMEM_LIMIT),
        name="in_proj",
    )(x2d, g, w, cos, sa, sb)


def _attn_kernel(*refs):
    n_tiles = refs[0].shape[0] // Q_TILE

    def q_tile(qi, carry):
        _attn_q_tile(qi, *refs)
        return carry

    lax.fori_loop(0, n_tiles, q_tile, 0)


def _attn_q_tile(qi, q_ref, k_ref, v_ref, km_ref, vm_ref, lq1_ref, lk1_ref, lq2_ref, lk2_ref, sg_ref,
                 o_ref, m_ref, l_ref, acc_ref, pa_ref, pb_ref):
    q_rows = pl.ds(pl.multiple_of(qi * Q_TILE, Q_TILE), Q_TILE)
    q = q_ref[q_rows, :]
    lane = lax.broadcasted_iota(jnp.int32, q.shape, 1)
    zero = jnp.zeros_like(q)
    qs = (jnp.where(lane < QK_DIM, q, zero), jnp.where(lane >= QK_DIM, q, zero))
    nt = (((1,), (1,)), ((), ()))
    wide = (Q_TILE, LANES)

    def block(k, mask, p_out, prev):
        n_col = k.shape[0] // LANES
        for i in range(2):
            s = lax.dot_general(qs[i], k, nt, preferred_element_type=F32)
            if mask is not None:
                s = jnp.where(mask, s, -jnp.inf)
            cols = [s[:, c * LANES:(c + 1) * LANES] for c in range(n_col)]
            m_tile = functools.reduce(jnp.maximum, cols)
            m_new = jnp.broadcast_to(jnp.max(m_tile, axis=1, keepdims=True), wide)
            if prev is not None:
                m_old = m_ref[i]
                m_new = jnp.maximum(m_old, m_new)
                alpha = jnp.exp2(m_old - m_new)
            m_ref[i] = m_new
            ps = [jnp.exp2(c - m_new) for c in cols]
            l_tile = functools.reduce(jnp.add, ps)
            l_ref[i] = l_tile if prev is None else alpha * l_ref[i] + l_tile
            for c in range(n_col):
                p_out[i, :, c * LANES:(c + 1) * LANES] = ps[c].astype(BF16)
            if prev is not None:
                p_prev, v_prev = prev
                pv = jnp.dot(p_prev[i], v_prev, preferred_element_type=F32)
                acc_ref[i] = (acc_ref[i] + pv) * alpha

    def tile(ref, t):
        return ref[pl.ds(pl.multiple_of(t * KV_TILE, KV_TILE), KV_TILE), :]

    def finish(pending, p_meta):
        meta_mask = lax.broadcasted_iota(jnp.int32, wide, 1) < N_META
        block(km_ref[...], meta_mask, p_meta, pending)
        for i in range(2):
            acc_ref[i] = acc_ref[i] + jnp.dot(p_meta[i, :, :LANES], vm_ref[...], preferred_element_type=F32)

    acc_ref[...] = jnp.zeros(acc_ref.shape, F32)
    causal = (lax.broadcasted_iota(jnp.int32, (Q_TILE, KV_TILE), 1)
              <= lax.broadcasted_iota(jnp.int32, (Q_TILE, KV_TILE), 0))
    block(tile(k_ref, qi), causal, pa_ref, None)

    def run(t0, n):
        for d in range(n):
            before = jnp.where(t0 == 0, qi, t0 - 1) if d == 0 else t0 + d - 1
            src, dst = (pa_ref, pb_ref) if d % 2 == 0 else (pb_ref, pa_ref)
            block(tile(k_ref, t0 + d), None, dst, (src, tile(v_ref, before)))

    def unrolled(jj, carry):
        run(ATTN_UNROLL * jj, ATTN_UNROLL)
        return carry

    lax.fori_loop(0, qi // ATTN_UNROLL, unrolled, 0)
    for n in (ATTN_UNROLL // 2, ATTN_UNROLL // 4):
        if n >= 2:
            done = (qi // (2 * n)) * (2 * n)
            pl.when(qi - done >= n)(functools.partial(run, done, n))
    n_even = (qi // 2) * 2
    pend_a = jnp.where(n_even == 0, qi, n_even - 1)

    @pl.when(qi % 2 == 1)
    def _():
        block(tile(k_ref, qi - 1), None, pb_ref, (pa_ref, tile(v_ref, pend_a)))
        finish((pb_ref, tile(v_ref, qi - 1)), pa_ref)

    @pl.when(qi % 2 == 0)
    def _():
        finish((pa_ref, tile(v_ref, pend_a)), pb_ref)

    lam = (jnp.exp(jnp.sum(lq1_ref[...] * lk1_ref[...], axis=1, keepdims=True))
           - jnp.exp(jnp.sum(lq2_ref[...] * lk2_ref[...], axis=1, keepdims=True)) + LAM_INIT)
    l1 = jnp.sum(l_ref[0], axis=1, keepdims=True)
    l2 = jnp.sum(l_ref[1], axis=1, keepdims=True)
    o = acc_ref[0] / l1 - lam * (acc_ref[1] / l2)
    o_ref[q_rows, :] = (_rms(o, sg_ref[...]) * (1.0 - LAM_INIT)).astype(BF16)


def _attention(q, k, v, km, vm, lq1, lk1, lq2, lk2, sg, bsz, seq):
    assert Q_TILE == KV_TILE and seq % Q_TILE == 0
    kvmap = lambda b, h: (b, h)
    metamap = lambda b, h: (0, h)
    const = lambda b, h: (0, 0)
    vec = pl.BlockSpec((1, QK_DIM), const)
    return pl.pallas_call(
        _attn_kernel,
        grid=(bsz, N_HEADS),
        in_specs=[
            pl.BlockSpec((seq, LANES), kvmap),
            pl.BlockSpec((seq, LANES), kvmap),
            pl.BlockSpec((seq, LANES), kvmap),
            pl.BlockSpec((LANES, LANES), metamap),
            pl.BlockSpec((LANES, LANES), metamap),
            vec, vec, vec, vec,
            pl.BlockSpec((1, V_DIM), const),
        ],
        out_specs=pl.BlockSpec((seq, LANES), kvmap),
        out_shape=jax.ShapeDtypeStruct((bsz * seq, ATTN_WIDTH), BF16),
        scratch_shapes=[
            pltpu.VMEM((2, Q_TILE, LANES), F32),
            pltpu.VMEM((2, Q_TILE, LANES), F32),
            pltpu.VMEM((2, Q_TILE, V_DIM), F32),
            pltpu.VMEM((2, Q_TILE, KV_TILE), BF16),
            pltpu.VMEM((2, Q_TILE, KV_TILE), BF16),
        ],
        compiler_params=pltpu.CompilerParams(
            dimension_semantics=("arbitrary", "arbitrary"), vmem_limit_bytes=VMEM_LIMIT),
        name="diff_attention",
    )(q, k, v, km, vm, lq1, lk1, lq2, lk2, sg)


def _ssm_kernel(u_ref, um_ref, kc_ref, bc_ref, ccr_ref, cci_ref, ap_ref, d_ref, y_ref,
                tz_ref, bs_ref, cs_ref, ub_ref, buf_a, buf_b, *, n_chunks):
    n_steps = n_chunks.bit_length() - 1

    @pl.when((pl.program_id(0) == 0) & (pl.program_id(1) == 0))
    def _():
        buf_a[:SCAN_PAD, :] = jnp.zeros((SCAN_PAD, OCT_STATE), F32)
        buf_b[:SCAN_PAD, :] = jnp.zeros((SCAN_PAD, OCT_STATE), F32)

    @pl.when(pl.program_id(1) == 0)
    def _():
        def group_of(shape, axis):
            return (lax.broadcasted_iota(jnp.int32, shape, axis) // SSM_GROUP) % OCT

        same = group_of((LANES, LANES), 0) == group_of((LANES, LANES), 1)
        lag_blk = [jnp.where(same, kc_ref[n, 0], 0.0).astype(BF16) for n in range(CHUNK)]
        for m in range(CHUNK // 2):
            tz_ref[m, :LANES, :LANES] = lag_blk[2 * m]
            tz_ref[m, :LANES, LANES:] = lag_blk[2 * m + 1]
            tz_ref[m, LANES:, :LANES] = lag_blk[2 * m - 1] if m else jnp.zeros((LANES, LANES), BF16)
            tz_ref[m, LANES:, LANES:] = lag_blk[2 * m]
        bc = bc_ref[:, 0].reshape(OCT_IN, 4 * SSM_STATE)
        g_row = group_of(bc.shape, 0)
        w_col = (lax.broadcasted_iota(jnp.int32, bc.shape, 1) // SSM_STATE) % 2
        for qq in range(OCT // 2):
            bs_ref[:, qq * MXU_DIM:(qq + 1) * MXU_DIM] = jnp.where(g_row == 2 * qq + w_col, bc, 0.0).astype(BF16)
        g_col = group_of((SSM_STATE, LANES), 1)
        for s in range(CHUNK):
            c_s = (ccr_ref[s + 1], -cci_ref[s + 1])
            for qq in range(OCT // 2):
                for part in range(2):
                    for w in range(2):
                        r0 = qq * MXU_DIM + part * LANES + w * SSM_STATE
                        cs_ref[r0:r0 + SSM_STATE, s * LANES:(s + 1) * LANES] = jnp.where(
                            g_col == 2 * qq + w, c_s[part], 0.0).astype(BF16)

    def slab(ref, t, rows):
        return ref[pl.ds(t, rows, stride=CHUNK), :]

    for t in range(CHUNK):
        ub_ref[t // 2, :, (t % 2) * LANES:(t % 2 + 1) * LANES] = slab(u_ref, t, n_chunks).astype(BF16)

    z = jnp.dot(ub_ref[0], bs_ref[:MXU_DIM, :], preferred_element_type=F32)
    for tt in range(1, CHUNK // 2):
        z = z + jnp.dot(ub_ref[tt], bs_ref[tt * MXU_DIM:(tt + 1) * MXU_DIM, :], preferred_element_type=F32)
    buf_a[SCAN_PAD:, :] = z
    um = jnp.concatenate([jnp.broadcast_to(um_ref[t:t + 1, :], (8, LANES)) for t in range(CHUNK)], axis=1)
    zm = jnp.dot(um.astype(BF16), bs_ref[...], preferred_element_type=F32)[0:1, :]
    buf_a[SCAN_PAD - 1:SCAN_PAD, :] = zm
    buf_b[SCAN_PAD - 1:SCAN_PAD, :] = zm

    src, dst = buf_a, buf_b
    for step in range(n_steps):
        shift = 1 << step
        for pair in range(OCT // 2):
            re = slice(pair * 2 * LANES, pair * 2 * LANES + LANES)
            im = slice(pair * 2 * LANES + LANES, (pair + 1) * 2 * LANES)
            ar = ap_ref[0, step:step + 1, pair * LANES:(pair + 1) * LANES]
            ai = ap_ref[0, n_steps + step:n_steps + step + 1, pair * LANES:(pair + 1) * LANES]
            xr = src[SCAN_PAD:, re]
            xi = src[SCAN_PAD:, im]
            pr = src[SCAN_PAD - shift:SCAN_PAD - shift + n_chunks, re]
            pi = src[SCAN_PAD - shift:SCAN_PAD - shift + n_chunks, im]
            dst[SCAN_PAD:, re] = xr + ar * pr - ai * pi
            dst[SCAN_PAD:, im] = xi + ar * pi + ai * pr
        src, dst = dst, src

    start_state = src[SCAN_PAD - 1:SCAN_PAD - 1 + n_chunks, :].astype(BF16)
    for ss in range(CHUNK // 2):
        cols = slice(ss * MXU_DIM, (ss + 1) * MXU_DIM)
        acc = jnp.dot(start_state, cs_ref[:, cols], preferred_element_type=F32)
        for tt in range(ss + 1):
            acc = acc + jnp.dot(ub_ref[tt], tz_ref[ss - tt], preferred_element_type=F32)
        for half in range(2):
            t = 2 * ss + half
            y = acc[:, half * LANES:(half + 1) * LANES] + d_ref[0] * slab(u_ref, t, n_chunks)
            y_ref[pl.ds(t, n_chunks, stride=CHUNK), :] = 0.5 * y * (1.0 + lax.erf(y * (1.0 / math.sqrt(2.0))))


def _ssm(u, u_meta, kc, bc, cc_re, cc_im, ap, dp, bsz, seq):
    n_chunks = seq // CHUNK
    assert n_chunks & (n_chunks - 1) == 0 and n_chunks // 2 <= SCAN_PAD and N_META == CHUNK
    n_steps = n_chunks.bit_length() - 1
    octet = lambda o, b: (o, 0, 0)
    scan_buf = pltpu.VMEM((SCAN_PAD + n_chunks, OCT_STATE), F32)
    return pl.pallas_call(
        functools.partial(_ssm_kernel, n_chunks=n_chunks),
        grid=(N_OCT, bsz),
        in_specs=[
            pl.BlockSpec((seq, LANES), lambda o, b: (b, o)),
            pl.BlockSpec((N_META, LANES), lambda o, b: (0, o)),
            pl.BlockSpec((CHUNK, 1, LANES, LANES), lambda o, b: (0, o, 0, 0)),
            pl.BlockSpec((CHUNK, 1, LANES, 4 * SSM_STATE), lambda o, b: (0, o, 0, 0)),
            pl.BlockSpec((CHUNK + 1, SSM_STATE, LANES), lambda o, b: (0, 0, o)),
            pl.BlockSpec((CHUNK + 1, SSM_STATE, LANES), lambda o, b: (0, 0, o)),
            pl.BlockSpec((1, 2 * n_steps, OCT_STATE // 2), octet),
            pl.BlockSpec((1, 1, LANES), octet),
        ],
        out_specs=pl.BlockSpec((seq, LANES), lambda o, b: (b, o)),
        out_shape=jax.ShapeDtypeStruct((bsz * seq, SSM_WIDTH), F32),
        scratch_shapes=[
            pltpu.VMEM((CHUNK // 2, MXU_DIM, MXU_DIM), BF16),
            pltpu.VMEM((OCT_IN, OCT_STATE), BF16),
            pltpu.VMEM((OCT_STATE, OCT_IN), BF16),
            pltpu.VMEM((CHUNK // 2, n_chunks, MXU_DIM), BF16), scan_buf, scan_buf],
        compiler_params=pltpu.CompilerParams(
            dimension_semantics=("arbitrary", "arbitrary"), vmem_limit_bytes=VMEM_LIMIT),
        name="s5_chunked",
    )(u, u_meta, kc, bc, cc_re, cc_im, ap, dp)


def _ssm_weights(a_re, a_im, log_dt, b_re, b_im, c_re, c_im, d_skip, n_steps):
    dt = jnp.exp(log_dt)[:, None]
    lam_re, lam_im = a_re * dt, a_im * dt

    def power(n):
        n = n[:, None, None]
        mag = jnp.exp(n * lam_re)
        return mag * jnp.cos(n * lam_im), mag * jnp.sin(n * lam_im)

    ab_re, ab_im = jnp.exp(lam_re) * jnp.cos(lam_im), jnp.exp(lam_re) * jnp.sin(lam_im)
    den = a_re * a_re + a_im * a_im
    nr, ni = ab_re - 1.0, ab_im
    f_re = (nr * a_re + ni * a_im) / den
    f_im = (ni * a_re - nr * a_im) / den
    bb_re = f_re[..., None] * b_re - f_im[..., None] * b_im
    bb_im = f_re[..., None] * b_im + f_im[..., None] * b_re

    p_re, p_im = power(jnp.arange(CHUNK + 1, dtype=F32))

    ct_re, ct_im = c_re.transpose(2, 0, 1)[None], c_im.transpose(2, 0, 1)[None]
    pt_re, pt_im = p_re.transpose(0, 2, 1)[..., None], p_im.transpose(0, 2, 1)[..., None]
    ca_re = ct_re * pt_re - ct_im * pt_im
    ca_im = ct_re * pt_im + ct_im * pt_re
    cc_re = ca_re.reshape(CHUNK + 1, SSM_STATE, SSM_WIDTH)
    cc_im = ca_im.reshape(CHUNK + 1, SSM_STATE, SSM_WIDTH)

    bp_re = bb_re.transpose(1, 0, 2)[None, :, :, :, None]
    bp_im = bb_im.transpose(1, 0, 2)[None, :, :, :, None]
    kern = jnp.sum(ca_re[:CHUNK, :, :, None, :] * bp_re - ca_im[:CHUNK, :, :, None, :] * bp_im, axis=1)
    repeat = jnp.tile(jnp.eye(SSM_GROUP, dtype=F32), (1, OCT))
    kc = jnp.dot(kern.reshape(-1, SSM_GROUP), repeat, precision=lax.Precision.HIGHEST)
    kc = kc.reshape(CHUNK, N_OCT, LANES, LANES)

    r_re, r_im = p_re[CHUNK - 1::-1][:, :, None, :], p_im[CHUNK - 1::-1][:, :, None, :]
    bt_re, bt_im = bb_re.transpose(0, 2, 1)[None], bb_im.transpose(0, 2, 1)[None]
    bs_re = r_re * bt_re - r_im * bt_im
    bs_im = r_re * bt_im + r_im * bt_re
    bc = jnp.concatenate([bs_re, bs_re, bs_im, bs_im], axis=-1).reshape(CHUNK, N_OCT, LANES, 4 * SSM_STATE)

    s_re, s_im = power(CHUNK * (2.0 ** jnp.arange(n_steps, dtype=F32)))
    pack_ap = lambda m: m.reshape(n_steps, N_OCT, OCT_STATE // 2).transpose(1, 0, 2)
    ap = jnp.concatenate([pack_ap(s_re), pack_ap(s_im)], axis=1)

    dp = d_skip.reshape(N_OCT, 1, LANES)
    return kc, bc, cc_re, cc_im, ap, dp


def _out_ffn_kernel(x_ref, o_ref, y_ref, wglu_ref, bglu_ref, sg_ref, wout_ref, pmg_ref, pfg_ref,
                    wg_ref, wu_ref, wd_ref, pog_ref, out_ref):
    y = y_ref[...]
    gate = jnp.dot(y.astype(BF16), wglu_ref[...], preferred_element_type=F32) + bglu_ref[...]
    y = _rms(y * jax.nn.sigmoid(gate), sg_ref[...]).astype(BF16)
    mix = (jnp.dot(o_ref[...], wout_ref[:ATTN_WIDTH, :], preferred_element_type=F32)
           + jnp.dot(y, wout_ref[ATTN_WIDTH:, :], preferred_element_type=F32))
    h1 = x_ref[...] + _rms(mix, pmg_ref[...])
    h2 = _rms(h1, pfg_ref[...]).astype(BF16)
    g = jnp.dot(h2, wg_ref[...], preferred_element_type=F32)
    up = jnp.dot(h2, wu_ref[...], preferred_element_type=F32)
    f = (g * jax.nn.sigmoid(g) * up).astype(BF16)
    f = jnp.dot(f, wd_ref[...], preferred_element_type=F32)
    out_ref[...] = h1 + _rms(f, pog_ref[...])


def _out_ffn(x2d, o, y, wglu, bglu, sg, wout, pmg, pfg, wg, wu, wd, pog, tm):
    n = x2d.shape[0]
    row = lambda i: (i, 0)
    const = lambda i: (0, 0)

    def resident(shape):
        return pl.BlockSpec(shape, const, pipeline_mode=pl.Buffered(1))

    return pl.pallas_call(
        _out_ffn_kernel,
        grid=(n // tm,),
        in_specs=[
            pl.BlockSpec((tm, D_MODEL), row),
            pl.BlockSpec((tm, ATTN_WIDTH), row),
            pl.BlockSpec((tm, SSM_WIDTH), row),
            resident((SSM_WIDTH, SSM_WIDTH)),
            resident((1, SSM_WIDTH)),
            resident((1, SSM_WIDTH)),
            resident((D_MODEL, D_MODEL)),
            resident((1, D_MODEL)),
            resident((1, D_MODEL)),
            resident((D_MODEL, D_FF)),
            resident((D_MODEL, D_FF)),
            resident((D_FF, D_MODEL)),
            resident((1, D_MODEL)),
        ],
        out_specs=pl.BlockSpec((tm, D_MODEL), row),
        out_shape=jax.ShapeDtypeStruct((n, D_MODEL), F32),
        compiler_params=pltpu.CompilerParams(
            dimension_semantics=("arbitrary",), vmem_limit_bytes=VMEM_LIMIT),
        name="out_ffn",
    )(x2d, o, y, wglu, bglu, sg, wout, pmg, pfg, wg, wu, wd, pog)


def _rope_tables(start, length):
    half = ROT_DIM // 2
    pos = jnp.arange(start, start + length, dtype=F32)
    inv_freq = ROPE_THETA ** (-jnp.arange(0, ROT_DIM, 2, dtype=F32) / ROT_DIM)
    d = jnp.arange(LANES) % QK_DIM
    freq = jnp.where(d < ROT_DIM, inv_freq[d % half], 0.0)
    ang = pos[:, None] * freq[None, :]
    sin = jnp.sin(ang)
    sa = jnp.where(d < half, -1.0, 0.0)
    sb = jnp.where((d >= half) & (d < ROT_DIM), 1.0, 0.0)
    return jnp.cos(ang), sin * sa[None, :], sin * sb[None, :]


def _permute_w_in(w):
    qk = QK_DIM * N_HEADS
    scale = math.log2(math.e) / math.sqrt(QK_DIM)
    q1, q2, k1, k2 = (w[:, i * qk:(i + 1) * qk].reshape(D_MODEL, N_HEADS, QK_DIM) for i in range(4))
    qp = (jnp.concatenate([q1, q2], axis=2) * scale).reshape(D_MODEL, 2 * qk)
    kp = jnp.concatenate([k1, k2], axis=2).reshape(D_MODEL, 2 * qk)
    return jnp.concatenate([qp, kp, w[:, 4 * qk:]], axis=1).astype(BF16)


def kernel(x, meta, pre_mix_g, w_in, lambda_q1, lambda_k1, lambda_q2, lambda_k2, subln_g, a_re, a_im, log_dt,
           b_re, b_im, c_re, c_im, d_skip, w_glu, b_glu, ssm_out_g, w_out, post_mix_g, pre_ffn_g, w_gate,
           w_up, w_down, post_ffn_g):
    bsz, seq, _ = x.shape
    n = bsz * seq
    x2d = x.reshape(n, D_MODEL)
    row = lambda t: t[0].reshape(1, -1)

    w_in_p = _permute_w_in(w_in[0])
    g0 = row(pre_mix_g)
    q, k, v, u = _in_proj(x2d, g0, w_in_p, *_rope_tables(N_META, seq), ROW_TILE)
    _, k_m, v_m, u_m = _in_proj(meta, g0, w_in_p, *_rope_tables(0, N_META), N_META)

    pad_meta = lambda t: jnp.pad(t, ((0, LANES - N_META), (0, 0)))
    o = _attention(q, k, v, pad_meta(k_m), pad_meta(v_m), lambda_q1, lambda_k1, lambda_q2, lambda_k2,
                   row(subln_g), bsz, seq)

    n_steps = (seq // CHUNK).bit_length() - 1
    ssm_w = _ssm_weights(a_re[0], a_im[0], log_dt[0], b_re[0], b_im[0], c_re[0], c_im[0], d_skip[0], n_steps)
    y = _ssm(u, u_m, *ssm_w, bsz, seq)

    out = _out_ffn(x2d, o, y, w_glu[0].astype(BF16), row(b_glu), row(ssm_out_g), w_out[0].astype(BF16),
                   row(post_mix_g), row(pre_ffn_g), w_gate[0].astype(BF16), w_up[0].astype(BF16),
                   w_down[0].astype(BF16), row(post_ffn_g), ROW_TILE)
    return out.reshape(bsz, seq, D_MODEL)
```

```python
import functools
import math

import jax
import jax.numpy as jnp
from jax import lax
from jax.experimental import pallas as pl
from jax.experimental.pallas import tpu as pltpu

D_MODEL = 1024
N_META = 16
N_HEADS = 4
QK_DIM = 64
V_DIM = 128
ROT_DIM = 16
ROPE_THETA = 500000.0
SSM_GROUP = 16
N_GROUPS = 32
SSM_STATE = 64
SSM_WIDTH = 512
ATTN_WIDTH = 512
D_FF = 2816
EPS = 1e-6
LAM_INIT = 0.8 - 0.6 * math.exp(-0.3 * 0)

LANES = 128
MXU_DIM = 256
CHUNK = 16
OCT = LANES // SSM_GROUP
N_OCT = N_GROUPS // OCT
OCT_IN = CHUNK * LANES
OCT_STATE = OCT * 2 * SSM_STATE
SCAN_PAD = 256

ROW_TILE = 512
Q_TILE = 512
KV_TILE = 512
ATTN_UNROLL = 4
VMEM_LIMIT = 56 * 1024 * 1024

F32 = jnp.float32
BF16 = jnp.bfloat16


def _rms(x, g):
    return x * lax.rsqrt(jnp.mean(x * x, axis=-1, keepdims=True) + EPS) * g


def _in_proj_kernel(x_ref, g_ref, w_ref, cos_ref, sa_ref, sb_ref, q_ref, k_ref, v_ref, u_ref):
    h = _rms(x_ref[...], g_ref[...]).astype(BF16)
    proj = jnp.dot(h, w_ref[...], preferred_element_type=F32)
    cos, sa, sb = cos_ref[...], sa_ref[...], sb_ref[...]

    def rope(t):
        return (t * cos + pltpu.roll(t, LANES - ROT_DIM // 2, axis=1) * sa
                + pltpu.roll(t, ROT_DIM // 2, axis=1) * sb)

    for c in range(ATTN_WIDTH // LANES):
        sl = slice(c * LANES, (c + 1) * LANES)
        q_ref[:, sl] = rope(proj[:, c * LANES:(c + 1) * LANES]).astype(BF16)
        k_ref[:, sl] = rope(proj[:, ATTN_WIDTH + c * LANES:ATTN_WIDTH + (c + 1) * LANES]).astype(BF16)
    v_ref[...] = proj[:, 2 * ATTN_WIDTH:3 * ATTN_WIDTH].astype(BF16)
    u_ref[...] = proj[:, 3 * ATTN_WIDTH:]


def _in_proj(x2d, g, w, cos, sa, sb, tm):
    n = x2d.shape[0]
    n_pos = cos.shape[0] // tm
    row = lambda i: (i, 0)
    pos = lambda i: (i % n_pos, 0)
    const = lambda i: (0, 0)
    return pl.pallas_call(
        _in_proj_kernel,
        grid=(n // tm,),
        in_specs=[
            pl.BlockSpec((tm, D_MODEL), row),
            pl.BlockSpec((1, D_MODEL), const),
            pl.BlockSpec((D_MODEL, 4 * ATTN_WIDTH), const),
            pl.BlockSpec((tm, LANES), pos),
            pl.BlockSpec((tm, LANES), pos),
            pl.BlockSpec((tm, LANES), pos),
        ],
        out_specs=[pl.BlockSpec((tm, ATTN_WIDTH), row)] * 4,
        out_shape=[jax.ShapeDtypeStruct((n, ATTN_WIDTH), BF16)] * 3
        + [jax.ShapeDtypeStruct((n, SSM_WIDTH), F32)],
        compiler_params=pltpu.CompilerParams(
            dimension_semantics=("arbitrary",), vmem_limit_bytes=VMEM_LIMIT),
        name="in_proj",
    )(x2d, g, w, cos, sa, sb)


def _attn_kernel(*refs):
    n_tiles = refs[0].shape[0] // Q_TILE

    def q_tile(qi, carry):
        _attn_q_tile(qi, *refs)
        return carry

    lax.fori_loop(0, n_tiles, q_tile, 0)


def _attn_q_tile(qi, q1_ref, q2_ref, k1_ref, k2_ref, v_ref, km1_ref, km2_ref, vm_ref,
                 lq1_ref, lk1_ref, lq2_ref, lk2_ref, sg_ref,
                 o_ref, m_ref, l_ref, acc_ref, pa_ref, pb_ref):
    q_rows = pl.ds(pl.multiple_of(qi * Q_TILE, Q_TILE), Q_TILE)
    own = lax.broadcasted_iota(jnp.int32, (Q_TILE, LANES), 1) // QK_DIM == pl.program_id(1) % 2
    qs = tuple(jnp.where(own, r[q_rows, :], jnp.zeros((Q_TILE, LANES), BF16)) for r in (q1_ref, q2_ref))
    k_refs, km_refs = (k1_ref, k2_ref), (km1_ref, km2_ref)
    nt = (((1,), (1,)), ((), ()))
    wide = (Q_TILE, LANES)

    def block(k, mask, p_out, prev):
        n_col = k[0].shape[0] // LANES
        for i in range(2):
            s = lax.dot_general(qs[i], k[i], nt, preferred_element_type=F32)
            if mask is not None:
                s = jnp.where(mask, s, -jnp.inf)
            cols = [s[:, c * LANES:(c + 1) * LANES] for c in range(n_col)]
            m_tile = functools.reduce(jnp.maximum, cols)
            m_new = jnp.broadcast_to(jnp.max(m_tile, axis=1, keepdims=True), wide)
            if prev is not None:
                m_old = m_ref[i]
                m_new = jnp.maximum(m_old, m_new)
                alpha = jnp.exp2(m_old - m_new)
            m_ref[i] = m_new
            ps = [jnp.exp2(c - m_new) for c in cols]
            l_tile = functools.reduce(jnp.add, ps)
            l_ref[i] = l_tile if prev is None else alpha * l_ref[i] + l_tile
            for c in range(n_col):
                p_out[i, :, c * LANES:(c + 1) * LANES] = ps[c].astype(BF16)
            if prev is not None:
                p_prev, v_prev = prev
                pv = jnp.dot(p_prev[i], v_prev, preferred_element_type=F32)
                acc_ref[i] = (acc_ref[i] + pv) * alpha

    def tile(ref, t):
        return ref[pl.ds(pl.multiple_of(t * KV_TILE, KV_TILE), KV_TILE), :]

    def ktile(t):
        return tuple(tile(r, t) for r in k_refs)

    def finish(pending, p_meta):
        meta_mask = lax.broadcasted_iota(jnp.int32, wide, 1) < N_META
        block(tuple(r[...] for r in km_refs), meta_mask, p_meta, pending)
        for i in range(2):
            acc_ref[i] = acc_ref[i] + jnp.dot(p_meta[i, :, :LANES], vm_ref[...], preferred_element_type=F32)

    acc_ref[...] = jnp.zeros(acc_ref.shape, F32)
    causal = (lax.broadcasted_iota(jnp.int32, (Q_TILE, KV_TILE), 1)
              <= lax.broadcasted_iota(jnp.int32, (Q_TILE, KV_TILE), 0))
    block(ktile(qi), causal, pa_ref, None)

    def run(t0, n):
        for d in range(n):
            before = jnp.where(t0 == 0, qi, t0 - 1) if d == 0 else t0 + d - 1
            src, dst = (pa_ref, pb_ref) if d % 2 == 0 else (pb_ref, pa_ref)
            block(ktile(t0 + d), None, dst, (src, tile(v_ref, before)))

    def unrolled(jj, carry):
        run(ATTN_UNROLL * jj, ATTN_UNROLL)
        return carry

    lax.fori_loop(0, qi // ATTN_UNROLL, unrolled, 0)
    for n in (ATTN_UNROLL // 2, ATTN_UNROLL // 4):
        if n >= 2:
            done = (qi // (2 * n)) * (2 * n)
            pl.when(qi - done >= n)(functools.partial(run, done, n))
    n_even = (qi // 2) * 2
    pend_a = jnp.where(n_even == 0, qi, n_even - 1)

    @pl.when(qi % 2 == 1)
    def _():
        block(ktile(qi - 1), None, pb_ref, (pa_ref, tile(v_ref, pend_a)))
        finish((pb_ref, tile(v_ref, qi - 1)), pa_ref)

    @pl.when(qi % 2 == 0)
    def _():
        finish((pa_ref, tile(v_ref, pend_a)), pb_ref)

    lam = (jnp.exp(jnp.sum(lq1_ref[...] * lk1_ref[...], axis=1, keepdims=True))
           - jnp.exp(jnp.sum(lq2_ref[...] * lk2_ref[...], axis=1, keepdims=True)) + LAM_INIT)
    l1 = jnp.sum(l_ref[0], axis=1, keepdims=True)
    l2 = jnp.sum(l_ref[1], axis=1, keepdims=True)
    o = acc_ref[0] / l1 - lam * (acc_ref[1] / l2)
    o_ref[q_rows, :] = (_rms(o, sg_ref[...]) * (1.0 - LAM_INIT)).astype(BF16)


def _attention(q, k, v, km, vm, lq1, lk1, lq2, lk2, sg, bsz, seq):
    assert Q_TILE == KV_TILE and seq % Q_TILE == 0
    pairs = N_HEADS // 2
    kvmap = lambda b, h: (b, h)
    map1 = pl.BlockSpec((seq, LANES), lambda b, h: (b, h // 2))
    map2 = pl.BlockSpec((seq, LANES), lambda b, h: (b, pairs + h // 2))
    meta1 = pl.BlockSpec((LANES, LANES), lambda b, h: (0, h // 2))
    meta2 = pl.BlockSpec((LANES, LANES), lambda b, h: (0, pairs + h // 2))
    const = lambda b, h: (0, 0)
    vec = pl.BlockSpec((1, QK_DIM), const)
    return pl.pallas_call(
        _attn_kernel,
        grid=(bsz, N_HEADS),
        in_specs=[
            map1, map2, map1, map2,
            pl.BlockSpec((seq, LANES), kvmap),
            meta1, meta2,
            pl.BlockSpec((LANES, LANES), lambda b, h: (0, h)),
            vec, vec, vec, vec,
            pl.BlockSpec((1, V_DIM), const),
        ],
        out_specs=pl.BlockSpec((seq, LANES), kvmap),
        out_shape=jax.ShapeDtypeStruct((bsz * seq, ATTN_WIDTH), BF16),
        scratch_shapes=[
            pltpu.VMEM((2, Q_TILE, LANES), F32),
            pltpu.VMEM((2, Q_TILE, LANES), F32),
            pltpu.VMEM((2, Q_TILE, V_DIM), F32),
            pltpu.VMEM((2, Q_TILE, KV_TILE), BF16),
            pltpu.VMEM((2, Q_TILE, KV_TILE), BF16),
        ],
        compiler_params=pltpu.CompilerParams(
            dimension_semantics=("arbitrary", "arbitrary"), vmem_limit_bytes=VMEM_LIMIT),
        name="diff_attention",
    )(q, q, k, k, v, km, km, vm, lq1, lk1, lq2, lk2, sg)


def _ssm_kernel(u_ref, um_ref, kc_ref, bc_ref, ccr_ref, cci_ref, ap_ref, d_ref, y_ref,
                tz_ref, bs_ref, cs_ref, ub_ref, buf_a, buf_b, yt_ref, *, n_chunks):
    n_steps = n_chunks.bit_length() - 1

    @pl.when((pl.program_id(0) == 0) & (pl.program_id(1) == 0))
    def _():
        buf_a[:SCAN_PAD, :] = jnp.zeros((SCAN_PAD, OCT_STATE), F32)
        buf_b[:SCAN_PAD, :] = jnp.zeros((SCAN_PAD, OCT_STATE), F32)

    @pl.when(pl.program_id(1) == 0)
    def _():
        def group_of(shape, axis):
            return (lax.broadcasted_iota(jnp.int32, shape, axis) // SSM_GROUP) % OCT

        same = group_of((LANES, LANES), 0) == group_of((LANES, LANES), 1)
        lag_blk = [jnp.where(same, kc_ref[n, 0], 0.0).astype(BF16) for n in range(CHUNK)]
        for m in range(CHUNK // 2):
            tz_ref[m, :LANES, :LANES] = lag_blk[2 * m]
            tz_ref[m, :LANES, LANES:] = lag_blk[2 * m + 1]
            tz_ref[m, LANES:, :LANES] = lag_blk[2 * m - 1] if m else jnp.zeros((LANES, LANES), BF16)
            tz_ref[m, LANES:, LANES:] = lag_blk[2 * m]
        bc = bc_ref[:, 0].reshape(OCT_IN, 4 * SSM_STATE)
        g_row = group_of(bc.shape, 0)
        w_col = (lax.broadcasted_iota(jnp.int32, bc.shape, 1) // SSM_STATE) % 2
        for qq in range(OCT // 2):
            bs_ref[:, qq * MXU_DIM:(qq + 1) * MXU_DIM] = jnp.where(g_row == 2 * qq + w_col, bc, 0.0).astype(BF16)
        g_col = group_of((SSM_STATE, LANES), 1)
        for s in range(CHUNK):
            c_s = (ccr_ref[s + 1], -cci_ref[s + 1])
            for qq in range(OCT // 2):
                for part in range(2):
                    for w in range(2):
                        r0 = qq * MXU_DIM + part * LANES + w * SSM_STATE
                        cs_ref[r0:r0 + SSM_STATE, s * LANES:(s + 1) * LANES] = jnp.where(
                            g_col == 2 * qq + w, c_s[part], 0.0).astype(BF16)

    def slab(ref, t, rows):
        return ref[pl.ds(t, rows, stride=CHUNK), :]

    for t in range(CHUNK):
        ub_ref[t // 2, :, (t % 2) * LANES:(t % 2 + 1) * LANES] = slab(u_ref, t, n_chunks).astype(BF16)

    z = jnp.dot(ub_ref[0], bs_ref[:MXU_DIM, :], preferred_element_type=F32)
    for tt in range(1, CHUNK // 2):
        z = z + jnp.dot(ub_ref[tt], bs_ref[tt * MXU_DIM:(tt + 1) * MXU_DIM, :], preferred_element_type=F32)
    buf_a[SCAN_PAD:, :] = z
    um = jnp.concatenate([jnp.broadcast_to(um_ref[t:t + 1, :], (8, LANES)) for t in range(CHUNK)], axis=1)
    zm = jnp.dot(um.astype(BF16), bs_ref[...], preferred_element_type=F32)[0:1, :]
    buf_a[SCAN_PAD - 1:SCAN_PAD, :] = zm
    buf_b[SCAN_PAD - 1:SCAN_PAD, :] = zm

    def within_chunk(ss):
        acc = jnp.dot(ub_ref[0], tz_ref[ss], preferred_element_type=F32)
        for tt in range(1, ss + 1):
            acc = acc + jnp.dot(ub_ref[tt], tz_ref[ss - tt], preferred_element_type=F32)
        yt_ref[ss] = acc

    assert n_steps >= CHUNK // 2
    src, dst = buf_a, buf_b
    for step in range(n_steps):
        shift = 1 << step
        if step < CHUNK // 2:
            within_chunk(CHUNK // 2 - 1 - step)
        for pair in range(OCT // 2):
            re = slice(pair * 2 * LANES, pair * 2 * LANES + LANES)
            im = slice(pair * 2 * LANES + LANES, (pair + 1) * 2 * LANES)
            ar = ap_ref[0, step:step + 1, pair * LANES:(pair + 1) * LANES]
            ai = ap_ref[0, n_steps + step:n_steps + step + 1, pair * LANES:(pair + 1) * LANES]
            xr = src[SCAN_PAD:, re]
            xi = src[SCAN_PAD:, im]
            pr = src[SCAN_PAD - shift:SCAN_PAD - shift + n_chunks, re]
            pi = src[SCAN_PAD - shift:SCAN_PAD - shift + n_chunks, im]
            dst[SCAN_PAD:, re] = xr + ar * pr - ai * pi
            dst[SCAN_PAD:, im] = xi + ar * pi + ai * pr
        src, dst = dst, src

    start_state = src[SCAN_PAD - 1:SCAN_PAD - 1 + n_chunks, :].astype(BF16)
    for ss in range(CHUNK // 2):
        cols = slice(ss * MXU_DIM, (ss + 1) * MXU_DIM)
        acc = yt_ref[ss] + jnp.dot(start_state, cs_ref[:, cols], preferred_element_type=F32)
        for half in range(2):
            t = 2 * ss + half
            y = acc[:, half * LANES:(half + 1) * LANES] + d_ref[0] * slab(u_ref, t, n_chunks)
            y_ref[pl.ds(t, n_chunks, stride=CHUNK), :] = 0.5 * y * (1.0 + lax.erf(y * (1.0 / math.sqrt(2.0))))


def _ssm(u, u_meta, kc, bc, cc_re, cc_im, ap, dp, bsz, seq):
    n_chunks = seq // CHUNK
    assert n_chunks & (n_chunks - 1) == 0 and n_chunks // 2 <= SCAN_PAD and N_META == CHUNK
    n_steps = n_chunks.bit_length() - 1
    octet = lambda o, b: (o, 0, 0)
    scan_buf = pltpu.VMEM((SCAN_PAD + n_chunks, OCT_STATE), F32)
    return pl.pallas_call(
        functools.partial(_ssm_kernel, n_chunks=n_chunks),
        grid=(N_OCT, bsz),
        in_specs=[
            pl.BlockSpec((seq, LANES), lambda o, b: (b, o)),
            pl.BlockSpec((N_META, LANES), lambda o, b: (0, o)),
            pl.BlockSpec((CHUNK, 1, LANES, LANES), lambda o, b: (0, o, 0, 0)),
            pl.BlockSpec((CHUNK, 1, LANES, 4 * SSM_STATE), lambda o, b: (0, o, 0, 0)),
            pl.BlockSpec((CHUNK + 1, SSM_STATE, LANES), lambda o, b: (0, 0, o)),
            pl.BlockSpec((CHUNK + 1, SSM_STATE, LANES), lambda o, b: (0, 0, o)),
            pl.BlockSpec((1, 2 * n_steps, OCT_STATE // 2), octet),
            pl.BlockSpec((1, 1, LANES), octet),
        ],
        out_specs=pl.BlockSpec((seq, LANES), lambda o, b: (b, o)),
        out_shape=jax.ShapeDtypeStruct((bsz * seq, SSM_WIDTH), F32),
        scratch_shapes=[
            pltpu.VMEM((CHUNK // 2, MXU_DIM, MXU_DIM), BF16),
            pltpu.VMEM((OCT_IN, OCT_STATE), BF16),
            pltpu.VMEM((OCT_STATE, OCT_IN), BF16),
            pltpu.VMEM((CHUNK // 2, n_chunks, MXU_DIM), BF16), scan_buf, scan_buf,
            pltpu.VMEM((CHUNK // 2, n_chunks, MXU_DIM), F32)],
        compiler_params=pltpu.CompilerParams(
            dimension_semantics=("arbitrary", "arbitrary"), vmem_limit_bytes=VMEM_LIMIT),
        name="s5_chunked",
    )(u, u_meta, kc, bc, cc_re, cc_im, ap, dp)


def _ssm_weights(a_re, a_im, log_dt, b_re, b_im, c_re, c_im, d_skip, n_steps):
    dt = jnp.exp(log_dt)[:, None]
    lam_re, lam_im = a_re * dt, a_im * dt

    def power(n):
        n = n[:, None, None]
        mag = jnp.exp(n * lam_re)
        return mag * jnp.cos(n * lam_im), mag * jnp.sin(n * lam_im)

    ab_re, ab_im = jnp.exp(lam_re) * jnp.cos(lam_im), jnp.exp(lam_re) * jnp.sin(lam_im)
    den = a_re * a_re + a_im * a_im
    nr, ni = ab_re - 1.0, ab_im
    f_re = (nr * a_re + ni * a_im) / den
    f_im = (ni * a_re - nr * a_im) / den
    bb_re = f_re[..., None] * b_re - f_im[..., None] * b_im
    bb_im = f_re[..., None] * b_im + f_im[..., None] * b_re

    p_re, p_im = power(jnp.arange(CHUNK + 1, dtype=F32))

    ct_re, ct_im = c_re.transpose(2, 0, 1)[None], c_im.transpose(2, 0, 1)[None]
    pt_re, pt_im = p_re.transpose(0, 2, 1)[..., None], p_im.transpose(0, 2, 1)[..., None]
    ca_re = ct_re * pt_re - ct_im * pt_im
    ca_im = ct_re * pt_im + ct_im * pt_re
    cc_re = ca_re.reshape(CHUNK + 1, SSM_STATE, SSM_WIDTH)
    cc_im = ca_im.reshape(CHUNK + 1, SSM_STATE, SSM_WIDTH)

    bp_re = bb_re.transpose(1, 0, 2)[None, :, :, :, None]
    bp_im = bb_im.transpose(1, 0, 2)[None, :, :, :, None]
    kern = jnp.sum(ca_re[:CHUNK, :, :, None, :] * bp_re - ca_im[:CHUNK, :, :, None, :] * bp_im, axis=1)
    repeat = jnp.tile(jnp.eye(SSM_GROUP, dtype=F32), (1, OCT))
    kc = jnp.dot(kern.reshape(-1, SSM_GROUP), repeat, precision=lax.Precision.HIGHEST)
    kc = kc.reshape(CHUNK, N_OCT, LANES, LANES)

    r_re, r_im = p_re[CHUNK - 1::-1][:, :, None, :], p_im[CHUNK - 1::-1][:, :, None, :]
    bt_re, bt_im = bb_re.transpose(0, 2, 1)[None], bb_im.transpose(0, 2, 1)[None]
    bs_re = r_re * bt_re - r_im * bt_im
    bs_im = r_re * bt_im + r_im * bt_re
    bc = jnp.concatenate([bs_re, bs_re, bs_im, bs_im], axis=-1).reshape(CHUNK, N_OCT, LANES, 4 * SSM_STATE)

    s_re, s_im = power(CHUNK * (2.0 ** jnp.arange(n_steps, dtype=F32)))
    pack_ap = lambda m: m.reshape(n_steps, N_OCT, OCT_STATE // 2).transpose(1, 0, 2)
    ap = jnp.concatenate([pack_ap(s_re), pack_ap(s_im)], axis=1)

    dp = d_skip.reshape(N_OCT, 1, LANES)
    return kc, bc, cc_re, cc_im, ap, dp


def _out_ffn_kernel(x_ref, o_ref, y_ref, wglu_ref, bglu_ref, sg_ref, wout_ref, pmg_ref, pfg_ref,
                    wg_ref, wu_ref, wd_ref, pog_ref, out_ref):
    y = y_ref[...]
    gate = jnp.dot(y.astype(BF16), wglu_ref[...], preferred_element_type=F32) + bglu_ref[...]
    y = _rms(y * jax.nn.sigmoid(gate), sg_ref[...]).astype(BF16)
    mix = (jnp.dot(o_ref[...], wout_ref[:ATTN_WIDTH, :], preferred_element_type=F32)
           + jnp.dot(y, wout_ref[ATTN_WIDTH:, :], preferred_element_type=F32))
    h1 = x_ref[...] + _rms(mix, pmg_ref[...])
    h2 = _rms(h1, pfg_ref[...]).astype(BF16)
    g = jnp.dot(h2, wg_ref[...], preferred_element_type=F32)
    up = jnp.dot(h2, wu_ref[...], preferred_element_type=F32)
    f = (g * jax.nn.sigmoid(g) * up).astype(BF16)
    f = jnp.dot(f, wd_ref[...], preferred_element_type=F32)
    out_ref[...] = h1 + _rms(f, pog_ref[...])


def _out_ffn(x2d, o, y, wglu, bglu, sg, wout, pmg, pfg, wg, wu, wd, pog, tm):
    n = x2d.shape[0]
    row = lambda i: (i, 0)
    const = lambda i: (0, 0)

    def resident(shape):
        return pl.BlockSpec(shape, const, pipeline_mode=pl.Buffered(1))

    return pl.pallas_call(
        _out_ffn_kernel,
        grid=(n // tm,),
        in_specs=[
            pl.BlockSpec((tm, D_MODEL), row),
            pl.BlockSpec((tm, ATTN_WIDTH), row),
            pl.BlockSpec((tm, SSM_WIDTH), row),
            resident((SSM_WIDTH, SSM_WIDTH)),
            resident((1, SSM_WIDTH)),
            resident((1, SSM_WIDTH)),
            resident((D_MODEL, D_MODEL)),
            resident((1, D_MODEL)),
            resident((1, D_MODEL)),
            resident((D_MODEL, D_FF)),
            resident((D_MODEL, D_FF)),
            resident((D_FF, D_MODEL)),
            resident((1, D_MODEL)),
        ],
        out_specs=pl.BlockSpec((tm, D_MODEL), row),
        out_shape=jax.ShapeDtypeStruct((n, D_MODEL), F32),
        compiler_params=pltpu.CompilerParams(
            dimension_semantics=("arbitrary",), vmem_limit_bytes=VMEM_LIMIT),
        name="out_ffn",
    )(x2d, o, y, wglu, bglu, sg, wout, pmg, pfg, wg, wu, wd, pog)


def _rope_tables(start, length):
    half = ROT_DIM // 2
    pos = jnp.arange(start, start + length, dtype=F32)
    d = jnp.arange(LANES) % QK_DIM
    inv_freq = ROPE_THETA ** (-(2 * (d % half)).astype(F32) / ROT_DIM)
    freq = jnp.where(d < ROT_DIM, inv_freq, 0.0)
    ang = pos[:, None] * freq[None, :]
    sin = jnp.sin(ang)
    sa = jnp.where(d < half, -1.0, 0.0)
    sb = jnp.where((d >= half) & (d < ROT_DIM), 1.0, 0.0)
    return jnp.cos(ang), sin * sa[None, :], sin * sb[None, :]


def _scale_w_in(w):
    scale = math.log2(math.e) / math.sqrt(QK_DIM)
    col_scale = jnp.where(jnp.arange(w.shape[1]) < ATTN_WIDTH, scale, 1.0).astype(F32)
    return (w * col_scale[None, :]).astype(BF16)


def kernel(x, meta, pre_mix_g, w_in, lambda_q1, lambda_k1, lambda_q2, lambda_k2, subln_g, a_re, a_im, log_dt,
           b_re, b_im, c_re, c_im, d_skip, w_glu, b_glu, ssm_out_g, w_out, post_mix_g, pre_ffn_g, w_gate,
           w_up, w_down, post_ffn_g):
    bsz, seq, _ = x.shape
    n = bsz * seq
    x2d = x.reshape(n, D_MODEL)
    row = lambda t: t[0].reshape(1, -1)

    w_in_p = _scale_w_in(w_in[0])
    g0 = row(pre_mix_g)
    q, k, v, u = _in_proj(x2d, g0, w_in_p, *_rope_tables(N_META, seq), ROW_TILE)
    _, k_m, v_m, u_m = _in_proj(meta, g0, w_in_p, *_rope_tables(0, N_META), N_META)

    pad_meta = lambda t: jnp.pad(t, ((0, LANES - N_META), (0, 0)))
    o = _attention(q, k, v, pad_meta(k_m), pad_meta(v_m), lambda_q1, lambda_k1, lambda_q2, lambda_k2,
                   row(subln_g), bsz, seq)

    n_steps = (seq // CHUNK).bit_length() - 1
    ssm_w = _ssm_weights(a_re[0], a_im[0], log_dt[0], b_re[0], b_im[0], c_re[0], c_im[0], d_skip[0], n_steps)
    y = _ssm(u, u_m, *ssm_w, bsz, seq)

    out = _out_ffn(x2d, o, y, w_glu[0].astype(BF16), row(b_glu), row(ssm_out_g), w_out[0].astype(BF16),
                   row(post_mix_g), row(pre_ffn_g), w_gate[0].astype(BF16), w_up[0].astype(BF16),
                   w_down[0].astype(BF16), row(post_ffn_g), ROW_TILE)
    return out.reshape(bsz, seq, D_MODEL)
```

```python
import functools
import math

import jax
import jax.numpy as jnp
from jax import lax
from jax.experimental import pallas as pl
from jax.experimental.pallas import tpu as pltpu

D_MODEL = 1024
N_META = 16
N_HEADS = 4
QK_DIM = 64
V_DIM = 128
ROT_DIM = 16
ROPE_THETA = 500000.0
SSM_GROUP = 16
N_GROUPS = 32
SSM_STATE = 64
SSM_WIDTH = 512
ATTN_WIDTH = 512
D_FF = 2816
EPS = 1e-6
LAM_INIT = 0.8 - 0.6 * math.exp(-0.3 * 0)

LANES = 128
MXU_DIM = 256
CHUNK = 16
OCT = LANES // SSM_GROUP
N_OCT = N_GROUPS // OCT
OCT_IN = CHUNK * LANES
OCT_STATE = OCT * 2 * SSM_STATE
SCAN_PAD = 256

ROW_TILE = 512
Q_TILE = 512
KV_TILE = 512
ATTN_UNROLL = 4
VMEM_LIMIT = 56 * 1024 * 1024

F32 = jnp.float32
BF16 = jnp.bfloat16


def _rms(x, g):
    return x * lax.rsqrt(jnp.mean(x * x, axis=-1, keepdims=True) + EPS) * g


def _in_proj_kernel(x_ref, g_ref, w_ref, cos_ref, sa_ref, sb_ref, q_ref, k_ref, v_ref, u_ref):
    h = _rms(x_ref[...], g_ref[...]).astype(BF16)
    proj = jnp.dot(h, w_ref[...], preferred_element_type=F32)
    cos, sa, sb = cos_ref[...], sa_ref[...], sb_ref[...]

    def rope(t):
        return (t * cos + pltpu.roll(t, LANES - ROT_DIM // 2, axis=1) * sa
                + pltpu.roll(t, ROT_DIM // 2, axis=1) * sb)

    for c in range(ATTN_WIDTH // LANES):
        sl = slice(c * LANES, (c + 1) * LANES)
        q_ref[:, sl] = rope(proj[:, c * LANES:(c + 1) * LANES]).astype(BF16)
        k_ref[:, sl] = rope(proj[:, ATTN_WIDTH + c * LANES:ATTN_WIDTH + (c + 1) * LANES]).astype(BF16)
    v_ref[...] = proj[:, 2 * ATTN_WIDTH:3 * ATTN_WIDTH].astype(BF16)
    u_ref[...] = proj[:, 3 * ATTN_WIDTH:]


def _in_proj(x2d, g, w, cos, sa, sb, tm):
    n = x2d.shape[0]
    n_pos = cos.shape[0] // tm
    row = lambda i: (i, 0)
    pos = lambda i: (i % n_pos, 0)
    const = lambda i: (0, 0)
    return pl.pallas_call(
        _in_proj_kernel,
        grid=(n // tm,),
        in_specs=[
            pl.BlockSpec((tm, D_MODEL), row),
            pl.BlockSpec((1, D_MODEL), const),
            pl.BlockSpec((D_MODEL, 4 * ATTN_WIDTH), const),
            pl.BlockSpec((tm, LANES), pos),
            pl.BlockSpec((tm, LANES), pos),
            pl.BlockSpec((tm, LANES), pos),
        ],
        out_specs=[pl.BlockSpec((tm, ATTN_WIDTH), row)] * 4,
        out_shape=[jax.ShapeDtypeStruct((n, ATTN_WIDTH), BF16)] * 3
        + [jax.ShapeDtypeStruct((n, SSM_WIDTH), F32)],
        compiler_params=pltpu.CompilerParams(
            dimension_semantics=("arbitrary",), vmem_limit_bytes=VMEM_LIMIT),
        name="in_proj",
    )(x2d, g, w, cos, sa, sb)


def _attn_kernel(*refs):
    n_tiles = refs[0].shape[0] // Q_TILE

    def q_tile(qi, carry):
        _attn_q_tile(qi, *refs)
        return carry

    lax.fori_loop(0, n_tiles, q_tile, 0)


def _attn_q_tile(qi, q1_ref, q2_ref, k1_ref, k2_ref, v_ref, km1_ref, km2_ref, vm_ref,
                 lq1_ref, lk1_ref, lq2_ref, lk2_ref, sg_ref,
                 o_ref, m_ref, l_ref, acc_ref, pa_ref, pb_ref):
    q_rows = pl.ds(pl.multiple_of(qi * Q_TILE, Q_TILE), Q_TILE)
    own = lax.broadcasted_iota(jnp.int32, (Q_TILE, LANES), 1) // QK_DIM == pl.program_id(1) % 2
    qs = tuple(jnp.where(own, r[q_rows, :], jnp.zeros((Q_TILE, LANES), BF16)) for r in (q1_ref, q2_ref))
    k_refs, km_refs = (k1_ref, k2_ref), (km1_ref, km2_ref)
    nt = (((1,), (1,)), ((), ()))
    wide = (Q_TILE, LANES)

    def block(k, mask, p_out, prev):
        n_col = k[0].shape[0] // LANES
        for i in range(2):
            s = lax.dot_general(qs[i], k[i], nt, preferred_element_type=F32)
            if mask is not None:
                s = jnp.where(mask, s, -jnp.inf)
            cols = [s[:, c * LANES:(c + 1) * LANES] for c in range(n_col)]
            m_tile = functools.reduce(jnp.maximum, cols)
            m_new = jnp.broadcast_to(jnp.max(m_tile, axis=1, keepdims=True), wide)
            if prev is not None:
                m_old = m_ref[i]
                m_new = jnp.maximum(m_old, m_new)
                alpha = jnp.exp2(m_old - m_new)
            m_ref[i] = m_new
            ps = [jnp.exp2(c - m_new) for c in cols]
            l_tile = functools.reduce(jnp.add, ps)
            l_ref[i] = l_tile if prev is None else alpha * l_ref[i] + l_tile
            for c in range(n_col):
                p_out[i, :, c * LANES:(c + 1) * LANES] = ps[c].astype(BF16)
            if prev is not None:
                p_prev, v_prev = prev
                pv = jnp.dot(p_prev[i], v_prev, preferred_element_type=F32)
                acc_ref[i] = (acc_ref[i] + pv) * alpha

    def tile(ref, t):
        return ref[pl.ds(pl.multiple_of(t * KV_TILE, KV_TILE), KV_TILE), :]

    def ktile(t):
        return tuple(tile(r, t) for r in k_refs)

    def finish(pending, p_meta):
        meta_mask = lax.broadcasted_iota(jnp.int32, wide, 1) < N_META
        block(tuple(r[...] for r in km_refs), meta_mask, p_meta, pending)
        for i in range(2):
            acc_ref[i] = acc_ref[i] + jnp.dot(p_meta[i, :, :LANES], vm_ref[...], preferred_element_type=F32)

    acc_ref[...] = jnp.zeros(acc_ref.shape, F32)
    causal = (lax.broadcasted_iota(jnp.int32, (Q_TILE, KV_TILE), 1)
              <= lax.broadcasted_iota(jnp.int32, (Q_TILE, KV_TILE), 0))
    block(ktile(qi), causal, pa_ref, None)

    def run(t0, n):
        slots = (pa_ref, pb_ref)
        before = jnp.where(t0 == 0, qi, t0 - 1)
        for d in range(n):
            block(ktile(t0 + d), None, slots[(d + 1) % 2], (slots[d % 2], tile(v_ref, before)))
            before = t0 + d
        return before

    def unrolled(jj, carry):
        run(ATTN_UNROLL * jj, ATTN_UNROLL)
        return carry

    lax.fori_loop(0, qi // ATTN_UNROLL, unrolled, 0)
    done = (qi // ATTN_UNROLL) * ATTN_UNROLL

    def tail(left):
        before = run(done, left)
        slots = (pa_ref, pb_ref)
        finish((slots[left % 2], tile(v_ref, before)), slots[(left + 1) % 2])

    for left in range(ATTN_UNROLL):
        pl.when(qi - done == left)(functools.partial(tail, left))

    lam = (jnp.exp(jnp.sum(lq1_ref[...] * lk1_ref[...], axis=1, keepdims=True))
           - jnp.exp(jnp.sum(lq2_ref[...] * lk2_ref[...], axis=1, keepdims=True)) + LAM_INIT)
    l1 = jnp.sum(l_ref[0], axis=1, keepdims=True)
    l2 = jnp.sum(l_ref[1], axis=1, keepdims=True)
    o = acc_ref[0] / l1 - lam * (acc_ref[1] / l2)
    o_ref[q_rows, :] = (_rms(o, sg_ref[...]) * (1.0 - LAM_INIT)).astype(BF16)


def _attention(q, k, v, km, vm, lq1, lk1, lq2, lk2, sg, bsz, seq):
    assert Q_TILE == KV_TILE and seq % Q_TILE == 0
    pairs = N_HEADS // 2
    kvmap = lambda b, h: (b, h)
    map1 = pl.BlockSpec((seq, LANES), lambda b, h: (b, h // 2))
    map2 = pl.BlockSpec((seq, LANES), lambda b, h: (b, pairs + h // 2))
    meta1 = pl.BlockSpec((LANES, LANES), lambda b, h: (0, h // 2))
    meta2 = pl.BlockSpec((LANES, LANES), lambda b, h: (0, pairs + h // 2))
    const = lambda b, h: (0, 0)
    vec = pl.BlockSpec((1, QK_DIM), const)
    return pl.pallas_call(
        _attn_kernel,
        grid=(bsz, N_HEADS),
        in_specs=[
            map1, map2, map1, map2,
            pl.BlockSpec((seq, LANES), kvmap),
            meta1, meta2,
            pl.BlockSpec((LANES, LANES), lambda b, h: (0, h)),
            vec, vec, vec, vec,
            pl.BlockSpec((1, V_DIM), const),
        ],
        out_specs=pl.BlockSpec((seq, LANES), kvmap),
        out_shape=jax.ShapeDtypeStruct((bsz * seq, ATTN_WIDTH), BF16),
        scratch_shapes=[
            pltpu.VMEM((2, Q_TILE, LANES), F32),
            pltpu.VMEM((2, Q_TILE, LANES), F32),
            pltpu.VMEM((2, Q_TILE, V_DIM), F32),
            pltpu.VMEM((2, Q_TILE, KV_TILE), BF16),
            pltpu.VMEM((2, Q_TILE, KV_TILE), BF16),
        ],
        compiler_params=pltpu.CompilerParams(
            dimension_semantics=("arbitrary", "arbitrary"), vmem_limit_bytes=VMEM_LIMIT),
        name="diff_attention",
    )(q, q, k, k, v, km, km, vm, lq1, lk1, lq2, lk2, sg)


def _ssm_kernel(u_ref, um_ref, kc_ref, bc_ref, ccr_ref, cci_ref, ap_ref, d_ref, y_ref,
                tz_ref, bs_ref, cs_ref, ub_ref, buf_a, buf_b, yt_ref, *, n_chunks):
    n_steps = n_chunks.bit_length() - 1

    @pl.when((pl.program_id(0) == 0) & (pl.program_id(1) == 0))
    def _():
        buf_a[:SCAN_PAD, :] = jnp.zeros((SCAN_PAD, OCT_STATE), F32)
        buf_b[:SCAN_PAD, :] = jnp.zeros((SCAN_PAD, OCT_STATE), F32)

    @pl.when(pl.program_id(1) == 0)
    def _():
        def group_of(shape, axis):
            return (lax.broadcasted_iota(jnp.int32, shape, axis) // SSM_GROUP) % OCT

        same = group_of((LANES, LANES), 0) == group_of((LANES, LANES), 1)
        lag_blk = [jnp.where(same, kc_ref[n, 0], 0.0).astype(BF16) for n in range(CHUNK)]
        for m in range(CHUNK // 2):
            tz_ref[m, :LANES, :LANES] = lag_blk[2 * m]
            tz_ref[m, :LANES, LANES:] = lag_blk[2 * m + 1]
            tz_ref[m, LANES:, :LANES] = lag_blk[2 * m - 1] if m else jnp.zeros((LANES, LANES), BF16)
            tz_ref[m, LANES:, LANES:] = lag_blk[2 * m]
        bc = bc_ref[:, 0].reshape(OCT_IN, 4 * SSM_STATE)
        g_row = group_of(bc.shape, 0)
        w_col = (lax.broadcasted_iota(jnp.int32, bc.shape, 1) // SSM_STATE) % 2
        for qq in range(OCT // 2):
            bs_ref[:, qq * MXU_DIM:(qq + 1) * MXU_DIM] = jnp.where(g_row == 2 * qq + w_col, bc, 0.0).astype(BF16)
        g_col = group_of((SSM_STATE, LANES), 1)
        for s in range(CHUNK):
            c_s = (ccr_ref[s + 1], -cci_ref[s + 1])
            for qq in range(OCT // 2):
                for part in range(2):
                    for w in range(2):
                        r0 = qq * MXU_DIM + part * LANES + w * SSM_STATE
                        cs_ref[r0:r0 + SSM_STATE, s * LANES:(s + 1) * LANES] = jnp.where(
                            g_col == 2 * qq + w, c_s[part], 0.0).astype(BF16)

    def slab(ref, t, rows):
        return ref[pl.ds(t, rows, stride=CHUNK), :]

    for t in range(CHUNK):
        ub_ref[t // 2, :, (t % 2) * LANES:(t % 2 + 1) * LANES] = slab(u_ref, t, n_chunks).astype(BF16)

    z = jnp.dot(ub_ref[0], bs_ref[:MXU_DIM, :], preferred_element_type=F32)
    for tt in range(1, CHUNK // 2):
        z = z + jnp.dot(ub_ref[tt], bs_ref[tt * MXU_DIM:(tt + 1) * MXU_DIM, :], preferred_element_type=F32)
    buf_a[SCAN_PAD:, :] = z
    um = jnp.concatenate([jnp.broadcast_to(um_ref[t:t + 1, :], (8, LANES)) for t in range(CHUNK)], axis=1)
    zm = jnp.dot(um.astype(BF16), bs_ref[...], preferred_element_type=F32)[0:1, :]
    buf_a[SCAN_PAD - 1:SCAN_PAD, :] = zm
    buf_b[SCAN_PAD - 1:SCAN_PAD, :] = zm

    def within_chunk(ss):
        acc = jnp.dot(ub_ref[0], tz_ref[ss], preferred_element_type=F32)
        for tt in range(1, ss + 1):
            acc = acc + jnp.dot(ub_ref[tt], tz_ref[ss - tt], preferred_element_type=F32)
        yt_ref[ss] = acc

    assert n_steps >= CHUNK // 2
    src, dst = buf_a, buf_b
    for step in range(n_steps):
        shift = 1 << step
        if step < CHUNK // 2:
            within_chunk(CHUNK // 2 - 1 - step)
        for pair in range(OCT // 2):
            re = slice(pair * 2 * LANES, pair * 2 * LANES + LANES)
            im = slice(pair * 2 * LANES + LANES, (pair + 1) * 2 * LANES)
            ar = ap_ref[0, step:step + 1, pair * LANES:(pair + 1) * LANES]
            ai = ap_ref[0, n_steps + step:n_steps + step + 1, pair * LANES:(pair + 1) * LANES]
            xr = src[SCAN_PAD:, re]
            xi = src[SCAN_PAD:, im]
            pr = src[SCAN_PAD - shift:SCAN_PAD - shift + n_chunks, re]
            pi = src[SCAN_PAD - shift:SCAN_PAD - shift + n_chunks, im]
            dst[SCAN_PAD:, re] = xr + ar * pr - ai * pi
            dst[SCAN_PAD:, im] = xi + ar * pi + ai * pr
        src, dst = dst, src

    start_state = src[SCAN_PAD - 1:SCAN_PAD - 1 + n_chunks, :].astype(BF16)
    for ss in range(CHUNK // 2):
        cols = slice(ss * MXU_DIM, (ss + 1) * MXU_DIM)
        acc = yt_ref[ss] + jnp.dot(start_state, cs_ref[:, cols], preferred_element_type=F32)
        for half in range(2):
            t = 2 * ss + half
            y = acc[:, half * LANES:(half + 1) * LANES] + d_ref[0] * slab(u_ref, t, n_chunks)
            y_ref[pl.ds(t, n_chunks, stride=CHUNK), :] = 0.5 * y * (1.0 + lax.erf(y * (1.0 / math.sqrt(2.0))))


def _ssm(u, u_meta, kc, bc, cc_re, cc_im, ap, dp, bsz, seq):
    n_chunks = seq // CHUNK
    assert n_chunks & (n_chunks - 1) == 0 and n_chunks // 2 <= SCAN_PAD and N_META == CHUNK
    n_steps = n_chunks.bit_length() - 1
    octet = lambda o, b: (o, 0, 0)
    scan_buf = pltpu.VMEM((SCAN_PAD + n_chunks, OCT_STATE), F32)
    return pl.pallas_call(
        functools.partial(_ssm_kernel, n_chunks=n_chunks),
        grid=(N_OCT, bsz),
        in_specs=[
            pl.BlockSpec((seq, LANES), lambda o, b: (b, o)),
            pl.BlockSpec((N_META, LANES), lambda o, b: (0, o)),
            pl.BlockSpec((CHUNK, 1, LANES, LANES), lambda o, b: (0, o, 0, 0)),
            pl.BlockSpec((CHUNK, 1, LANES, 4 * SSM_STATE), lambda o, b: (0, o, 0, 0)),
            pl.BlockSpec((CHUNK + 1, SSM_STATE, LANES), lambda o, b: (0, 0, o)),
            pl.BlockSpec((CHUNK + 1, SSM_STATE, LANES), lambda o, b: (0, 0, o)),
            pl.BlockSpec((1, 2 * n_steps, OCT_STATE // 2), octet),
            pl.BlockSpec((1, 1, LANES), octet),
        ],
        out_specs=pl.BlockSpec((seq, LANES), lambda o, b: (b, o)),
        out_shape=jax.ShapeDtypeStruct((bsz * seq, SSM_WIDTH), F32),
        scratch_shapes=[
            pltpu.VMEM((CHUNK // 2, MXU_DIM, MXU_DIM), BF16),
            pltpu.VMEM((OCT_IN, OCT_STATE), BF16),
            pltpu.VMEM((OCT_STATE, OCT_IN), BF16),
            pltpu.VMEM((CHUNK // 2, n_chunks, MXU_DIM), BF16), scan_buf, scan_buf,
            pltpu.VMEM((CHUNK // 2, n_chunks, MXU_DIM), F32)],
        compiler_params=pltpu.CompilerParams(
            dimension_semantics=("arbitrary", "arbitrary"), vmem_limit_bytes=VMEM_LIMIT),
        name="s5_chunked",
    )(u, u_meta, kc, bc, cc_re, cc_im, ap, dp)


def _ssm_weights(a_re, a_im, log_dt, b_re, b_im, c_re, c_im, d_skip, n_steps):
    dt = jnp.exp(log_dt)[:, None]
    lam_re, lam_im = a_re * dt, a_im * dt

    def power(n):
        n = n[:, None, None]
        mag = jnp.exp(n * lam_re)
        return mag * jnp.cos(n * lam_im), mag * jnp.sin(n * lam_im)

    ab_re, ab_im = jnp.exp(lam_re) * jnp.cos(lam_im), jnp.exp(lam_re) * jnp.sin(lam_im)
    den = a_re * a_re + a_im * a_im
    nr, ni = ab_re - 1.0, ab_im
    f_re = (nr * a_re + ni * a_im) / den
    f_im = (ni * a_re - nr * a_im) / den
    bb_re = f_re[..., None] * b_re - f_im[..., None] * b_im
    bb_im = f_re[..., None] * b_im + f_im[..., None] * b_re

    p_re, p_im = power(jnp.arange(CHUNK + 1, dtype=F32))

    ct_re, ct_im = c_re.transpose(2, 0, 1)[None], c_im.transpose(2, 0, 1)[None]
    pt_re, pt_im = p_re.transpose(0, 2, 1)[..., None], p_im.transpose(0, 2, 1)[..., None]
    ca_re = ct_re * pt_re - ct_im * pt_im
    ca_im = ct_re * pt_im + ct_im * pt_re
    cc_re = ca_re.reshape(CHUNK + 1, SSM_STATE, SSM_WIDTH)
    cc_im = ca_im.reshape(CHUNK + 1, SSM_STATE, SSM_WIDTH)

    bp_re = bb_re.transpose(1, 0, 2)[None, :, :, :, None]
    bp_im = bb_im.transpose(1, 0, 2)[None, :, :, :, None]
    kern = jnp.sum(ca_re[:CHUNK, :, :, None, :] * bp_re - ca_im[:CHUNK, :, :, None, :] * bp_im, axis=1)
    repeat = jnp.tile(jnp.eye(SSM_GROUP, dtype=F32), (1, OCT))
    kc = jnp.dot(kern.reshape(-1, SSM_GROUP), repeat, precision=lax.Precision.HIGHEST)
    kc = kc.reshape(CHUNK, N_OCT, LANES, LANES)

    r_re, r_im = p_re[CHUNK - 1::-1][:, :, None, :], p_im[CHUNK - 1::-1][:, :, None, :]
    bt_re, bt_im = bb_re.transpose(0, 2, 1)[None], bb_im.transpose(0, 2, 1)[None]
    bs_re = r_re * bt_re - r_im * bt_im
    bs_im = r_re * bt_im + r_im * bt_re
    bc = jnp.concatenate([bs_re, bs_re, bs_im, bs_im], axis=-1).reshape(CHUNK, N_OCT, LANES, 4 * SSM_STATE)

    s_re, s_im = power(CHUNK * (2.0 ** jnp.arange(n_steps, dtype=F32)))
    pack_ap = lambda m: m.reshape(n_steps, N_OCT, OCT_STATE // 2).transpose(1, 0, 2)
    ap = jnp.concatenate([pack_ap(s_re), pack_ap(s_im)], axis=1)

    dp = d_skip.reshape(N_OCT, 1, LANES)
    return kc, bc, cc_re, cc_im, ap, dp


def _out_ffn_kernel(x_ref, o_ref, y_ref, wglu_ref, bglu_ref, sg_ref, wout_ref, pmg_ref, pfg_ref,
                    wg_ref, wu_ref, wd_ref, pog_ref, out_ref):
    y = y_ref[...]
    gate = jnp.dot(y.astype(BF16), wglu_ref[...], preferred_element_type=F32) + bglu_ref[...]
    y = _rms(y * jax.nn.sigmoid(gate), sg_ref[...]).astype(BF16)
    mix = (jnp.dot(o_ref[...], wout_ref[:ATTN_WIDTH, :], preferred_element_type=F32)
           + jnp.dot(y, wout_ref[ATTN_WIDTH:, :], preferred_element_type=F32))
    h1 = x_ref[...] + _rms(mix, pmg_ref[...])
    h2 = _rms(h1, pfg_ref[...]).astype(BF16)
    g = jnp.dot(h2, wg_ref[...], preferred_element_type=F32)
    up = jnp.dot(h2, wu_ref[...], preferred_element_type=F32)
    f = (g * jax.nn.sigmoid(g) * up).astype(BF16)
    f = jnp.dot(f, wd_ref[...], preferred_element_type=F32)
    out_ref[...] = h1 + _rms(f, pog_ref[...])


def _out_ffn(x2d, o, y, wglu, bglu, sg, wout, pmg, pfg, wg, wu, wd, pog, tm):
    n = x2d.shape[0]
    row = lambda i: (i, 0)
    const = lambda i: (0, 0)

    def resident(shape):
        return pl.BlockSpec(shape, const, pipeline_mode=pl.Buffered(1))

    return pl.pallas_call(
        _out_ffn_kernel,
        grid=(n // tm,),
        in_specs=[
            pl.BlockSpec((tm, D_MODEL), row),
            pl.BlockSpec((tm, ATTN_WIDTH), row),
            pl.BlockSpec((tm, SSM_WIDTH), row),
            resident((SSM_WIDTH, SSM_WIDTH)),
            resident((1, SSM_WIDTH)),
            resident((1, SSM_WIDTH)),
            resident((D_MODEL, D_MODEL)),
            resident((1, D_MODEL)),
            resident((1, D_MODEL)),
            resident((D_MODEL, D_FF)),
            resident((D_MODEL, D_FF)),
            resident((D_FF, D_MODEL)),
            resident((1, D_MODEL)),
        ],
        out_specs=pl.BlockSpec((tm, D_MODEL), row),
        out_shape=jax.ShapeDtypeStruct((n, D_MODEL), F32),
        compiler_params=pltpu.CompilerParams(
            dimension_semantics=("arbitrary",), vmem_limit_bytes=VMEM_LIMIT),
        name="out_ffn",
    )(x2d, o, y, wglu, bglu, sg, wout, pmg, pfg, wg, wu, wd, pog)


def _rope_tables(start, length):
    half = ROT_DIM // 2
    pos = jnp.arange(start, start + length, dtype=F32)
    d = jnp.arange(LANES) % QK_DIM
    inv_freq = ROPE_THETA ** (-(2 * (d % half)).astype(F32) / ROT_DIM)
    freq = lax.optimization_barrier(jnp.where(d < ROT_DIM, inv_freq, 0.0))
    ang = pos[:, None] * freq[None, :]
    sin = jnp.sin(ang)
    sa = jnp.where(d < half, -1.0, 0.0)
    sb = jnp.where((d >= half) & (d < ROT_DIM), 1.0, 0.0)
    return jnp.cos(ang), sin * sa[None, :], sin * sb[None, :]


def _scale_w_in(w):
    scale = math.log2(math.e) / math.sqrt(QK_DIM)
    col_scale = jnp.where(jnp.arange(w.shape[1]) < ATTN_WIDTH, scale, 1.0).astype(F32)
    return (w * col_scale[None, :]).astype(BF16)


def kernel(x, meta, pre_mix_g, w_in, lambda_q1, lambda_k1, lambda_q2, lambda_k2, subln_g, a_re, a_im, log_dt,
           b_re, b_im, c_re, c_im, d_skip, w_glu, b_glu, ssm_out_g, w_out, post_mix_g, pre_ffn_g, w_gate,
           w_up, w_down, post_ffn_g):
    bsz, seq, _ = x.shape
    n = bsz * seq
    x2d = x.reshape(n, D_MODEL)
    row = lambda t: t[0].reshape(1, -1)

    w_in_p = _scale_w_in(w_in[0])
    g0 = row(pre_mix_g)
    q, k, v, u = _in_proj(x2d, g0, w_in_p, *_rope_tables(N_META, seq), ROW_TILE)
    _, k_m, v_m, u_m = _in_proj(meta, g0, w_in_p, *_rope_tables(0, N_META), N_META)

    pad_meta = lambda t: jnp.pad(t, ((0, LANES - N_META), (0, 0)))
    o = _attention(q, k, v, pad_meta(k_m), pad_meta(v_m), lambda_q1, lambda_k1, lambda_q2, lambda_k2,
                   row(subln_g), bsz, seq)

    n_steps = (seq // CHUNK).bit_length() - 1
    ssm_w = _ssm_weights(a_re[0], a_im[0], log_dt[0], b_re[0], b_im[0], c_re[0], c_im[0], d_skip[0], n_steps)
    y = _ssm(u, u_m, *ssm_w, bsz, seq)

    out = _out_ffn(x2d, o, y, w_glu[0].astype(BF16), row(b_glu), row(ssm_out_g), w_out[0].astype(BF16),
                   row(post_mix_g), row(pre_ffn_g), w_gate[0].astype(BF16), w_up[0].astype(BF16),
                   w_down[0].astype(BF16), row(post_ffn_g), ROW_TILE)
    return out.reshape(bsz, seq, D_MODEL)
```

```python
import functools
import math

import jax
import jax.numpy as jnp
from jax import lax
from jax.experimental import pallas as pl
from jax.experimental.pallas import tpu as pltpu

D_MODEL = 1024
N_META = 16
N_HEADS = 4
QK_DIM = 64
V_DIM = 128
ROT_DIM = 16
ROPE_THETA = 500000.0
SSM_GROUP = 16
N_GROUPS = 32
SSM_STATE = 64
SSM_WIDTH = 512
ATTN_WIDTH = 512
D_FF = 2816
EPS = 1e-6
LAM_INIT = 0.8 - 0.6 * math.exp(-0.3 * 0)

LANES = 128
SUBLANES = 8
MXU_DIM = 256
CHUNK = 16
OCT = LANES // SSM_GROUP
N_OCT = N_GROUPS // OCT
OCT_IN = CHUNK * LANES
OCT_STATE = OCT * 2 * SSM_STATE
SCAN_PAD = 256

ROW_TILE = 512
Q_TILE = 512
KV_TILE = 512
ATTN_UNROLL = 4
VMEM_LIMIT = 56 * 1024 * 1024

F32 = jnp.float32
BF16 = jnp.bfloat16


def _rms(x, g):
    return x * lax.rsqrt(jnp.mean(x * x, axis=-1, keepdims=True) + EPS) * g


def _in_proj_kernel(x_ref, g_ref, w_ref, cr_ref, sr_ref, cb_ref, sb_ref, q_ref, k_ref, v_ref, u_ref):
    h = _rms(x_ref[...], g_ref[...]).astype(BF16)
    proj = jnp.dot(h, w_ref[...], preferred_element_type=F32)
    cos = cb_ref[0] * cr_ref[...] - sb_ref[0] * sr_ref[...]
    sin = sb_ref[0] * cr_ref[...] + cb_ref[0] * sr_ref[...]
    d = lax.broadcasted_iota(jnp.int32, (1, LANES), 1) % QK_DIM
    sa = sin * jnp.where(d < ROT_DIM // 2, -1.0, 0.0)
    sb = sin * jnp.where((d >= ROT_DIM // 2) & (d < ROT_DIM), 1.0, 0.0)

    def rope(t):
        return (t * cos + pltpu.roll(t, LANES - ROT_DIM // 2, axis=1) * sa
                + pltpu.roll(t, ROT_DIM // 2, axis=1) * sb)

    for c in range(ATTN_WIDTH // LANES):
        sl = slice(c * LANES, (c + 1) * LANES)
        q_ref[:, sl] = rope(proj[:, c * LANES:(c + 1) * LANES]).astype(BF16)
        k_ref[:, sl] = rope(proj[:, ATTN_WIDTH + c * LANES:ATTN_WIDTH + (c + 1) * LANES]).astype(BF16)
    v_ref[...] = proj[:, 2 * ATTN_WIDTH:3 * ATTN_WIDTH].astype(BF16)
    u_ref[...] = proj[:, 3 * ATTN_WIDTH:]


def _in_proj(x2d, g, w, first_pos, seq, tm):
    n = x2d.shape[0]
    tiles_per_seq = seq // tm
    cos_r, sin_r = _rope_angles(jnp.arange(tm, dtype=F32))
    cos_b, sin_b = _rope_angles(first_pos + tm * jnp.arange(tiles_per_seq, dtype=F32))
    row = lambda i: (i, 0)
    const = lambda i: (0, 0)
    base = pl.BlockSpec((1, 1, LANES), lambda i: (i % tiles_per_seq, 0, 0))
    return pl.pallas_call(
        _in_proj_kernel,
        grid=(n // tm,),
        in_specs=[
            pl.BlockSpec((tm, D_MODEL), row),
            pl.BlockSpec((1, D_MODEL), const),
            pl.BlockSpec((D_MODEL, 4 * ATTN_WIDTH), const),
            pl.BlockSpec((tm, LANES), const),
            pl.BlockSpec((tm, LANES), const),
            base, base,
        ],
        out_specs=[pl.BlockSpec((tm, ATTN_WIDTH), row)] * 4,
        out_shape=[jax.ShapeDtypeStruct((n, ATTN_WIDTH), BF16)] * 3
        + [jax.ShapeDtypeStruct((n, SSM_WIDTH), F32)],
        compiler_params=pltpu.CompilerParams(
            dimension_semantics=("arbitrary",), vmem_limit_bytes=VMEM_LIMIT),
        name="in_proj",
    )(x2d, g, w, cos_r, sin_r, cos_b[:, None, :], sin_b[:, None, :])


def _attn_kernel(*refs):
    n_tiles = refs[0].shape[0] // Q_TILE

    def q_tile(qi, carry):
        _attn_q_tile(qi, *refs)
        return carry

    lax.fori_loop(0, n_tiles, q_tile, 0)


def _attn_q_tile(qi, q1_ref, q2_ref, k1_ref, k2_ref, v_ref, km1_ref, km2_ref, vm_ref,
                 lq1_ref, lk1_ref, lq2_ref, lk2_ref, sg_ref,
                 o_ref, m_ref, l_ref, acc_ref, pa_ref, pb_ref):
    q_rows = pl.ds(pl.multiple_of(qi * Q_TILE, Q_TILE), Q_TILE)
    own = lax.broadcasted_iota(jnp.int32, (Q_TILE, LANES), 1) // QK_DIM == pl.program_id(1) % 2
    qs = tuple(jnp.where(own, r[q_rows, :], jnp.zeros((Q_TILE, LANES), BF16)) for r in (q1_ref, q2_ref))
    k_refs, km_refs = (k1_ref, k2_ref), (km1_ref, km2_ref)
    nt = (((1,), (1,)), ((), ()))
    wide = (Q_TILE, LANES)

    def block(k, mask, p_out, prev):
        n_col = k[0].shape[0] // LANES
        for i in range(2):
            s = lax.dot_general(qs[i], k[i], nt, preferred_element_type=F32)
            if mask is not None:
                s = jnp.where(mask, s, -jnp.inf)
            cols = [s[:, c * LANES:(c + 1) * LANES] for c in range(n_col)]
            m_tile = functools.reduce(jnp.maximum, cols)
            m_new = jnp.broadcast_to(jnp.max(m_tile, axis=1, keepdims=True), wide)
            if prev is not None:
                m_old = m_ref[i]
                m_new = jnp.maximum(m_old, m_new)
                alpha = jnp.exp2(m_old - m_new)
            m_ref[i] = m_new
            ps = [jnp.exp2(c - m_new) for c in cols]
            l_tile = functools.reduce(jnp.add, ps)
            l_ref[i] = l_tile if prev is None else alpha * l_ref[i] + l_tile
            for c in range(n_col):
                p_out[i, :, c * LANES:(c + 1) * LANES] = ps[c].astype(BF16)
            if prev is not None:
                p_prev, v_prev = prev
                pv = jnp.dot(p_prev[i], v_prev, preferred_element_type=F32)
                acc_ref[i] = (acc_ref[i] + pv) * alpha

    def tile(ref, t):
        return ref[pl.ds(pl.multiple_of(t * KV_TILE, KV_TILE), KV_TILE), :]

    def ktile(t):
        return tuple(tile(r, t) for r in k_refs)

    def finish(pending, p_meta):
        meta_mask = lax.broadcasted_iota(jnp.int32, wide, 1) < N_META
        block(tuple(r[...] for r in km_refs), meta_mask, p_meta, pending)
        for i in range(2):
            acc_ref[i] = acc_ref[i] + jnp.dot(p_meta[i, :, :LANES], vm_ref[...], preferred_element_type=F32)

    acc_ref[...] = jnp.zeros(acc_ref.shape, F32)
    causal = (lax.broadcasted_iota(jnp.int32, (Q_TILE, KV_TILE), 1)
              <= lax.broadcasted_iota(jnp.int32, (Q_TILE, KV_TILE), 0))
    block(ktile(qi), causal, pa_ref, None)

    def run(t0, n):
        slots = (pa_ref, pb_ref)
        before = jnp.where(t0 == 0, qi, t0 - 1)
        for d in range(n):
            block(ktile(t0 + d), None, slots[(d + 1) % 2], (slots[d % 2], tile(v_ref, before)))
            before = t0 + d
        return before

    def unrolled(jj, carry):
        run(ATTN_UNROLL * jj, ATTN_UNROLL)
        return carry

    lax.fori_loop(0, qi // ATTN_UNROLL, unrolled, 0)
    done = (qi // ATTN_UNROLL) * ATTN_UNROLL

    def tail(left):
        before = run(done, left)
        slots = (pa_ref, pb_ref)
        finish((slots[left % 2], tile(v_ref, before)), slots[(left + 1) % 2])

    for left in range(ATTN_UNROLL):
        pl.when(qi - done == left)(functools.partial(tail, left))

    lam = (jnp.exp(jnp.sum(lq1_ref[...] * lk1_ref[...], axis=1, keepdims=True))
           - jnp.exp(jnp.sum(lq2_ref[...] * lk2_ref[...], axis=1, keepdims=True)) + LAM_INIT)
    l1 = jnp.sum(l_ref[0], axis=1, keepdims=True)
    l2 = jnp.sum(l_ref[1], axis=1, keepdims=True)
    o = acc_ref[0] / l1 - lam * (acc_ref[1] / l2)
    o_ref[q_rows, :] = (_rms(o, sg_ref[...]) * (1.0 - LAM_INIT)).astype(BF16)


def _attention(q, k, v, km, vm, lq1, lk1, lq2, lk2, sg, bsz, seq):
    assert Q_TILE == KV_TILE and seq % Q_TILE == 0
    pairs = N_HEADS // 2
    kvmap = lambda b, h: (b, h)
    map1 = pl.BlockSpec((seq, LANES), lambda b, h: (b, h // 2))
    map2 = pl.BlockSpec((seq, LANES), lambda b, h: (b, pairs + h // 2))
    meta1 = pl.BlockSpec((LANES, LANES), lambda b, h: (0, h // 2))
    meta2 = pl.BlockSpec((LANES, LANES), lambda b, h: (0, pairs + h // 2))
    const = lambda b, h: (0, 0)
    vec = pl.BlockSpec((1, QK_DIM), const)
    return pl.pallas_call(
        _attn_kernel,
        grid=(bsz, N_HEADS),
        in_specs=[
            map1, map2, map1, map2,
            pl.BlockSpec((seq, LANES), kvmap),
            meta1, meta2,
            pl.BlockSpec((LANES, LANES), lambda b, h: (0, h)),
            vec, vec, vec, vec,
            pl.BlockSpec((1, V_DIM), const),
        ],
        out_specs=pl.BlockSpec((seq, LANES), kvmap),
        out_shape=jax.ShapeDtypeStruct((bsz * seq, ATTN_WIDTH), BF16),
        scratch_shapes=[
            pltpu.VMEM((2, Q_TILE, LANES), F32),
            pltpu.VMEM((2, Q_TILE, LANES), F32),
            pltpu.VMEM((2, Q_TILE, V_DIM), F32),
            pltpu.VMEM((2, Q_TILE, KV_TILE), BF16),
            pltpu.VMEM((2, Q_TILE, KV_TILE), BF16),
        ],
        compiler_params=pltpu.CompilerParams(
            dimension_semantics=("arbitrary", "arbitrary"), vmem_limit_bytes=VMEM_LIMIT),
        name="diff_attention",
    )(q, q, k, k, v, km, km, vm, lq1, lk1, lq2, lk2, sg)


def _ssm_kernel(u_ref, um_ref, kc_ref, bc_ref, ccr_ref, cci_ref, ap_ref, d_ref, y_ref,
                tz_ref, bs_ref, cs_ref, ub_ref, buf_a, buf_b, yt_ref, *, n_chunks):
    n_steps = n_chunks.bit_length() - 1

    @pl.when((pl.program_id(0) == 0) & (pl.program_id(1) == 0))
    def _():
        buf_a[:SCAN_PAD, :] = jnp.zeros((SCAN_PAD, OCT_STATE), F32)
        buf_b[:SCAN_PAD, :] = jnp.zeros((SCAN_PAD, OCT_STATE), F32)

    @pl.when(pl.program_id(1) == 0)
    def _():
        def group_of(shape, axis):
            return (lax.broadcasted_iota(jnp.int32, shape, axis) // SSM_GROUP) % OCT

        same = group_of((LANES, LANES), 0) == group_of((LANES, LANES), 1)
        lag_blk = [jnp.where(same, kc_ref[n, 0], 0.0).astype(BF16) for n in range(CHUNK)]
        for m in range(CHUNK // 2):
            tz_ref[m, :LANES, :LANES] = lag_blk[2 * m]
            tz_ref[m, :LANES, LANES:] = lag_blk[2 * m + 1]
            tz_ref[m, LANES:, :LANES] = lag_blk[2 * m - 1] if m else jnp.zeros((LANES, LANES), BF16)
            tz_ref[m, LANES:, LANES:] = lag_blk[2 * m]
        bc = bc_ref[:, 0].reshape(OCT_IN, 4 * SSM_STATE)
        g_row = group_of(bc.shape, 0)
        w_col = (lax.broadcasted_iota(jnp.int32, bc.shape, 1) // SSM_STATE) % 2
        for qq in range(OCT // 2):
            bs_ref[:, qq * MXU_DIM:(qq + 1) * MXU_DIM] = jnp.where(g_row == 2 * qq + w_col, bc, 0.0).astype(BF16)
        g_col = group_of((SSM_STATE, LANES), 1)
        for s in range(CHUNK):
            c_s = (ccr_ref[s + 1], -cci_ref[s + 1])
            for qq in range(OCT // 2):
                for part in range(2):
                    for w in range(2):
                        r0 = qq * MXU_DIM + part * LANES + w * SSM_STATE
                        cs_ref[r0:r0 + SSM_STATE, s * LANES:(s + 1) * LANES] = jnp.where(
                            g_col == 2 * qq + w, c_s[part], 0.0).astype(BF16)

    def slab(ref, t, rows):
        return ref[pl.ds(t, rows, stride=CHUNK), :]

    for t in range(CHUNK):
        ub_ref[t // 2, :, (t % 2) * LANES:(t % 2 + 1) * LANES] = slab(u_ref, t, n_chunks).astype(BF16)

    z = jnp.dot(ub_ref[0], bs_ref[:MXU_DIM, :], preferred_element_type=F32)
    for tt in range(1, CHUNK // 2):
        z = z + jnp.dot(ub_ref[tt], bs_ref[tt * MXU_DIM:(tt + 1) * MXU_DIM, :], preferred_element_type=F32)
    buf_a[SCAN_PAD:, :] = z
    um = jnp.concatenate([jnp.broadcast_to(um_ref[t:t + 1, :], (SUBLANES, LANES)) for t in range(CHUNK)], axis=1)
    zm = jnp.dot(um.astype(BF16), bs_ref[...], preferred_element_type=F32)[0:1, :]
    buf_a[SCAN_PAD - 1:SCAN_PAD, :] = zm
    buf_b[SCAN_PAD - 1:SCAN_PAD, :] = zm

    def within_chunk(ss):
        acc = jnp.dot(ub_ref[0], tz_ref[ss], preferred_element_type=F32)
        for tt in range(1, ss + 1):
            acc = acc + jnp.dot(ub_ref[tt], tz_ref[ss - tt], preferred_element_type=F32)
        yt_ref[ss] = acc

    assert n_steps >= CHUNK // 2
    src, dst = buf_a, buf_b
    for step in range(n_steps):
        shift = 1 << step
        if step < CHUNK // 2:
            within_chunk(CHUNK // 2 - 1 - step)
        for pair in range(OCT // 2):
            re = slice(pair * 2 * LANES, pair * 2 * LANES + LANES)
            im = slice(pair * 2 * LANES + LANES, (pair + 1) * 2 * LANES)
            ar = ap_ref[0, step:step + 1, pair * LANES:(pair + 1) * LANES]
            ai = ap_ref[0, n_steps + step:n_steps + step + 1, pair * LANES:(pair + 1) * LANES]
            xr = src[SCAN_PAD:, re]
            xi = src[SCAN_PAD:, im]
            pr = src[SCAN_PAD - shift:SCAN_PAD - shift + n_chunks, re]
            pi = src[SCAN_PAD - shift:SCAN_PAD - shift + n_chunks, im]
            dst[SCAN_PAD:, re] = xr + ar * pr - ai * pi
            dst[SCAN_PAD:, im] = xi + ar * pi + ai * pr
        src, dst = dst, src

    start_state = src[SCAN_PAD - 1:SCAN_PAD - 1 + n_chunks, :].astype(BF16)
    for ss in range(CHUNK // 2):
        cols = slice(ss * MXU_DIM, (ss + 1) * MXU_DIM)
        acc = yt_ref[ss] + jnp.dot(start_state, cs_ref[:, cols], preferred_element_type=F32)
        for half in range(2):
            t = 2 * ss + half
            y = acc[:, half * LANES:(half + 1) * LANES] + d_ref[0] * slab(u_ref, t, n_chunks)
            y_ref[pl.ds(t, n_chunks, stride=CHUNK), :] = 0.5 * y * (1.0 + lax.erf(y * (1.0 / math.sqrt(2.0))))


def _ssm(u, u_meta, kc, bc, cc_re, cc_im, ap, dp, bsz, seq):
    n_chunks = seq // CHUNK
    assert n_chunks & (n_chunks - 1) == 0 and n_chunks // 2 <= SCAN_PAD and N_META == CHUNK
    n_steps = n_chunks.bit_length() - 1
    octet = lambda o, b: (o, 0, 0)
    scan_buf = pltpu.VMEM((SCAN_PAD + n_chunks, OCT_STATE), F32)
    return pl.pallas_call(
        functools.partial(_ssm_kernel, n_chunks=n_chunks),
        grid=(N_OCT, bsz),
        in_specs=[
            pl.BlockSpec((seq, LANES), lambda o, b: (b, o)),
            pl.BlockSpec((N_META, LANES), lambda o, b: (0, o)),
            pl.BlockSpec((CHUNK, 1, LANES, LANES), lambda o, b: (0, o, 0, 0)),
            pl.BlockSpec((CHUNK, 1, LANES, 4 * SSM_STATE), lambda o, b: (0, o, 0, 0)),
            pl.BlockSpec((CHUNK + 1, SSM_STATE, LANES), lambda o, b: (0, 0, o)),
            pl.BlockSpec((CHUNK + 1, SSM_STATE, LANES), lambda o, b: (0, 0, o)),
            pl.BlockSpec((1, 2 * n_steps, OCT_STATE // 2), octet),
            pl.BlockSpec((1, 1, LANES), octet),
        ],
        out_specs=pl.BlockSpec((seq, LANES), lambda o, b: (b, o)),
        out_shape=jax.ShapeDtypeStruct((bsz * seq, SSM_WIDTH), F32),
        scratch_shapes=[
            pltpu.VMEM((CHUNK // 2, MXU_DIM, MXU_DIM), BF16),
            pltpu.VMEM((OCT_IN, OCT_STATE), BF16),
            pltpu.VMEM((OCT_STATE, OCT_IN), BF16),
            pltpu.VMEM((CHUNK // 2, n_chunks, MXU_DIM), BF16), scan_buf, scan_buf,
            pltpu.VMEM((CHUNK // 2, n_chunks, MXU_DIM), F32)],
        compiler_params=pltpu.CompilerParams(
            dimension_semantics=("arbitrary", "arbitrary"), vmem_limit_bytes=VMEM_LIMIT),
        name="s5_chunked",
    )(u, u_meta, kc, bc, cc_re, cc_im, ap, dp)


def _ssm_weights(a_re, a_im, log_dt, b_re, b_im, c_re, c_im, d_skip, n_steps):
    dt = jnp.exp(log_dt)[:, None]
    lam_re, lam_im = a_re * dt, a_im * dt

    def power(n):
        n = n[:, None, None]
        mag = jnp.exp(n * lam_re)
        return mag * jnp.cos(n * lam_im), mag * jnp.sin(n * lam_im)

    ab_re, ab_im = jnp.exp(lam_re) * jnp.cos(lam_im), jnp.exp(lam_re) * jnp.sin(lam_im)
    den = a_re * a_re + a_im * a_im
    nr, ni = ab_re - 1.0, ab_im
    f_re = (nr * a_re + ni * a_im) / den
    f_im = (ni * a_re - nr * a_im) / den
    bb_re = f_re[..., None] * b_re - f_im[..., None] * b_im
    bb_im = f_re[..., None] * b_im + f_im[..., None] * b_re

    p_re, p_im = power(jnp.arange(CHUNK + 1, dtype=F32))

    ct_re, ct_im = c_re.transpose(2, 0, 1)[None], c_im.transpose(2, 0, 1)[None]
    pt_re, pt_im = p_re.transpose(0, 2, 1)[..., None], p_im.transpose(0, 2, 1)[..., None]
    ca_re = ct_re * pt_re - ct_im * pt_im
    ca_im = ct_re * pt_im + ct_im * pt_re
    cc_re = ca_re.reshape(CHUNK + 1, SSM_STATE, SSM_WIDTH)
    cc_im = ca_im.reshape(CHUNK + 1, SSM_STATE, SSM_WIDTH)

    bp_re = bb_re.transpose(1, 0, 2)[None, :, :, :, None]
    bp_im = bb_im.transpose(1, 0, 2)[None, :, :, :, None]
    kern = jnp.sum(ca_re[:CHUNK, :, :, None, :] * bp_re - ca_im[:CHUNK, :, :, None, :] * bp_im, axis=1)
    repeat = jnp.tile(jnp.eye(SSM_GROUP, dtype=F32), (1, OCT))
    kc = jnp.dot(kern.reshape(-1, SSM_GROUP), repeat, precision=lax.Precision.HIGHEST)
    kc = kc.reshape(CHUNK, N_OCT, LANES, LANES)

    r_re, r_im = p_re[CHUNK - 1::-1][:, :, None, :], p_im[CHUNK - 1::-1][:, :, None, :]
    bt_re, bt_im = bb_re.transpose(0, 2, 1)[None], bb_im.transpose(0, 2, 1)[None]
    bs_re = r_re * bt_re - r_im * bt_im
    bs_im = r_re * bt_im + r_im * bt_re
    bc = jnp.concatenate([bs_re, bs_re, bs_im, bs_im], axis=-1).reshape(CHUNK, N_OCT, LANES, 4 * SSM_STATE)

    s_re, s_im = power(CHUNK * (2.0 ** jnp.arange(n_steps, dtype=F32)))
    pack_ap = lambda m: m.reshape(n_steps, N_OCT, OCT_STATE // 2).transpose(1, 0, 2)
    ap = jnp.concatenate([pack_ap(s_re), pack_ap(s_im)], axis=1)

    dp = d_skip.reshape(N_OCT, 1, LANES)
    return kc, bc, cc_re, cc_im, ap, dp


def _out_ffn_kernel(x_ref, o_ref, y_ref, wglu_ref, bglu_ref, sg_ref, wout_ref, pmg_ref, pfg_ref,
                    wg_ref, wu_ref, wd_ref, pog_ref, out_ref):
    y = y_ref[...]
    gate = jnp.dot(y.astype(BF16), wglu_ref[...], preferred_element_type=F32) + bglu_ref[...]
    y = _rms(y * jax.nn.sigmoid(gate), sg_ref[...]).astype(BF16)
    mix = (jnp.dot(o_ref[...], wout_ref[:ATTN_WIDTH, :], preferred_element_type=F32)
           + jnp.dot(y, wout_ref[ATTN_WIDTH:, :], preferred_element_type=F32))
    h1 = x_ref[...] + _rms(mix, pmg_ref[...])
    h2 = _rms(h1, pfg_ref[...]).astype(BF16)
    g = jnp.dot(h2, wg_ref[...], preferred_element_type=F32)
    up = jnp.dot(h2, wu_ref[...], preferred_element_type=F32)
    f = (g * jax.nn.sigmoid(g) * up).astype(BF16)
    f = jnp.dot(f, wd_ref[...], preferred_element_type=F32)
    out_ref[...] = h1 + _rms(f, pog_ref[...])


def _out_ffn(x2d, o, y, wglu, bglu, sg, wout, pmg, pfg, wg, wu, wd, pog, tm):
    n = x2d.shape[0]
    row = lambda i: (i, 0)
    const = lambda i: (0, 0)

    def resident(shape):
        return pl.BlockSpec(shape, const, pipeline_mode=pl.Buffered(1))

    return pl.pallas_call(
        _out_ffn_kernel,
        grid=(n // tm,),
        in_specs=[
            pl.BlockSpec((tm, D_MODEL), row),
            pl.BlockSpec((tm, ATTN_WIDTH), row),
            pl.BlockSpec((tm, SSM_WIDTH), row),
            resident((SSM_WIDTH, SSM_WIDTH)),
            resident((1, SSM_WIDTH)),
            resident((1, SSM_WIDTH)),
            resident((D_MODEL, D_MODEL)),
            resident((1, D_MODEL)),
            resident((1, D_MODEL)),
            resident((D_MODEL, D_FF)),
            resident((D_MODEL, D_FF)),
            resident((D_FF, D_MODEL)),
            resident((1, D_MODEL)),
        ],
        out_specs=pl.BlockSpec((tm, D_MODEL), row),
        out_shape=jax.ShapeDtypeStruct((n, D_MODEL), F32),
        compiler_params=pltpu.CompilerParams(
            dimension_semantics=("arbitrary",), vmem_limit_bytes=VMEM_LIMIT),
        name="out_ffn",
    )(x2d, o, y, wglu, bglu, sg, wout, pmg, pfg, wg, wu, wd, pog)


def _rope_angles(pos):
    d = jnp.arange(LANES) % QK_DIM
    inv_freq = ROPE_THETA ** (-(2 * (d % (ROT_DIM // 2))).astype(F32) / ROT_DIM)
    ang = pos[:, None] * jnp.where(d < ROT_DIM, inv_freq, 0.0)[None, :]
    return jnp.cos(ang), jnp.sin(ang)


def _scale_w_in(w):
    scale = math.log2(math.e) / math.sqrt(QK_DIM)
    col_scale = jnp.where(jnp.arange(w.shape[1]) < ATTN_WIDTH, scale, 1.0).astype(F32)
    return (w * col_scale[None, :]).astype(BF16)


def kernel(x, meta, pre_mix_g, w_in, lambda_q1, lambda_k1, lambda_q2, lambda_k2, subln_g, a_re, a_im, log_dt,
           b_re, b_im, c_re, c_im, d_skip, w_glu, b_glu, ssm_out_g, w_out, post_mix_g, pre_ffn_g, w_gate,
           w_up, w_down, post_ffn_g):
    bsz, seq, _ = x.shape
    n = bsz * seq
    x2d = x.reshape(n, D_MODEL)
    row = lambda t: t[0].reshape(1, -1)

    w_in_p = _scale_w_in(w_in[0])
    g0 = row(pre_mix_g)
    q, k, v, u = _in_proj(x2d, g0, w_in_p, N_META, seq, ROW_TILE)
    _, k_m, v_m, u_m = _in_proj(meta, g0, w_in_p, 0, N_META, N_META)

    pad_meta = lambda t: jnp.pad(t, ((0, LANES - N_META), (0, 0)))
    o = _attention(q, k, v, pad_meta(k_m), pad_meta(v_m), lambda_q1, lambda_k1, lambda_q2, lambda_k2,
                   row(subln_g), bsz, seq)

    n_steps = (seq // CHUNK).bit_length() - 1
    ssm_w = _ssm_weights(a_re[0], a_im[0], log_dt[0], b_re[0], b_im[0], c_re[0], c_im[0], d_skip[0], n_steps)
    y = _ssm(u, u_m, *ssm_w, bsz, seq)

    out = _out_ffn(x2d, o, y, w_glu[0].astype(BF16), row(b_glu), row(ssm_out_g), w_out[0].astype(BF16),
                   row(post_mix_g), row(pre_ffn_g), w_gate[0].astype(BF16), w_up[0].astype(BF16),
                   w_down[0].astype(BF16), row(post_ffn_g), ROW_TILE)
    return out.reshape(bsz, seq, D_MODEL)
```

```python
import functools
import math

import jax
import jax.numpy as jnp
from jax import lax
from jax.experimental import pallas as pl
from jax.experimental.pallas import tpu as pltpu

D_MODEL = 1024
N_META = 16
N_HEADS = 4
QK_DIM = 64
V_DIM = 128
ROT_DIM = 16
ROPE_THETA = 500000.0
SSM_GROUP = 16
N_GROUPS = 32
SSM_STATE = 64
SSM_WIDTH = 512
ATTN_WIDTH = 512
D_FF = 2816
EPS = 1e-6
LAM_INIT = 0.8 - 0.6 * math.exp(-0.3 * 0)

LANES = 128
SUBLANES = 8
MXU_DIM = 256
CHUNK = 16
OCT = LANES // SSM_GROUP
N_OCT = N_GROUPS // OCT
OCT_IN = CHUNK * LANES
OCT_STATE = OCT * 2 * SSM_STATE
SCAN_PAD = SUBLANES
LOCAL_STEPS = 3

ROW_TILE = 512
Q_TILE = 512
KV_TILE = 512
ATTN_UNROLL = 4
VMEM_LIMIT = 56 * 1024 * 1024

F32 = jnp.float32
BF16 = jnp.bfloat16


def _rms(x, g):
    return x * lax.rsqrt(jnp.mean(x * x, axis=-1, keepdims=True) + EPS) * g


def _in_proj_kernel(x_ref, g_ref, w_ref, cr_ref, sr_ref, cb_ref, sb_ref, q_ref, k_ref, v_ref, u_ref):
    h = _rms(x_ref[...], g_ref[...]).astype(BF16)
    proj = jnp.dot(h, w_ref[...], preferred_element_type=F32)
    cos = cb_ref[0] * cr_ref[...] - sb_ref[0] * sr_ref[...]
    sin = sb_ref[0] * cr_ref[...] + cb_ref[0] * sr_ref[...]
    d = lax.broadcasted_iota(jnp.int32, (1, LANES), 1) % QK_DIM
    sa = sin * jnp.where(d < ROT_DIM // 2, -1.0, 0.0)
    sb = sin * jnp.where((d >= ROT_DIM // 2) & (d < ROT_DIM), 1.0, 0.0)

    def rope(t):
        return (t * cos + pltpu.roll(t, LANES - ROT_DIM // 2, axis=1) * sa
                + pltpu.roll(t, ROT_DIM // 2, axis=1) * sb)

    for c in range(ATTN_WIDTH // LANES):
        sl = slice(c * LANES, (c + 1) * LANES)
        q_ref[:, sl] = rope(proj[:, c * LANES:(c + 1) * LANES]).astype(BF16)
        k_ref[:, sl] = rope(proj[:, ATTN_WIDTH + c * LANES:ATTN_WIDTH + (c + 1) * LANES]).astype(BF16)
    v_ref[...] = proj[:, 2 * ATTN_WIDTH:3 * ATTN_WIDTH].astype(BF16)
    u_ref[...] = proj[:, 3 * ATTN_WIDTH:]


def _in_proj(x2d, g, w, first_pos, seq, tm):
    n = x2d.shape[0]
    tiles_per_seq = seq // tm
    cos_r, sin_r = _rope_angles(jnp.arange(tm, dtype=F32))
    cos_b, sin_b = _rope_angles(first_pos + tm * jnp.arange(tiles_per_seq, dtype=F32))
    row = lambda i: (i, 0)
    const = lambda i: (0, 0)
    base = pl.BlockSpec((1, 1, LANES), lambda i: (i % tiles_per_seq, 0, 0))
    return pl.pallas_call(
        _in_proj_kernel,
        grid=(n // tm,),
        in_specs=[
            pl.BlockSpec((tm, D_MODEL), row),
            pl.BlockSpec((1, D_MODEL), const),
            pl.BlockSpec((D_MODEL, 4 * ATTN_WIDTH), const),
            pl.BlockSpec((tm, LANES), const),
            pl.BlockSpec((tm, LANES), const),
            base, base,
        ],
        out_specs=[pl.BlockSpec((tm, ATTN_WIDTH), row)] * 4,
        out_shape=[jax.ShapeDtypeStruct((n, ATTN_WIDTH), BF16)] * 3
        + [jax.ShapeDtypeStruct((n, SSM_WIDTH), F32)],
        compiler_params=pltpu.CompilerParams(
            dimension_semantics=("arbitrary",), vmem_limit_bytes=VMEM_LIMIT),
        name="in_proj",
    )(x2d, g, w, cos_r, sin_r, cos_b[:, None, :], sin_b[:, None, :])


def _attn_kernel(*refs):
    n_tiles = refs[0].shape[0] // Q_TILE

    def q_tile(qi, carry):
        _attn_q_tile(qi, *refs)
        return carry

    lax.fori_loop(0, n_tiles, q_tile, 0)


def _attn_q_tile(qi, q1_ref, q2_ref, k1_ref, k2_ref, v_ref, km1_ref, km2_ref, vm_ref,
                 lq1_ref, lk1_ref, lq2_ref, lk2_ref, sg_ref,
                 o_ref, m_ref, l_ref, acc_ref, pa_ref, pb_ref):
    q_rows = pl.ds(pl.multiple_of(qi * Q_TILE, Q_TILE), Q_TILE)
    own = lax.broadcasted_iota(jnp.int32, (Q_TILE, LANES), 1) // QK_DIM == pl.program_id(1) % 2
    qs = tuple(jnp.where(own, r[q_rows, :], jnp.zeros((Q_TILE, LANES), BF16)) for r in (q1_ref, q2_ref))
    k_refs, km_refs = (k1_ref, k2_ref), (km1_ref, km2_ref)
    nt = (((1,), (1,)), ((), ()))
    wide = (Q_TILE, LANES)

    def block(k, mask, p_out, prev):
        n_col = k[0].shape[0] // LANES
        for i in range(2):
            s = lax.dot_general(qs[i], k[i], nt, preferred_element_type=F32)
            if mask is not None:
                s = jnp.where(mask, s, -jnp.inf)
            cols = [s[:, c * LANES:(c + 1) * LANES] for c in range(n_col)]
            m_tile = functools.reduce(jnp.maximum, cols)
            m_new = jnp.broadcast_to(jnp.max(m_tile, axis=1, keepdims=True), wide)
            if prev is not None:
                m_old = m_ref[i]
                m_new = jnp.maximum(m_old, m_new)
                alpha = jnp.exp2(m_old - m_new)
            m_ref[i] = m_new
            ps = [jnp.exp2(c - m_new) for c in cols]
            l_tile = functools.reduce(jnp.add, ps)
            l_ref[i] = l_tile if prev is None else alpha * l_ref[i] + l_tile
            for c in range(n_col):
                p_out[i, :, c * LANES:(c + 1) * LANES] = ps[c].astype(BF16)
            if prev is not None:
                p_prev, v_prev = prev
                pv = jnp.dot(p_prev[i], v_prev, preferred_element_type=F32)
                acc_ref[i] = (acc_ref[i] + pv) * alpha

    def tile(ref, t):
        return ref[pl.ds(pl.multiple_of(t * KV_TILE, KV_TILE), KV_TILE), :]

    def ktile(t):
        return tuple(tile(r, t) for r in k_refs)

    def finish(pending, p_meta):
        meta_mask = lax.broadcasted_iota(jnp.int32, wide, 1) < N_META
        block(tuple(r[...] for r in km_refs), meta_mask, p_meta, pending)
        for i in range(2):
            acc_ref[i] = acc_ref[i] + jnp.dot(p_meta[i, :, :LANES], vm_ref[...], preferred_element_type=F32)

    acc_ref[...] = jnp.zeros(acc_ref.shape, F32)
    causal = (lax.broadcasted_iota(jnp.int32, (Q_TILE, KV_TILE), 1)
              <= lax.broadcasted_iota(jnp.int32, (Q_TILE, KV_TILE), 0))
    block(ktile(qi), causal, pa_ref, None)

    def run(t0, n):
        slots = (pa_ref, pb_ref)
        before = jnp.where(t0 == 0, qi, t0 - 1)
        for d in range(n):
            block(ktile(t0 + d), None, slots[(d + 1) % 2], (slots[d % 2], tile(v_ref, before)))
            before = t0 + d
        return before

    def unrolled(jj, carry):
        run(ATTN_UNROLL * jj, ATTN_UNROLL)
        return carry

    lax.fori_loop(0, qi // ATTN_UNROLL, unrolled, 0)
    done = (qi // ATTN_UNROLL) * ATTN_UNROLL

    def tail(left):
        before = run(done, left)
        slots = (pa_ref, pb_ref)
        finish((slots[left % 2], tile(v_ref, before)), slots[(left + 1) % 2])

    for left in range(ATTN_UNROLL):
        pl.when(qi - done == left)(functools.partial(tail, left))

    lam = (jnp.exp(jnp.sum(lq1_ref[...] * lk1_ref[...], axis=1, keepdims=True))
           - jnp.exp(jnp.sum(lq2_ref[...] * lk2_ref[...], axis=1, keepdims=True)) + LAM_INIT)
    l1 = jnp.sum(l_ref[0], axis=1, keepdims=True)
    l2 = jnp.sum(l_ref[1], axis=1, keepdims=True)
    o = acc_ref[0] / l1 - lam * (acc_ref[1] / l2)
    o_ref[q_rows, :] = (_rms(o, sg_ref[...]) * (1.0 - LAM_INIT)).astype(BF16)


def _attention(q, k, v, km, vm, lq1, lk1, lq2, lk2, sg, bsz, seq):
    assert Q_TILE == KV_TILE and seq % Q_TILE == 0
    pairs = N_HEADS // 2
    kvmap = lambda b, h: (b, h)
    map1 = pl.BlockSpec((seq, LANES), lambda b, h: (b, h // 2))
    map2 = pl.BlockSpec((seq, LANES), lambda b, h: (b, pairs + h // 2))
    meta1 = pl.BlockSpec((LANES, LANES), lambda b, h: (0, h // 2))
    meta2 = pl.BlockSpec((LANES, LANES), lambda b, h: (0, pairs + h // 2))
    const = lambda b, h: (0, 0)
    vec = pl.BlockSpec((1, QK_DIM), const)
    return pl.pallas_call(
        _attn_kernel,
        grid=(bsz, N_HEADS),
        in_specs=[
            map1, map2, map1, map2,
            pl.BlockSpec((seq, LANES), kvmap),
            meta1, meta2,
            pl.BlockSpec((LANES, LANES), lambda b, h: (0, h)),
            vec, vec, vec, vec,
            pl.BlockSpec((1, V_DIM), const),
        ],
        out_specs=pl.BlockSpec((seq, LANES), kvmap),
        out_shape=jax.ShapeDtypeStruct((bsz * seq, ATTN_WIDTH), BF16),
        scratch_shapes=[
            pltpu.VMEM((2, Q_TILE, LANES), F32),
            pltpu.VMEM((2, Q_TILE, LANES), F32),
            pltpu.VMEM((2, Q_TILE, V_DIM), F32),
            pltpu.VMEM((2, Q_TILE, KV_TILE), BF16),
            pltpu.VMEM((2, Q_TILE, KV_TILE), BF16),
        ],
        compiler_params=pltpu.CompilerParams(
            dimension_semantics=("arbitrary", "arbitrary"), vmem_limit_bytes=VMEM_LIMIT),
        name="diff_attention",
    )(q, q, k, k, v, km, km, vm, lq1, lk1, lq2, lk2, sg)


def _ssm_kernel(u_ref, um_ref, kc_ref, bc_ref, ccr_ref, cci_ref, loc_ref, blk_ref, d_ref, y_ref,
                tz_ref, bs_ref, cs_ref, ub_ref, buf_a, buf_b, cb_a, cb_b, rep_ref, yt_ref, *, n_chunks):
    n_tiles = OCT_STATE // LANES
    n_blocks = n_chunks // SUBLANES
    blk_steps = n_blocks.bit_length() - 1
    blk_pad = cb_a.shape[1] - n_blocks

    @pl.when((pl.program_id(0) == 0) & (pl.program_id(1) == 0))
    def _():
        for buf, pad in ((buf_a, SCAN_PAD), (buf_b, SCAN_PAD), (cb_a, blk_pad), (cb_b, blk_pad)):
            buf[:, :pad, :] = jnp.zeros((n_tiles, pad, LANES), F32)

    @pl.when(pl.program_id(1) == 0)
    def _():
        def group_of(shape, axis):
            return (lax.broadcasted_iota(jnp.int32, shape, axis) // SSM_GROUP) % OCT

        same = group_of((LANES, LANES), 0) == group_of((LANES, LANES), 1)
        lag_blk = [jnp.where(same, kc_ref[n, 0], 0.0).astype(BF16) for n in range(CHUNK)]
        for m in range(CHUNK // 2):
            tz_ref[m, :LANES, :LANES] = lag_blk[2 * m]
            tz_ref[m, :LANES, LANES:] = lag_blk[2 * m + 1]
            tz_ref[m, LANES:, :LANES] = lag_blk[2 * m - 1] if m else jnp.zeros((LANES, LANES), BF16)
            tz_ref[m, LANES:, LANES:] = lag_blk[2 * m]
        bc = bc_ref[:, 0].reshape(OCT_IN, 4 * SSM_STATE)
        g_row = group_of(bc.shape, 0)
        w_col = (lax.broadcasted_iota(jnp.int32, bc.shape, 1) // SSM_STATE) % 2
        for qq in range(OCT // 2):
            bs_ref[:, qq * MXU_DIM:(qq + 1) * MXU_DIM] = jnp.where(g_row == 2 * qq + w_col, bc, 0.0).astype(BF16)
        g_col = group_of((SSM_STATE, LANES), 1)
        for s in range(CHUNK):
            c_s = (ccr_ref[s + 1], -cci_ref[s + 1])
            for qq in range(OCT // 2):
                for part in range(2):
                    for w in range(2):
                        r0 = qq * MXU_DIM + part * LANES + w * SSM_STATE
                        cs_ref[r0:r0 + SSM_STATE, s * LANES:(s + 1) * LANES] = jnp.where(
                            g_col == 2 * qq + w, c_s[part], 0.0).astype(BF16)

    def slab(ref, t, rows):
        return ref[pl.ds(t, rows, stride=CHUNK), :]

    for t in range(CHUNK):
        ub_ref[t // 2, :, (t % 2) * LANES:(t % 2 + 1) * LANES] = slab(u_ref, t, n_chunks).astype(BF16)

    z = jnp.dot(ub_ref[0], bs_ref[:MXU_DIM, :], preferred_element_type=F32)
    for tt in range(1, CHUNK // 2):
        z = z + jnp.dot(ub_ref[tt], bs_ref[tt * MXU_DIM:(tt + 1) * MXU_DIM, :], preferred_element_type=F32)
    um = jnp.concatenate([jnp.broadcast_to(um_ref[t:t + 1, :], (SUBLANES, LANES)) for t in range(CHUNK)], axis=1)
    zm = jnp.dot(um.astype(BF16), bs_ref[...], preferred_element_type=F32)[0:1, :]
    for ct in range(n_tiles):
        cols = slice(ct * LANES, (ct + 1) * LANES)
        buf_a[ct, SCAN_PAD:, :] = z[:, cols]
        for buf, pad in ((buf_a, SCAN_PAD), (buf_b, SCAN_PAD), (cb_a, blk_pad), (cb_b, blk_pad)):
            buf[ct, pad - 1:pad, :] = zm[:, cols]

    def within_chunk(ss):
        acc = jnp.dot(ub_ref[0], tz_ref[ss], preferred_element_type=F32)
        for tt in range(1, ss + 1):
            acc = acc + jnp.dot(ub_ref[tt], tz_ref[ss - tt], preferred_element_type=F32)
        yt_ref[ss] = acc

    assert LOCAL_STEPS + blk_steps >= CHUNK // 2 and (1 << LOCAL_STEPS) == SUBLANES
    todo = list(range(CHUNK // 2 - 1, -1, -1))
    blocked = (n_blocks, SUBLANES, LANES)

    def lanes_of(pair):
        return slice(pair * LANES, (pair + 1) * LANES)

    def combine(x, p, ar, ai):
        return x[0] + ar * p[0] - ai * p[1], x[1] + ar * p[1] + ai * p[0]

    src, dst = buf_a, buf_b
    for step in range(LOCAL_STEPS):
        shift = 1 << step
        within_chunk(todo.pop(0))
        for pair in range(OCT // 2):
            ar, ai = (loc_ref[0, step, part, :, lanes_of(pair)] for part in range(2))
            x = [src[2 * pair + part, SCAN_PAD:, :].reshape(blocked) for part in range(2)]
            p = [src[2 * pair + part, SCAN_PAD - shift:SCAN_PAD - shift + n_chunks, :].reshape(blocked)
                 for part in range(2)]
            for part, val in enumerate(combine(x, p, ar, ai)):
                dst[2 * pair + part, SCAN_PAD:, :] = val.reshape(n_chunks, LANES)
        src, dst = dst, src

    for ct in range(n_tiles):
        cb_a[ct, blk_pad:, :] = src[ct, pl.ds(SCAN_PAD + SUBLANES - 1, n_blocks, stride=SUBLANES), :]
    csrc, cdst = cb_a, cb_b
    for step in range(blk_steps):
        shift = 1 << step
        if todo:
            within_chunk(todo.pop(0))
        for pair in range(OCT // 2):
            ar = blk_ref[0, step:step + 1, lanes_of(pair)]
            ai = blk_ref[0, blk_steps + step:blk_steps + step + 1, lanes_of(pair)]
            x = [csrc[2 * pair + part, blk_pad:, :] for part in range(2)]
            p = [csrc[2 * pair + part, blk_pad - shift:blk_pad - shift + n_blocks, :] for part in range(2)]
            for part, val in enumerate(combine(x, p, ar, ai)):
                cdst[2 * pair + part, blk_pad:, :] = val
        csrc, cdst = cdst, csrc

    for pair in range(OCT // 2):
        for part in range(2):
            entering = csrc[2 * pair + part, blk_pad - 1:blk_pad - 1 + n_blocks, :]
            for row in range(SUBLANES):
                rep_ref[pair, part, pl.ds(row, n_blocks, stride=SUBLANES), :] = entering
        ar, ai = (loc_ref[0, LOCAL_STEPS, part, :, lanes_of(pair)] for part in range(2))
        x = [src[2 * pair + part, SCAN_PAD:, :].reshape(blocked) for part in range(2)]
        p = [rep_ref[pair, part].reshape(blocked) for part in range(2)]
        for part, val in enumerate(combine(x, p, ar, ai)):
            dst[2 * pair + part, SCAN_PAD:, :] = val.reshape(n_chunks, LANES)

    start_state = jnp.concatenate(
        [dst[ct, SCAN_PAD - 1:SCAN_PAD - 1 + n_chunks, :] for ct in range(n_tiles)], axis=1).astype(BF16)
    for ss in range(CHUNK // 2):
        cols = slice(ss * MXU_DIM, (ss + 1) * MXU_DIM)
        acc = yt_ref[ss] + jnp.dot(start_state, cs_ref[:, cols], preferred_element_type=F32)
        for half in range(2):
            t = 2 * ss + half
            y = acc[:, half * LANES:(half + 1) * LANES] + d_ref[0] * slab(u_ref, t, n_chunks)
            y_ref[pl.ds(t, n_chunks, stride=CHUNK), :] = 0.5 * y * (1.0 + lax.erf(y * (1.0 / math.sqrt(2.0))))


def _ssm(u, u_meta, kc, bc, cc_re, cc_im, loc, blk, dp, bsz, seq):
    n_chunks = seq // CHUNK
    n_blocks = n_chunks // SUBLANES
    assert n_blocks & (n_blocks - 1) == 0 and N_META == CHUNK
    octet = lambda o, b: (o, 0, 0)
    n_tiles = OCT_STATE // LANES
    scan_buf = pltpu.VMEM((n_tiles, SCAN_PAD + n_chunks, LANES), F32)
    block_buf = pltpu.VMEM((n_tiles, n_blocks // 2 + n_blocks, LANES), F32)
    return pl.pallas_call(
        functools.partial(_ssm_kernel, n_chunks=n_chunks),
        grid=(N_OCT, bsz),
        in_specs=[
            pl.BlockSpec((seq, LANES), lambda o, b: (b, o)),
            pl.BlockSpec((N_META, LANES), lambda o, b: (0, o)),
            pl.BlockSpec((CHUNK, 1, LANES, LANES), lambda o, b: (0, o, 0, 0)),
            pl.BlockSpec((CHUNK, 1, LANES, 4 * SSM_STATE), lambda o, b: (0, o, 0, 0)),
            pl.BlockSpec((CHUNK + 1, SSM_STATE, LANES), lambda o, b: (0, 0, o)),
            pl.BlockSpec((CHUNK + 1, SSM_STATE, LANES), lambda o, b: (0, 0, o)),
            pl.BlockSpec((1,) + loc.shape[1:], lambda o, b: (o, 0, 0, 0, 0)),
            pl.BlockSpec((1,) + blk.shape[1:], octet),
            pl.BlockSpec((1, 1, LANES), octet),
        ],
        out_specs=pl.BlockSpec((seq, LANES), lambda o, b: (b, o)),
        out_shape=jax.ShapeDtypeStruct((bsz * seq, SSM_WIDTH), F32),
        scratch_shapes=[
            pltpu.VMEM((CHUNK // 2, MXU_DIM, MXU_DIM), BF16),
            pltpu.VMEM((OCT_IN, OCT_STATE), BF16),
            pltpu.VMEM((OCT_STATE, OCT_IN), BF16),
            pltpu.VMEM((CHUNK // 2, n_chunks, MXU_DIM), BF16), scan_buf, scan_buf, block_buf, block_buf,
            pltpu.VMEM((OCT // 2, 2, n_chunks, LANES), F32),
            pltpu.VMEM((CHUNK // 2, n_chunks, MXU_DIM), F32)],
        compiler_params=pltpu.CompilerParams(
            dimension_semantics=("arbitrary", "arbitrary"), vmem_limit_bytes=VMEM_LIMIT),
        name="s5_chunked",
    )(u, u_meta, kc, bc, cc_re, cc_im, loc, blk, dp)


def _ssm_weights(a_re, a_im, log_dt, b_re, b_im, c_re, c_im, d_skip, n_chunks):
    dt = jnp.exp(log_dt)[:, None]
    lam_re, lam_im = a_re * dt, a_im * dt

    def power(n, transposed=False):
        n = n[:, None, None]
        l_re, l_im = (lam_re.T, lam_im.T) if transposed else (lam_re, lam_im)
        mag = jnp.exp(n * l_re)
        return mag * jnp.cos(n * l_im), mag * jnp.sin(n * l_im)

    ab_re, ab_im = jnp.exp(lam_re) * jnp.cos(lam_im), jnp.exp(lam_re) * jnp.sin(lam_im)
    den = a_re * a_re + a_im * a_im
    nr, ni = ab_re - 1.0, ab_im
    f_re = (nr * a_re + ni * a_im) / den
    f_im = (ni * a_re - nr * a_im) / den
    bb_re = f_re[..., None] * b_re - f_im[..., None] * b_im
    bb_im = f_re[..., None] * b_im + f_im[..., None] * b_re

    p_re, p_im = power(jnp.arange(CHUNK + 1, dtype=F32))

    ct_re, ct_im = c_re.transpose(2, 0, 1)[None], c_im.transpose(2, 0, 1)[None]
    pt_re, pt_im = (t[..., None] for t in power(jnp.arange(CHUNK + 1, dtype=F32), transposed=True))
    ca_re = ct_re * pt_re - ct_im * pt_im
    ca_im = ct_re * pt_im + ct_im * pt_re
    cc_re = ca_re.reshape(CHUNK + 1, SSM_STATE, SSM_WIDTH)
    cc_im = ca_im.reshape(CHUNK + 1, SSM_STATE, SSM_WIDTH)

    bp_re = bb_re.transpose(1, 0, 2)[None, :, :, :, None]
    bp_im = bb_im.transpose(1, 0, 2)[None, :, :, :, None]
    kern = jnp.sum(ca_re[:CHUNK, :, :, None, :] * bp_re - ca_im[:CHUNK, :, :, None, :] * bp_im, axis=1)
    repeat = jnp.tile(jnp.eye(SSM_GROUP, dtype=F32), (1, OCT))
    kc = jnp.dot(kern.reshape(-1, SSM_GROUP), repeat, precision=lax.Precision.HIGHEST)
    kc = kc.reshape(CHUNK, N_OCT, LANES, LANES)

    r_re, r_im = p_re[CHUNK - 1::-1][:, :, None, :], p_im[CHUNK - 1::-1][:, :, None, :]
    bt_re, bt_im = bb_re.transpose(0, 2, 1)[None], bb_im.transpose(0, 2, 1)[None]
    bs_re = r_re * bt_re - r_im * bt_im
    bs_im = r_re * bt_im + r_im * bt_re
    bc = jnp.concatenate([bs_re, bs_re, bs_im, bs_im], axis=-1).reshape(CHUNK, N_OCT, LANES, 4 * SSM_STATE)

    rows = jnp.arange(SUBLANES)
    strides = 2 ** jnp.arange(LOCAL_STEPS)
    in_block = (rows[None, :] >= strides[:, None])[:, :, None, None]
    blk_steps = (n_chunks // SUBLANES).bit_length() - 1

    def pack(m):
        m = m.reshape(m.shape[:-2] + (N_OCT, OCT_STATE // 2))
        return jnp.moveaxis(m, -2, 0)

    loc, blk = [], []
    for carry, block_stride in zip(power(CHUNK * (rows + 1.0)), power(CHUNK * SUBLANES * 2.0 ** jnp.arange(blk_steps))):
        steps = jnp.where(in_block, carry[strides - 1][:, None], 0.0)
        loc.append(pack(jnp.concatenate([steps, carry[None]], axis=0)))
        blk.append(pack(block_stride))
    loc = jnp.stack(loc, axis=2)
    blk = jnp.concatenate(blk, axis=1)

    dp = d_skip.reshape(N_OCT, 1, LANES)
    return kc, bc, cc_re, cc_im, loc, blk, dp


def _out_ffn_kernel(x_ref, o_ref, y_ref, wglu_ref, bglu_ref, sg_ref, wout_ref, pmg_ref, pfg_ref,
                    wg_ref, wu_ref, wd_ref, pog_ref, out_ref):
    y = y_ref[...]
    gate = jnp.dot(y.astype(BF16), wglu_ref[...], preferred_element_type=F32) + bglu_ref[...]
    y = _rms(y * jax.nn.sigmoid(gate), sg_ref[...]).astype(BF16)
    mix = (jnp.dot(o_ref[...], wout_ref[:ATTN_WIDTH, :], preferred_element_type=F32)
           + jnp.dot(y, wout_ref[ATTN_WIDTH:, :], preferred_element_type=F32))
    h1 = x_ref[...] + _rms(mix, pmg_ref[...])
    h2 = _rms(h1, pfg_ref[...]).astype(BF16)
    g = jnp.dot(h2, wg_ref[...], preferred_element_type=F32)
    up = jnp.dot(h2, wu_ref[...], preferred_element_type=F32)
    f = (g * jax.nn.sigmoid(g) * up).astype(BF16)
    f = jnp.dot(f, wd_ref[...], preferred_element_type=F32)
    out_ref[...] = h1 + _rms(f, pog_ref[...])


def _out_ffn(x2d, o, y, wglu, bglu, sg, wout, pmg, pfg, wg, wu, wd, pog, tm):
    n = x2d.shape[0]
    row = lambda i: (i, 0)
    const = lambda i: (0, 0)

    def resident(shape):
        return pl.BlockSpec(shape, const, pipeline_mode=pl.Buffered(1))

    return pl.pallas_call(
        _out_ffn_kernel,
        grid=(n // tm,),
        in_specs=[
            pl.BlockSpec((tm, D_MODEL), row),
            pl.BlockSpec((tm, ATTN_WIDTH), row),
            pl.BlockSpec((tm, SSM_WIDTH), row),
            resident((SSM_WIDTH, SSM_WIDTH)),
            resident((1, SSM_WIDTH)),
            resident((1, SSM_WIDTH)),
            resident((D_MODEL, D_MODEL)),
            resident((1, D_MODEL)),
            resident((1, D_MODEL)),
            resident((D_MODEL, D_FF)),
            resident((D_MODEL, D_FF)),
            resident((D_FF, D_MODEL)),
            resident((1, D_MODEL)),
        ],
        out_specs=pl.BlockSpec((tm, D_MODEL), row),
        out_shape=jax.ShapeDtypeStruct((n, D_MODEL), F32),
        compiler_params=pltpu.CompilerParams(
            dimension_semantics=("arbitrary",), vmem_limit_bytes=VMEM_LIMIT),
        name="out_ffn",
    )(x2d, o, y, wglu, bglu, sg, wout, pmg, pfg, wg, wu, wd, pog)


def _rope_angles(pos):
    d = jnp.arange(LANES) % QK_DIM
    inv_freq = ROPE_THETA ** (-(2 * (d % (ROT_DIM // 2))).astype(F32) / ROT_DIM)
    ang = pos[:, None] * jnp.where(d < ROT_DIM, inv_freq, 0.0)[None, :]
    return jnp.cos(ang), jnp.sin(ang)


def _scale_w_in(w):
    scale = math.log2(math.e) / math.sqrt(QK_DIM)
    col_scale = jnp.where(jnp.arange(w.shape[1]) < ATTN_WIDTH, scale, 1.0).astype(F32)
    return (w * col_scale[None, :]).astype(BF16)


def kernel(x, meta, pre_mix_g, w_in, lambda_q1, lambda_k1, lambda_q2, lambda_k2, subln_g, a_re, a_im, log_dt,
           b_re, b_im, c_re, c_im, d_skip, w_glu, b_glu, ssm_out_g, w_out, post_mix_g, pre_ffn_g, w_gate,
           w_up, w_down, post_ffn_g):
    bsz, seq, _ = x.shape
    n = bsz * seq
    x2d = x.reshape(n, D_MODEL)
    row = lambda t: t[0].reshape(1, -1)

    w_in_p = _scale_w_in(w_in[0])
    g0 = row(pre_mix_g)
    q, k, v, u = _in_proj(x2d, g0, w_in_p, N_META, seq, ROW_TILE)
    _, k_m, v_m, u_m = _in_proj(meta, g0, w_in_p, 0, N_META, N_META)

    pad_meta = lambda t: jnp.pad(t, ((0, LANES - N_META), (0, 0)))
    o = _attention(q, k, v, pad_meta(k_m), pad_meta(v_m), lambda_q1, lambda_k1, lambda_q2, lambda_k2,
                   row(subln_g), bsz, seq)

    ssm_w = _ssm_weights(a_re[0], a_im[0], log_dt[0], b_re[0], b_im[0], c_re[0], c_im[0], d_skip[0],
                         seq // CHUNK)
    y = _ssm(u, u_m, *ssm_w, bsz, seq)

    out = _out_ffn(x2d, o, y, w_glu[0].astype(BF16), row(b_glu), row(ssm_out_g), w_out[0].astype(BF16),
                   row(post_mix_g), row(pre_ffn_g), w_gate[0].astype(BF16), w_up[0].astype(BF16),
                   w_down[0].astype(BF16), row(post_ffn_g), ROW_TILE)
    return out.reshape(bsz, seq, D_MODEL)
```

```python
import functools
import math

import jax
import jax.numpy as jnp
from jax import lax
from jax.experimental import pallas as pl
from jax.experimental.pallas import tpu as pltpu

D_MODEL = 1024
N_META = 16
N_HEADS = 4
QK_DIM = 64
V_DIM = 128
ROT_DIM = 16
ROPE_THETA = 500000.0
SSM_GROUP = 16
N_GROUPS = 32
SSM_STATE = 64
SSM_WIDTH = 512
ATTN_WIDTH = 512
D_FF = 2816
EPS = 1e-6
LAM_INIT = 0.8 - 0.6 * math.exp(-0.3 * 0)

LANES = 128
SUBLANES = 8
MXU_DIM = 256
CHUNK = 16
OCT = LANES // SSM_GROUP
N_OCT = N_GROUPS // OCT
OCT_IN = CHUNK * LANES
OCT_STATE = OCT * 2 * SSM_STATE
SCAN_PAD = SUBLANES
LOCAL_STEPS = 3

ROW_TILE = 512
Q_TILE = 512
KV_TILE = 512
ATTN_UNROLL = 4
VMEM_LIMIT = 56 * 1024 * 1024

F32 = jnp.float32
BF16 = jnp.bfloat16


def _rms(x, g):
    return x * lax.rsqrt(jnp.mean(x * x, axis=-1, keepdims=True) + EPS) * g


def _in_proj_kernel(x_ref, g_ref, w_ref, cr_ref, sr_ref, cb_ref, sb_ref, q_ref, k_ref, v_ref, u_ref):
    h = _rms(x_ref[...], g_ref[...]).astype(BF16)
    proj = jnp.dot(h, w_ref[...], preferred_element_type=F32)
    cos = cb_ref[0] * cr_ref[...] - sb_ref[0] * sr_ref[...]
    sin = sb_ref[0] * cr_ref[...] + cb_ref[0] * sr_ref[...]
    d = lax.broadcasted_iota(jnp.int32, (1, LANES), 1) % QK_DIM
    sa = sin * jnp.where(d < ROT_DIM // 2, -1.0, 0.0)
    sb = sin * jnp.where((d >= ROT_DIM // 2) & (d < ROT_DIM), 1.0, 0.0)

    def rope(t):
        return (t * cos + pltpu.roll(t, LANES - ROT_DIM // 2, axis=1) * sa
                + pltpu.roll(t, ROT_DIM // 2, axis=1) * sb)

    for c in range(ATTN_WIDTH // LANES):
        sl = slice(c * LANES, (c + 1) * LANES)
        q_ref[:, sl] = rope(proj[:, c * LANES:(c + 1) * LANES]).astype(BF16)
        k_ref[:, sl] = rope(proj[:, ATTN_WIDTH + c * LANES:ATTN_WIDTH + (c + 1) * LANES]).astype(BF16)
    v_ref[...] = proj[:, 2 * ATTN_WIDTH:3 * ATTN_WIDTH].astype(BF16)
    u_ref[...] = proj[:, 3 * ATTN_WIDTH:]


def _in_proj(x2d, g, w, first_pos, seq, tm):
    n = x2d.shape[0]
    tiles_per_seq = seq // tm
    cos_r, sin_r = _rope_angles(jnp.arange(tm, dtype=F32))
    cos_b, sin_b = _rope_angles(first_pos + tm * jnp.arange(tiles_per_seq, dtype=F32))
    row = lambda i: (i, 0)
    const = lambda i: (0, 0)
    base = pl.BlockSpec((1, 1, LANES), lambda i: (i % tiles_per_seq, 0, 0))
    return pl.pallas_call(
        _in_proj_kernel,
        grid=(n // tm,),
        in_specs=[
            pl.BlockSpec((tm, D_MODEL), row),
            pl.BlockSpec((1, D_MODEL), const),
            pl.BlockSpec((D_MODEL, 4 * ATTN_WIDTH), const),
            pl.BlockSpec((tm, LANES), const),
            pl.BlockSpec((tm, LANES), const),
            base, base,
        ],
        out_specs=[pl.BlockSpec((tm, ATTN_WIDTH), row)] * 4,
        out_shape=[jax.ShapeDtypeStruct((n, ATTN_WIDTH), BF16)] * 3
        + [jax.ShapeDtypeStruct((n, SSM_WIDTH), F32)],
        compiler_params=pltpu.CompilerParams(
            dimension_semantics=("arbitrary",), vmem_limit_bytes=VMEM_LIMIT),
        name="in_proj",
    )(x2d, g, w, cos_r, sin_r, cos_b[:, None, :], sin_b[:, None, :])


def _attn_kernel(*refs):
    n_tiles = refs[0].shape[0] // Q_TILE

    def q_tile(qi, carry):
        _attn_q_tile(qi, *refs)
        return carry

    lax.fori_loop(0, n_tiles, q_tile, 0)


def _attn_q_tile(qi, q1_ref, q2_ref, k1_ref, k2_ref, v_ref, km1_ref, km2_ref, vm_ref,
                 lq1_ref, lk1_ref, lq2_ref, lk2_ref, sg_ref,
                 o_ref, m_ref, l_ref, acc_ref, pa_ref, pb_ref):
    q_rows = pl.ds(pl.multiple_of(qi * Q_TILE, Q_TILE), Q_TILE)
    own = lax.broadcasted_iota(jnp.int32, (Q_TILE, LANES), 1) // QK_DIM == pl.program_id(1) % 2
    qs = tuple(jnp.where(own, r[q_rows, :], jnp.zeros((Q_TILE, LANES), BF16)) for r in (q1_ref, q2_ref))
    k_refs, km_refs = (k1_ref, k2_ref), (km1_ref, km2_ref)
    nt = (((1,), (1,)), ((), ()))
    wide = (Q_TILE, LANES)

    def block(k, mask, p_out, prev):
        n_col = k[0].shape[0] // LANES
        for i in range(2):
            s = lax.dot_general(qs[i], k[i], nt, preferred_element_type=F32)
            if mask is not None:
                s = jnp.where(mask, s, -jnp.inf)
            cols = [s[:, c * LANES:(c + 1) * LANES] for c in range(n_col)]
            m_tile = functools.reduce(jnp.maximum, cols)
            m_new = jnp.broadcast_to(jnp.max(m_tile, axis=1, keepdims=True), wide)
            if prev is not None:
                m_old = m_ref[i]
                m_new = jnp.maximum(m_old, m_new)
                alpha = jnp.exp2(m_old - m_new)
            m_ref[i] = m_new
            ps = [jnp.exp2(c - m_new) for c in cols]
            l_tile = functools.reduce(jnp.add, ps)
            l_ref[i] = l_tile if prev is None else alpha * l_ref[i] + l_tile
            for c in range(n_col):
                p_out[i, :, c * LANES:(c + 1) * LANES] = ps[c].astype(BF16)
            if prev is not None:
                p_prev, v_prev = prev
                pv = jnp.dot(p_prev[i], v_prev, preferred_element_type=F32)
                acc_ref[i] = (acc_ref[i] + pv) * alpha

    def tile(ref, t):
        return ref[pl.ds(pl.multiple_of(t * KV_TILE, KV_TILE), KV_TILE), :]

    def ktile(t):
        return tuple(tile(r, t) for r in k_refs)

    def finish(pending, p_meta):
        meta_mask = lax.broadcasted_iota(jnp.int32, wide, 1) < N_META
        block(tuple(r[...] for r in km_refs), meta_mask, p_meta, pending)
        for i in range(2):
            acc_ref[i] = acc_ref[i] + jnp.dot(p_meta[i, :, :LANES], vm_ref[...], preferred_element_type=F32)

    acc_ref[...] = jnp.zeros(acc_ref.shape, F32)
    causal = (lax.broadcasted_iota(jnp.int32, (Q_TILE, KV_TILE), 1)
              <= lax.broadcasted_iota(jnp.int32, (Q_TILE, KV_TILE), 0))
    block(ktile(qi), causal, pa_ref, None)

    def run(t0, n):
        slots = (pa_ref, pb_ref)
        before = jnp.where(t0 == 0, qi, t0 - 1)
        for d in range(n):
            block(ktile(t0 + d), None, slots[(d + 1) % 2], (slots[d % 2], tile(v_ref, before)))
            before = t0 + d
        return before

    def unrolled(jj, carry):
        run(ATTN_UNROLL * jj, ATTN_UNROLL)
        return carry

    lax.fori_loop(0, qi // ATTN_UNROLL, unrolled, 0)
    done = (qi // ATTN_UNROLL) * ATTN_UNROLL

    def tail(left):
        before = run(done, left)
        slots = (pa_ref, pb_ref)
        finish((slots[left % 2], tile(v_ref, before)), slots[(left + 1) % 2])

    for left in range(ATTN_UNROLL):
        pl.when(qi - done == left)(functools.partial(tail, left))

    lam = (jnp.exp(jnp.sum(lq1_ref[...] * lk1_ref[...], axis=1, keepdims=True))
           - jnp.exp(jnp.sum(lq2_ref[...] * lk2_ref[...], axis=1, keepdims=True)) + LAM_INIT)
    l1 = jnp.sum(l_ref[0], axis=1, keepdims=True)
    l2 = jnp.sum(l_ref[1], axis=1, keepdims=True)
    o = acc_ref[0] / l1 - lam * (acc_ref[1] / l2)
    o_ref[q_rows, :] = (_rms(o, sg_ref[...]) * (1.0 - LAM_INIT)).astype(BF16)


def _attention(q, k, v, km, vm, lq1, lk1, lq2, lk2, sg, bsz, seq):
    assert Q_TILE == KV_TILE and seq % Q_TILE == 0
    pairs = N_HEADS // 2
    kvmap = lambda b, h: (b, h)
    map1 = pl.BlockSpec((seq, LANES), lambda b, h: (b, h // 2))
    map2 = pl.BlockSpec((seq, LANES), lambda b, h: (b, pairs + h // 2))
    meta1 = pl.BlockSpec((LANES, LANES), lambda b, h: (0, h // 2))
    meta2 = pl.BlockSpec((LANES, LANES), lambda b, h: (0, pairs + h // 2))
    const = lambda b, h: (0, 0)
    vec = pl.BlockSpec((1, QK_DIM), const)
    return pl.pallas_call(
        _attn_kernel,
        grid=(bsz, N_HEADS),
        in_specs=[
            map1, map2, map1, map2,
            pl.BlockSpec((seq, LANES), kvmap),
            meta1, meta2,
            pl.BlockSpec((LANES, LANES), lambda b, h: (0, h)),
            vec, vec, vec, vec,
            pl.BlockSpec((1, V_DIM), const),
        ],
        out_specs=pl.BlockSpec((seq, LANES), kvmap),
        out_shape=jax.ShapeDtypeStruct((bsz * seq, ATTN_WIDTH), BF16),
        scratch_shapes=[
            pltpu.VMEM((2, Q_TILE, LANES), F32),
            pltpu.VMEM((2, Q_TILE, LANES), F32),
            pltpu.VMEM((2, Q_TILE, V_DIM), F32),
            pltpu.VMEM((2, Q_TILE, KV_TILE), BF16),
            pltpu.VMEM((2, Q_TILE, KV_TILE), BF16),
        ],
        compiler_params=pltpu.CompilerParams(
            dimension_semantics=("arbitrary", "arbitrary"), vmem_limit_bytes=VMEM_LIMIT),
        name="diff_attention",
    )(q, q, k, k, v, km, km, vm, lq1, lk1, lq2, lk2, sg)


def _ssm_kernel(u_ref, um_ref, bb_ref, bc_ref, ccr_ref, cci_ref, loc_ref, blk_ref, d_ref, y_ref,
                tz_ref, bs_ref, cs_ref, ub_ref, buf_a, buf_b, cb_a, cb_b, rep_ref, yt_ref, *, n_chunks):
    n_tiles = OCT_STATE // LANES
    n_blocks = n_chunks // SUBLANES
    blk_steps = n_blocks.bit_length() - 1
    blk_pad = cb_a.shape[1] - n_blocks

    @pl.when((pl.program_id(0) == 0) & (pl.program_id(1) == 0))
    def _():
        for buf, pad in ((buf_a, SCAN_PAD), (buf_b, SCAN_PAD), (cb_a, blk_pad), (cb_b, blk_pad)):
            buf[:, :pad, :] = jnp.zeros((n_tiles, pad, LANES), F32)

    @pl.when(pl.program_id(1) == 0)
    def _():
        def group_of(shape, axis):
            return (lax.broadcasted_iota(jnp.int32, shape, axis) // SSM_GROUP) % OCT

        def split(x):
            head = x.astype(BF16)
            return head, (x - head.astype(F32)).astype(BF16)

        wide_shape = (LANES, OCT * SSM_STATE)
        own_cols = group_of(wide_shape, 0) == lax.broadcasted_iota(jnp.int32, wide_shape, 1) // SSM_STATE
        bbar = [split(jnp.where(own_cols, jnp.concatenate([bb_ref[part, 0]] * (OCT // 2), axis=1), 0.0))
                for part in range(2)]
        g_lane = group_of((SSM_STATE, LANES), 1)
        lag_blk = []
        for n in range(CHUNK):
            terms = []
            for part, c_ref in enumerate((ccr_ref, cci_ref)):
                c_n = c_ref[n]
                ca_head, ca_rest = split(jnp.concatenate(
                    [jnp.where(g_lane == g, c_n, 0.0) for g in range(OCT)], axis=0))
                b_head, b_rest = bbar[part]
                terms.append(jnp.dot(b_head, ca_head, preferred_element_type=F32)
                             + jnp.dot(b_head, ca_rest, preferred_element_type=F32)
                             + jnp.dot(b_rest, ca_head, preferred_element_type=F32))
            lag_blk.append((terms[0] - terms[1]).astype(BF16))
        for m in range(CHUNK // 2):
            tz_ref[m, :LANES, :LANES] = lag_blk[2 * m]
            tz_ref[m, :LANES, LANES:] = lag_blk[2 * m + 1]
            tz_ref[m, LANES:, :LANES] = lag_blk[2 * m - 1] if m else jnp.zeros((LANES, LANES), BF16)
            tz_ref[m, LANES:, LANES:] = lag_blk[2 * m]
        bc = bc_ref[:, 0].reshape(OCT_IN, 4 * SSM_STATE)
        g_row = group_of(bc.shape, 0)
        w_col = (lax.broadcasted_iota(jnp.int32, bc.shape, 1) // SSM_STATE) % 2
        for qq in range(OCT // 2):
            bs_ref[:, qq * MXU_DIM:(qq + 1) * MXU_DIM] = jnp.where(g_row == 2 * qq + w_col, bc, 0.0).astype(BF16)
        g_col = group_of((SSM_STATE, LANES), 1)
        for s in range(CHUNK):
            c_s = (ccr_ref[s + 1], -cci_ref[s + 1])
            for qq in range(OCT // 2):
                for part in range(2):
                    for w in range(2):
                        r0 = qq * MXU_DIM + part * LANES + w * SSM_STATE
                        cs_ref[r0:r0 + SSM_STATE, s * LANES:(s + 1) * LANES] = jnp.where(
                            g_col == 2 * qq + w, c_s[part], 0.0).astype(BF16)

    def slab(ref, t, rows):
        return ref[pl.ds(t, rows, stride=CHUNK), :]

    for t in range(CHUNK):
        ub_ref[t // 2, :, (t % 2) * LANES:(t % 2 + 1) * LANES] = slab(u_ref, t, n_chunks).astype(BF16)

    z = jnp.dot(ub_ref[0], bs_ref[:MXU_DIM, :], preferred_element_type=F32)
    for tt in range(1, CHUNK // 2):
        z = z + jnp.dot(ub_ref[tt], bs_ref[tt * MXU_DIM:(tt + 1) * MXU_DIM, :], preferred_element_type=F32)
    um = jnp.concatenate([jnp.broadcast_to(um_ref[t:t + 1, :], (SUBLANES, LANES)) for t in range(CHUNK)], axis=1)
    zm = jnp.dot(um.astype(BF16), bs_ref[...], preferred_element_type=F32)[0:1, :]
    for ct in range(n_tiles):
        cols = slice(ct * LANES, (ct + 1) * LANES)
        buf_a[ct, SCAN_PAD:, :] = z[:, cols]
        for buf, pad in ((buf_a, SCAN_PAD), (buf_b, SCAN_PAD), (cb_a, blk_pad), (cb_b, blk_pad)):
            buf[ct, pad - 1:pad, :] = zm[:, cols]

    def within_chunk(ss):
        acc = jnp.dot(ub_ref[0], tz_ref[ss], preferred_element_type=F32)
        for tt in range(1, ss + 1):
            acc = acc + jnp.dot(ub_ref[tt], tz_ref[ss - tt], preferred_element_type=F32)
        yt_ref[ss] = acc

    assert LOCAL_STEPS + blk_steps >= CHUNK // 2 and (1 << LOCAL_STEPS) == SUBLANES
    todo = list(range(CHUNK // 2 - 1, -1, -1))
    blocked = (n_blocks, SUBLANES, LANES)

    def lanes_of(pair):
        return slice(pair * LANES, (pair + 1) * LANES)

    def combine(x, p, ar, ai):
        return x[0] + ar * p[0] - ai * p[1], x[1] + ar * p[1] + ai * p[0]

    src, dst = buf_a, buf_b
    for step in range(LOCAL_STEPS):
        shift = 1 << step
        within_chunk(todo.pop(0))
        for pair in range(OCT // 2):
            ar, ai = (loc_ref[0, step, part, :, lanes_of(pair)] for part in range(2))
            x = [src[2 * pair + part, SCAN_PAD:, :].reshape(blocked) for part in range(2)]
            p = [src[2 * pair + part, SCAN_PAD - shift:SCAN_PAD - shift + n_chunks, :].reshape(blocked)
                 for part in range(2)]
            for part, val in enumerate(combine(x, p, ar, ai)):
                dst[2 * pair + part, SCAN_PAD:, :] = val.reshape(n_chunks, LANES)
        src, dst = dst, src

    for ct in range(n_tiles):
        cb_a[ct, blk_pad:, :] = src[ct, pl.ds(SCAN_PAD + SUBLANES - 1, n_blocks, stride=SUBLANES), :]
    csrc, cdst = cb_a, cb_b
    for step in range(blk_steps):
        shift = 1 << step
        if todo:
            within_chunk(todo.pop(0))
        for pair in range(OCT // 2):
            ar = blk_ref[0, step:step + 1, lanes_of(pair)]
            ai = blk_ref[0, blk_steps + step:blk_steps + step + 1, lanes_of(pair)]
            x = [csrc[2 * pair + part, blk_pad:, :] for part in range(2)]
            p = [csrc[2 * pair + part, blk_pad - shift:blk_pad - shift + n_blocks, :] for part in range(2)]
            for part, val in enumerate(combine(x, p, ar, ai)):
                cdst[2 * pair + part, blk_pad:, :] = val
        csrc, cdst = cdst, csrc

    for pair in range(OCT // 2):
        for part in range(2):
            entering = csrc[2 * pair + part, blk_pad - 1:blk_pad - 1 + n_blocks, :]
            for row in range(SUBLANES):
                rep_ref[pair, part, pl.ds(row, n_blocks, stride=SUBLANES), :] = entering
        ar, ai = (loc_ref[0, LOCAL_STEPS, part, :, lanes_of(pair)] for part in range(2))
        x = [src[2 * pair + part, SCAN_PAD:, :].reshape(blocked) for part in range(2)]
        p = [rep_ref[pair, part].reshape(blocked) for part in range(2)]
        for part, val in enumerate(combine(x, p, ar, ai)):
            dst[2 * pair + part, SCAN_PAD:, :] = val.reshape(n_chunks, LANES)

    start_state = jnp.concatenate(
        [dst[ct, SCAN_PAD - 1:SCAN_PAD - 1 + n_chunks, :] for ct in range(n_tiles)], axis=1).astype(BF16)
    for ss in range(CHUNK // 2):
        cols = slice(ss * MXU_DIM, (ss + 1) * MXU_DIM)
        acc = yt_ref[ss] + jnp.dot(start_state, cs_ref[:, cols], preferred_element_type=F32)
        for half in range(2):
            t = 2 * ss + half
            y = acc[:, half * LANES:(half + 1) * LANES] + d_ref[0] * slab(u_ref, t, n_chunks)
            y_ref[pl.ds(t, n_chunks, stride=CHUNK), :] = 0.5 * y * (1.0 + lax.erf(y * (1.0 / math.sqrt(2.0))))


def _ssm(u, u_meta, bb2, bc, cc_re, cc_im, loc, blk, dp, bsz, seq):
    n_chunks = seq // CHUNK
    n_blocks = n_chunks // SUBLANES
    assert n_blocks & (n_blocks - 1) == 0 and N_META == CHUNK
    octet = lambda o, b: (o, 0, 0)
    n_tiles = OCT_STATE // LANES
    scan_buf = pltpu.VMEM((n_tiles, SCAN_PAD + n_chunks, LANES), F32)
    block_buf = pltpu.VMEM((n_tiles, n_blocks // 2 + n_blocks, LANES), F32)
    return pl.pallas_call(
        functools.partial(_ssm_kernel, n_chunks=n_chunks),
        grid=(N_OCT, bsz),
        in_specs=[
            pl.BlockSpec((seq, LANES), lambda o, b: (b, o)),
            pl.BlockSpec((N_META, LANES), lambda o, b: (0, o)),
            pl.BlockSpec((2, 1, LANES, 2 * SSM_STATE), lambda o, b: (0, o, 0, 0)),
            pl.BlockSpec((CHUNK, 1, LANES, 4 * SSM_STATE), lambda o, b: (0, o, 0, 0)),
            pl.BlockSpec((CHUNK + 1, SSM_STATE, LANES), lambda o, b: (0, 0, o)),
            pl.BlockSpec((CHUNK + 1, SSM_STATE, LANES), lambda o, b: (0, 0, o)),
            pl.BlockSpec((1,) + loc.shape[1:], lambda o, b: (o, 0, 0, 0, 0)),
            pl.BlockSpec((1,) + blk.shape[1:], octet),
            pl.BlockSpec((1, 1, LANES), octet),
        ],
        out_specs=pl.BlockSpec((seq, LANES), lambda o, b: (b, o)),
        out_shape=jax.ShapeDtypeStruct((bsz * seq, SSM_WIDTH), F32),
        scratch_shapes=[
            pltpu.VMEM((CHUNK // 2, MXU_DIM, MXU_DIM), BF16),
            pltpu.VMEM((OCT_IN, OCT_STATE), BF16),
            pltpu.VMEM((OCT_STATE, OCT_IN), BF16),
            pltpu.VMEM((CHUNK // 2, n_chunks, MXU_DIM), BF16), scan_buf, scan_buf, block_buf, block_buf,
            pltpu.VMEM((OCT // 2, 2, n_chunks, LANES), F32),
            pltpu.VMEM((CHUNK // 2, n_chunks, MXU_DIM), F32)],
        compiler_params=pltpu.CompilerParams(
            dimension_semantics=("arbitrary", "arbitrary"), vmem_limit_bytes=VMEM_LIMIT),
        name="s5_chunked",
    )(u, u_meta, bb2, bc, cc_re, cc_im, loc, blk, dp)


def _ssm_weights(a_re, a_im, log_dt, b_re, b_im, c_re, c_im, d_skip, n_chunks):
    dt = jnp.exp(log_dt)[:, None]
    lam_re, lam_im = a_re * dt, a_im * dt

    def power(n, transposed=False):
        n = n[:, None, None]
        l_re, l_im = (lam_re.T, lam_im.T) if transposed else (lam_re, lam_im)
        mag = jnp.exp(n * l_re)
        return mag * jnp.cos(n * l_im), mag * jnp.sin(n * l_im)

    ab_re, ab_im = jnp.exp(lam_re) * jnp.cos(lam_im), jnp.exp(lam_re) * jnp.sin(lam_im)
    den = a_re * a_re + a_im * a_im
    nr, ni = ab_re - 1.0, ab_im
    f_re = (nr * a_re + ni * a_im) / den
    f_im = (ni * a_re - nr * a_im) / den
    bb_re = f_re[..., None] * b_re - f_im[..., None] * b_im
    bb_im = f_re[..., None] * b_im + f_im[..., None] * b_re

    p_re, p_im = power(jnp.arange(CHUNK + 1, dtype=F32))

    ct_re, ct_im = c_re.transpose(2, 0, 1)[None], c_im.transpose(2, 0, 1)[None]
    pt_re, pt_im = (t[..., None] for t in power(jnp.arange(CHUNK + 1, dtype=F32), transposed=True))
    ca_re = ct_re * pt_re - ct_im * pt_im
    ca_im = ct_re * pt_im + ct_im * pt_re
    cc_re = ca_re.reshape(CHUNK + 1, SSM_STATE, SSM_WIDTH)
    cc_im = ca_im.reshape(CHUNK + 1, SSM_STATE, SSM_WIDTH)

    r_re, r_im = p_re[CHUNK - 1::-1][:, :, None, :], p_im[CHUNK - 1::-1][:, :, None, :]
    bt_re, bt_im = bb_re.transpose(0, 2, 1)[None], bb_im.transpose(0, 2, 1)[None]
    bb2 = jnp.stack([jnp.concatenate([t, t], axis=-1).reshape(N_OCT, LANES, 2 * SSM_STATE) for t in (bt_re, bt_im)])
    bs_re = r_re * bt_re - r_im * bt_im
    bs_im = r_re * bt_im + r_im * bt_re
    bc = jnp.concatenate([bs_re, bs_re, bs_im, bs_im], axis=-1).reshape(CHUNK, N_OCT, LANES, 4 * SSM_STATE)

    rows = jnp.arange(SUBLANES)
    strides = 2 ** jnp.arange(LOCAL_STEPS)
    in_block = (rows[None, :] >= strides[:, None])[:, :, None, None]
    blk_steps = (n_chunks // SUBLANES).bit_length() - 1

    def pack(m):
        m = m.reshape(m.shape[:-2] + (N_OCT, OCT_STATE // 2))
        return jnp.moveaxis(m, -2, 0)

    loc, blk = [], []
    for carry, block_stride in zip(power(CHUNK * (rows + 1.0)), power(CHUNK * SUBLANES * 2.0 ** jnp.arange(blk_steps))):
        steps = jnp.where(in_block, carry[strides - 1][:, None], 0.0)
        loc.append(pack(jnp.concatenate([steps, carry[None]], axis=0)))
        blk.append(pack(block_stride))
    loc = jnp.stack(loc, axis=2)
    blk = jnp.concatenate(blk, axis=1)

    dp = d_skip.reshape(N_OCT, 1, LANES)
    return bb2, bc, cc_re, cc_im, loc, blk, dp


def _out_ffn_kernel(x_ref, o_ref, y_ref, wglu_ref, bglu_ref, sg_ref, wout_ref, pmg_ref, pfg_ref,
                    wg_ref, wu_ref, wd_ref, pog_ref, out_ref):
    y = y_ref[...]
    gate = jnp.dot(y.astype(BF16), wglu_ref[...], preferred_element_type=F32) + bglu_ref[...]
    y = _rms(y * jax.nn.sigmoid(gate), sg_ref[...]).astype(BF16)
    mix = (jnp.dot(o_ref[...], wout_ref[:ATTN_WIDTH, :], preferred_element_type=F32)
           + jnp.dot(y, wout_ref[ATTN_WIDTH:, :], preferred_element_type=F32))
    h1 = x_ref[...] + _rms(mix, pmg_ref[...])
    h2 = _rms(h1, pfg_ref[...]).astype(BF16)
    g = jnp.dot(h2, wg_ref[...], preferred_element_type=F32)
    up = jnp.dot(h2, wu_ref[...], preferred_element_type=F32)
    f = (g * jax.nn.sigmoid(g) * up).astype(BF16)
    f = jnp.dot(f, wd_ref[...], preferred_element_type=F32)
    out_ref[...] = h1 + _rms(f, pog_ref[...])


def _out_ffn(x2d, o, y, wglu, bglu, sg, wout, pmg, pfg, wg, wu, wd, pog, tm):
    n = x2d.shape[0]
    row = lambda i: (i, 0)
    const = lambda i: (0, 0)

    def resident(shape):
        return pl.BlockSpec(shape, const, pipeline_mode=pl.Buffered(1))

    return pl.pallas_call(
        _out_ffn_kernel,
        grid=(n // tm,),
        in_specs=[
            pl.BlockSpec((tm, D_MODEL), row),
            pl.BlockSpec((tm, ATTN_WIDTH), row),
            pl.BlockSpec((tm, SSM_WIDTH), row),
            resident((SSM_WIDTH, SSM_WIDTH)),
            resident((1, SSM_WIDTH)),
            resident((1, SSM_WIDTH)),
            resident((D_MODEL, D_MODEL)),
            resident((1, D_MODEL)),
            resident((1, D_MODEL)),
            resident((D_MODEL, D_FF)),
            resident((D_MODEL, D_FF)),
            resident((D_FF, D_MODEL)),
            resident((1, D_MODEL)),
        ],
        out_specs=pl.BlockSpec((tm, D_MODEL), row),
        out_shape=jax.ShapeDtypeStruct((n, D_MODEL), F32),
        compiler_params=pltpu.CompilerParams(
            dimension_semantics=("arbitrary",), vmem_limit_bytes=VMEM_LIMIT),
        name="out_ffn",
    )(x2d, o, y, wglu, bglu, sg, wout, pmg, pfg, wg, wu, wd, pog)


def _rope_angles(pos):
    d = jnp.arange(LANES) % QK_DIM
    inv_freq = ROPE_THETA ** (-(2 * (d % (ROT_DIM // 2))).astype(F32) / ROT_DIM)
    ang = pos[:, None] * jnp.where(d < ROT_DIM, inv_freq, 0.0)[None, :]
    return jnp.cos(ang), jnp.sin(ang)


def _scale_w_in(w):
    scale = math.log2(math.e) / math.sqrt(QK_DIM)
    col_scale = jnp.where(jnp.arange(w.shape[1]) < ATTN_WIDTH, scale, 1.0).astype(F32)
    return (w * col_scale[None, :]).astype(BF16)


def kernel(x, meta, pre_mix_g, w_in, lambda_q1, lambda_k1, lambda_q2, lambda_k2, subln_g, a_re, a_im, log_dt,
           b_re, b_im, c_re, c_im, d_skip, w_glu, b_glu, ssm_out_g, w_out, post_mix_g, pre_ffn_g, w_gate,
           w_up, w_down, post_ffn_g):
    bsz, seq, _ = x.shape
    n = bsz * seq
    x2d = x.reshape(n, D_MODEL)
    row = lambda t: t[0].reshape(1, -1)

    w_in_p = _scale_w_in(w_in[0])
    g0 = row(pre_mix_g)
    q, k, v, u = _in_proj(x2d, g0, w_in_p, N_META, seq, ROW_TILE)
    _, k_m, v_m, u_m = _in_proj(meta, g0, w_in_p, 0, N_META, N_META)

    pad_meta = lambda t: jnp.pad(t, ((0, LANES - N_META), (0, 0)))
    o = _attention(q, k, v, pad_meta(k_m), pad_meta(v_m), lambda_q1, lambda_k1, lambda_q2, lambda_k2,
                   row(subln_g), bsz, seq)

    ssm_w = _ssm_weights(a_re[0], a_im[0], log_dt[0], b_re[0], b_im[0], c_re[0], c_im[0], d_skip[0],
                         seq // CHUNK)
    y = _ssm(u, u_m, *ssm_w, bsz, seq)

    out = _out_ffn(x2d, o, y, w_glu[0].astype(BF16), row(b_glu), row(ssm_out_g), w_out[0].astype(BF16),
                   row(post_mix_g), row(pre_ffn_g), w_gate[0].astype(BF16), w_up[0].astype(BF16),
                   w_down[0].astype(BF16), row(post_ffn_g), ROW_TILE)
    return out.reshape(bsz, seq, D_MODEL)
```

```python
import functools
import math

import jax
import jax.numpy as jnp
from jax import lax
from jax.experimental import pallas as pl
from jax.experimental.pallas import tpu as pltpu

D_MODEL = 1024
N_META = 16
N_HEADS = 4
QK_DIM = 64
V_DIM = 128
ROT_DIM = 16
ROPE_THETA = 500000.0
SSM_GROUP = 16
N_GROUPS = 32
SSM_STATE = 64
SSM_WIDTH = 512
ATTN_WIDTH = 512
D_FF = 2816
EPS = 1e-6
LAM_INIT = 0.8 - 0.6 * math.exp(-0.3 * 0)

LANES = 128
SUBLANES = 8
MXU_DIM = 256
CHUNK = 16
OCT = LANES // SSM_GROUP
N_OCT = N_GROUPS // OCT
OCT_IN = CHUNK * LANES
OCT_STATE = OCT * 2 * SSM_STATE
SCAN_PAD = SUBLANES
LOCAL_STEPS = 3

ROW_TILE = 512
Q_TILE = 512
KV_TILE = 512
ATTN_UNROLL = 4
VMEM_LIMIT = 56 * 1024 * 1024

F32 = jnp.float32
BF16 = jnp.bfloat16


def _rms(x, g):
    return x * lax.rsqrt(jnp.mean(x * x, axis=-1, keepdims=True) + EPS) * g


def _in_proj_kernel(x_ref, g_ref, w_ref, cr_ref, sr_ref, cb_ref, sb_ref, q_ref, k_ref, v_ref, u_ref):
    h = _rms(x_ref[...], g_ref[...]).astype(BF16)
    proj = jnp.dot(h, w_ref[...], preferred_element_type=F32)
    cos = cb_ref[0] * cr_ref[...] - sb_ref[0] * sr_ref[...]
    sin = sb_ref[0] * cr_ref[...] + cb_ref[0] * sr_ref[...]
    d = lax.broadcasted_iota(jnp.int32, (1, LANES), 1) % QK_DIM
    sa = sin * jnp.where(d < ROT_DIM // 2, -1.0, 0.0)
    sb = sin * jnp.where((d >= ROT_DIM // 2) & (d < ROT_DIM), 1.0, 0.0)

    def rope(t):
        return (t * cos + pltpu.roll(t, LANES - ROT_DIM // 2, axis=1) * sa
                + pltpu.roll(t, ROT_DIM // 2, axis=1) * sb)

    for c in range(ATTN_WIDTH // LANES):
        sl = slice(c * LANES, (c + 1) * LANES)
        q_ref[:, sl] = rope(proj[:, c * LANES:(c + 1) * LANES]).astype(BF16)
        k_ref[:, sl] = rope(proj[:, ATTN_WIDTH + c * LANES:ATTN_WIDTH + (c + 1) * LANES]).astype(BF16)
    v_ref[...] = proj[:, 2 * ATTN_WIDTH:3 * ATTN_WIDTH].astype(BF16)
    u_ref[...] = proj[:, 3 * ATTN_WIDTH:]


def _in_proj(x2d, g, w, first_pos, seq, tm):
    n = x2d.shape[0]
    tiles_per_seq = seq // tm
    cos_r, sin_r = _rope_angles(jnp.arange(tm, dtype=F32))
    cos_b, sin_b = _rope_angles(first_pos + tm * jnp.arange(tiles_per_seq, dtype=F32))
    row = lambda i: (i, 0)
    const = lambda i: (0, 0)
    base = pl.BlockSpec((1, 1, LANES), lambda i: (i % tiles_per_seq, 0, 0))
    return pl.pallas_call(
        _in_proj_kernel,
        grid=(n // tm,),
        in_specs=[
            pl.BlockSpec((tm, D_MODEL), row),
            pl.BlockSpec((1, D_MODEL), const),
            pl.BlockSpec((D_MODEL, 4 * ATTN_WIDTH), const),
            pl.BlockSpec((tm, LANES), const),
            pl.BlockSpec((tm, LANES), const),
            base, base,
        ],
        out_specs=[pl.BlockSpec((tm, ATTN_WIDTH), row)] * 4,
        out_shape=[jax.ShapeDtypeStruct((n, ATTN_WIDTH), BF16)] * 3
        + [jax.ShapeDtypeStruct((n, SSM_WIDTH), F32)],
        compiler_params=pltpu.CompilerParams(
            dimension_semantics=("arbitrary",), vmem_limit_bytes=VMEM_LIMIT),
        name="in_proj",
    )(x2d, g, w, cos_r, sin_r, cos_b[:, None, :], sin_b[:, None, :])


def _attn_kernel(*refs):
    n_tiles = refs[0].shape[0] // Q_TILE

    def q_tile(qi, carry):
        _attn_q_tile(qi, *refs)
        return carry

    lax.fori_loop(0, n_tiles, q_tile, 0)


def _attn_q_tile(qi, q1_ref, q2_ref, k1_ref, k2_ref, v_ref, km1_ref, km2_ref, vm_ref,
                 lq1_ref, lk1_ref, lq2_ref, lk2_ref, sg_ref,
                 o_ref, m_ref, l_ref, acc_ref, pa_ref, pb_ref):
    q_rows = pl.ds(pl.multiple_of(qi * Q_TILE, Q_TILE), Q_TILE)
    own = lax.broadcasted_iota(jnp.int32, (Q_TILE, LANES), 1) // QK_DIM == pl.program_id(1) % 2
    qs = tuple(jnp.where(own, r[q_rows, :], jnp.zeros((Q_TILE, LANES), BF16)) for r in (q1_ref, q2_ref))
    k_refs, km_refs = (k1_ref, k2_ref), (km1_ref, km2_ref)
    nt = (((1,), (1,)), ((), ()))
    wide = (Q_TILE, LANES)

    def block(k, mask, p_out, prev):
        n_col = k[0].shape[0] // LANES
        for i in range(2):
            s = lax.dot_general(qs[i], k[i], nt, preferred_element_type=F32)
            if mask is not None:
                s = jnp.where(mask, s, -jnp.inf)
            cols = [s[:, c * LANES:(c + 1) * LANES] for c in range(n_col)]
            m_tile = functools.reduce(jnp.maximum, cols)
            m_new = jnp.broadcast_to(jnp.max(m_tile, axis=1, keepdims=True), wide)
            if prev is not None:
                m_old = m_ref[i]
                m_new = jnp.maximum(m_old, m_new)
                alpha = jnp.exp2(m_old - m_new)
            m_ref[i] = m_new
            ps = [jnp.exp2(c - m_new) for c in cols]
            l_tile = functools.reduce(jnp.add, ps)
            l_ref[i] = l_tile if prev is None else alpha * l_ref[i] + l_tile
            for c in range(n_col):
                p_out[i, :, c * LANES:(c + 1) * LANES] = ps[c].astype(BF16)
            if prev is not None:
                p_prev, v_prev = prev
                pv = jnp.dot(p_prev[i], v_prev, preferred_element_type=F32)
                acc_ref[i] = (acc_ref[i] + pv) * alpha

    def tile(ref, t):
        return ref[pl.ds(pl.multiple_of(t * KV_TILE, KV_TILE), KV_TILE), :]

    def ktile(t):
        return tuple(tile(r, t) for r in k_refs)

    def finish(pending, p_meta):
        meta_mask = lax.broadcasted_iota(jnp.int32, wide, 1) < N_META
        block(tuple(r[...] for r in km_refs), meta_mask, p_meta, pending)
        for i in range(2):
            acc_ref[i] = acc_ref[i] + jnp.dot(p_meta[i, :, :LANES], vm_ref[...], preferred_element_type=F32)

    acc_ref[...] = jnp.zeros(acc_ref.shape, F32)
    causal = (lax.broadcasted_iota(jnp.int32, (Q_TILE, KV_TILE), 1)
              <= lax.broadcasted_iota(jnp.int32, (Q_TILE, KV_TILE), 0))
    block(ktile(qi), causal, pa_ref, None)

    def run(t0, n):
        slots = (pa_ref, pb_ref)
        before = jnp.where(t0 == 0, qi, t0 - 1)
        for d in range(n):
            block(ktile(t0 + d), None, slots[(d + 1) % 2], (slots[d % 2], tile(v_ref, before)))
            before = t0 + d
        return before

    def unrolled(jj, carry):
        run(ATTN_UNROLL * jj, ATTN_UNROLL)
        return carry

    lax.fori_loop(0, qi // ATTN_UNROLL, unrolled, 0)
    done = (qi // ATTN_UNROLL) * ATTN_UNROLL

    def tail(left):
        before = run(done, left)
        slots = (pa_ref, pb_ref)
        finish((slots[left % 2], tile(v_ref, before)), slots[(left + 1) % 2])

    for left in range(ATTN_UNROLL):
        pl.when(qi - done == left)(functools.partial(tail, left))

    lam = (jnp.exp(jnp.sum(lq1_ref[...] * lk1_ref[...], axis=1, keepdims=True))
           - jnp.exp(jnp.sum(lq2_ref[...] * lk2_ref[...], axis=1, keepdims=True)) + LAM_INIT)
    l1 = jnp.sum(l_ref[0], axis=1, keepdims=True)
    l2 = jnp.sum(l_ref[1], axis=1, keepdims=True)
    o = acc_ref[0] / l1 - lam * (acc_ref[1] / l2)
    o_ref[q_rows, :] = (_rms(o, sg_ref[...]) * (1.0 - LAM_INIT)).astype(BF16)


def _attention(q, k, v, km, vm, lq1, lk1, lq2, lk2, sg, bsz, seq):
    assert Q_TILE == KV_TILE and seq % Q_TILE == 0
    pairs = N_HEADS // 2
    kvmap = lambda b, h: (b, h)
    map1 = pl.BlockSpec((seq, LANES), lambda b, h: (b, h // 2))
    map2 = pl.BlockSpec((seq, LANES), lambda b, h: (b, pairs + h // 2))
    meta1 = pl.BlockSpec((LANES, LANES), lambda b, h: (0, h // 2))
    meta2 = pl.BlockSpec((LANES, LANES), lambda b, h: (0, pairs + h // 2))
    const = lambda b, h: (0, 0)
    vec = pl.BlockSpec((1, QK_DIM), const)
    return pl.pallas_call(
        _attn_kernel,
        grid=(bsz, N_HEADS),
        in_specs=[
            map1, map2, map1, map2,
            pl.BlockSpec((seq, LANES), kvmap),
            meta1, meta2,
            pl.BlockSpec((LANES, LANES), lambda b, h: (0, h)),
            vec, vec, vec, vec,
            pl.BlockSpec((1, V_DIM), const),
        ],
        out_specs=pl.BlockSpec((seq, LANES), kvmap),
        out_shape=jax.ShapeDtypeStruct((bsz * seq, ATTN_WIDTH), BF16),
        scratch_shapes=[
            pltpu.VMEM((2, Q_TILE, LANES), F32),
            pltpu.VMEM((2, Q_TILE, LANES), F32),
            pltpu.VMEM((2, Q_TILE, V_DIM), F32),
            pltpu.VMEM((2, Q_TILE, KV_TILE), BF16),
            pltpu.VMEM((2, Q_TILE, KV_TILE), BF16),
        ],
        compiler_params=pltpu.CompilerParams(
            dimension_semantics=("arbitrary", "arbitrary"), vmem_limit_bytes=VMEM_LIMIT),
        name="diff_attention",
    )(q, q, k, k, v, km, km, vm, lq1, lk1, lq2, lk2, sg)


def _ssm_kernel(u_ref, um_ref, bb_ref, bc_ref, ccr_ref, cci_ref, loc_ref, blk_ref, d_ref, y_ref,
                tz_ref, bs_ref, cs_ref, ub_ref, buf_a, buf_b, cb_a, cb_b, rep_ref, yt_ref, *, n_chunks):
    n_tiles = OCT_STATE // LANES
    n_blocks = n_chunks // SUBLANES
    blk_steps = n_blocks.bit_length() - 1
    blk_pad = cb_a.shape[1] - n_blocks

    @pl.when((pl.program_id(0) == 0) & (pl.program_id(1) == 0))
    def _():
        for buf, pad in ((buf_a, SCAN_PAD), (buf_b, SCAN_PAD), (cb_a, blk_pad), (cb_b, blk_pad)):
            buf[:, :pad, :] = jnp.zeros((n_tiles, pad, LANES), F32)

    @pl.when(pl.program_id(1) == 0)
    def _():
        def group_of(shape, axis):
            return (lax.broadcasted_iota(jnp.int32, shape, axis) // SSM_GROUP) % OCT

        def split(x):
            head = x.astype(BF16)
            return head, (x - head.astype(F32)).astype(BF16)

        wide_shape = (LANES, OCT * SSM_STATE)
        own_cols = group_of(wide_shape, 0) == lax.broadcasted_iota(jnp.int32, wide_shape, 1) // SSM_STATE
        bbar = [split(jnp.where(own_cols, jnp.concatenate([bb_ref[part, 0]] * (OCT // 2), axis=1), 0.0))
                for part in range(2)]
        lag_blk = []
        for n in range(0, CHUNK, 2):
            terms = []
            for part, c_ref in enumerate((ccr_ref, cci_ref)):
                c_n = jnp.concatenate([c_ref[n], c_ref[n + 1]], axis=1)
                g_lane = group_of(c_n.shape, 1)
                ca_head, ca_rest = split(jnp.concatenate(
                    [jnp.where(g_lane == g, c_n, 0.0) for g in range(OCT)], axis=0))
                b_head, b_rest = bbar[part]
                terms.append(jnp.dot(b_head, ca_head, preferred_element_type=F32)
                             + jnp.dot(b_head, ca_rest, preferred_element_type=F32)
                             + jnp.dot(b_rest, ca_head, preferred_element_type=F32))
            both = (terms[0] - terms[1]).astype(BF16)
            lag_blk += [both[:, :LANES], both[:, LANES:]]
        for m in range(CHUNK // 2):
            tz_ref[m, :LANES, :LANES] = lag_blk[2 * m]
            tz_ref[m, :LANES, LANES:] = lag_blk[2 * m + 1]
            tz_ref[m, LANES:, :LANES] = lag_blk[2 * m - 1] if m else jnp.zeros((LANES, LANES), BF16)
            tz_ref[m, LANES:, LANES:] = lag_blk[2 * m]
        bc = bc_ref[:, 0].reshape(OCT_IN, 4 * SSM_STATE)
        g_row = group_of(bc.shape, 0)
        w_col = (lax.broadcasted_iota(jnp.int32, bc.shape, 1) // SSM_STATE) % 2
        for qq in range(OCT // 2):
            bs_ref[:, qq * MXU_DIM:(qq + 1) * MXU_DIM] = jnp.where(g_row == 2 * qq + w_col, bc, 0.0).astype(BF16)
        g_col = group_of((SSM_STATE, LANES), 1)
        for s in range(CHUNK):
            c_s = (ccr_ref[s + 1], -cci_ref[s + 1])
            for qq in range(OCT // 2):
                for part in range(2):
                    for w in range(2):
                        r0 = qq * MXU_DIM + part * LANES + w * SSM_STATE
                        cs_ref[r0:r0 + SSM_STATE, s * LANES:(s + 1) * LANES] = jnp.where(
                            g_col == 2 * qq + w, c_s[part], 0.0).astype(BF16)

    def slab(ref, t, rows):
        return ref[pl.ds(t, rows, stride=CHUNK), :]

    for t in range(CHUNK):
        ub_ref[t // 2, :, (t % 2) * LANES:(t % 2 + 1) * LANES] = slab(u_ref, t, n_chunks).astype(BF16)

    z = jnp.dot(ub_ref[0], bs_ref[:MXU_DIM, :], preferred_element_type=F32)
    for tt in range(1, CHUNK // 2):
        z = z + jnp.dot(ub_ref[tt], bs_ref[tt * MXU_DIM:(tt + 1) * MXU_DIM, :], preferred_element_type=F32)
    um = jnp.concatenate([jnp.broadcast_to(um_ref[t:t + 1, :], (SUBLANES, LANES)) for t in range(CHUNK)], axis=1)
    zm = jnp.dot(um.astype(BF16), bs_ref[...], preferred_element_type=F32)[0:1, :]
    for ct in range(n_tiles):
        cols = slice(ct * LANES, (ct + 1) * LANES)
        buf_a[ct, SCAN_PAD:, :] = z[:, cols]
        for buf, pad in ((buf_a, SCAN_PAD), (buf_b, SCAN_PAD), (cb_a, blk_pad), (cb_b, blk_pad)):
            buf[ct, pad - 1:pad, :] = zm[:, cols]

    def within_chunk(ss):
        acc = jnp.dot(ub_ref[0], tz_ref[ss], preferred_element_type=F32)
        for tt in range(1, ss + 1):
            acc = acc + jnp.dot(ub_ref[tt], tz_ref[ss - tt], preferred_element_type=F32)
        yt_ref[ss] = acc

    assert LOCAL_STEPS + blk_steps >= CHUNK // 2 and (1 << LOCAL_STEPS) == SUBLANES
    todo = list(range(CHUNK // 2 - 1, -1, -1))
    blocked = (n_blocks, SUBLANES, LANES)

    def lanes_of(pair):
        return slice(pair * LANES, (pair + 1) * LANES)

    def combine(x, p, ar, ai):
        return x[0] + ar * p[0] - ai * p[1], x[1] + ar * p[1] + ai * p[0]

    src, dst = buf_a, buf_b
    for step in range(LOCAL_STEPS):
        shift = 1 << step
        within_chunk(todo.pop(0))
        for pair in range(OCT // 2):
            ar, ai = (loc_ref[0, step, part, :, lanes_of(pair)] for part in range(2))
            x = [src[2 * pair + part, SCAN_PAD:, :].reshape(blocked) for part in range(2)]
            p = [src[2 * pair + part, SCAN_PAD - shift:SCAN_PAD - shift + n_chunks, :].reshape(blocked)
                 for part in range(2)]
            for part, val in enumerate(combine(x, p, ar, ai)):
                dst[2 * pair + part, SCAN_PAD:, :] = val.reshape(n_chunks, LANES)
        src, dst = dst, src

    for ct in range(n_tiles):
        cb_a[ct, blk_pad:, :] = src[ct, pl.ds(SCAN_PAD + SUBLANES - 1, n_blocks, stride=SUBLANES), :]
    csrc, cdst = cb_a, cb_b
    for step in range(blk_steps):
        shift = 1 << step
        if todo:
            within_chunk(todo.pop(0))
        for pair in range(OCT // 2):
            ar = blk_ref[0, step:step + 1, lanes_of(pair)]
            ai = blk_ref[0, blk_steps + step:blk_steps + step + 1, lanes_of(pair)]
            x = [csrc[2 * pair + part, blk_pad:, :] for part in range(2)]
            p = [csrc[2 * pair + part, blk_pad - shift:blk_pad - shift + n_blocks, :] for part in range(2)]
            for part, val in enumerate(combine(x, p, ar, ai)):
                cdst[2 * pair + part, blk_pad:, :] = val
        csrc, cdst = cdst, csrc

    for pair in range(OCT // 2):
        for part in range(2):
            entering = csrc[2 * pair + part, blk_pad - 1:blk_pad - 1 + n_blocks, :]
            for row in range(SUBLANES):
                rep_ref[pair, part, pl.ds(row, n_blocks, stride=SUBLANES), :] = entering
        ar, ai = (loc_ref[0, LOCAL_STEPS, part, :, lanes_of(pair)] for part in range(2))
        x = [src[2 * pair + part, SCAN_PAD:, :].reshape(blocked) for part in range(2)]
        p = [rep_ref[pair, part].reshape(blocked) for part in range(2)]
        for part, val in enumerate(combine(x, p, ar, ai)):
            dst[2 * pair + part, SCAN_PAD:, :] = val.reshape(n_chunks, LANES)

    start_state = jnp.concatenate(
        [dst[ct, SCAN_PAD - 1:SCAN_PAD - 1 + n_chunks, :] for ct in range(n_tiles)], axis=1).astype(BF16)
    for ss in range(CHUNK // 2):
        cols = slice(ss * MXU_DIM, (ss + 1) * MXU_DIM)
        acc = yt_ref[ss] + jnp.dot(start_state, cs_ref[:, cols], preferred_element_type=F32)
        for half in range(2):
            t = 2 * ss + half
            y = acc[:, half * LANES:(half + 1) * LANES] + d_ref[0] * slab(u_ref, t, n_chunks)
            y_ref[pl.ds(t, n_chunks, stride=CHUNK), :] = 0.5 * y * (1.0 + lax.erf(y * (1.0 / math.sqrt(2.0))))


def _ssm(u, u_meta, bb2, bc, cc_re, cc_im, loc, blk, dp, bsz, seq):
    n_chunks = seq // CHUNK
    n_blocks = n_chunks // SUBLANES
    assert n_blocks & (n_blocks - 1) == 0 and N_META == CHUNK
    octet = lambda o, b: (o, 0, 0)
    n_tiles = OCT_STATE // LANES
    scan_buf = pltpu.VMEM((n_tiles, SCAN_PAD + n_chunks, LANES), F32)
    block_buf = pltpu.VMEM((n_tiles, n_blocks // 2 + n_blocks, LANES), F32)
    return pl.pallas_call(
        functools.partial(_ssm_kernel, n_chunks=n_chunks),
        grid=(N_OCT, bsz),
        in_specs=[
            pl.BlockSpec((seq, LANES), lambda o, b: (b, o)),
            pl.BlockSpec((N_META, LANES), lambda o, b: (0, o)),
            pl.BlockSpec((2, 1, LANES, 2 * SSM_STATE), lambda o, b: (0, o, 0, 0)),
            pl.BlockSpec((CHUNK, 1, LANES, 4 * SSM_STATE), lambda o, b: (0, o, 0, 0)),
            pl.BlockSpec((CHUNK + 1, SSM_STATE, LANES), lambda o, b: (0, 0, o)),
            pl.BlockSpec((CHUNK + 1, SSM_STATE, LANES), lambda o, b: (0, 0, o)),
            pl.BlockSpec((1,) + loc.shape[1:], lambda o, b: (o, 0, 0, 0, 0)),
            pl.BlockSpec((1,) + blk.shape[1:], octet),
            pl.BlockSpec((1, 1, LANES), octet),
        ],
        out_specs=pl.BlockSpec((seq, LANES), lambda o, b: (b, o)),
        out_shape=jax.ShapeDtypeStruct((bsz * seq, SSM_WIDTH), F32),
        scratch_shapes=[
            pltpu.VMEM((CHUNK // 2, MXU_DIM, MXU_DIM), BF16),
            pltpu.VMEM((OCT_IN, OCT_STATE), BF16),
            pltpu.VMEM((OCT_STATE, OCT_IN), BF16),
            pltpu.VMEM((CHUNK // 2, n_chunks, MXU_DIM), BF16), scan_buf, scan_buf, block_buf, block_buf,
            pltpu.VMEM((OCT // 2, 2, n_chunks, LANES), F32),
            pltpu.VMEM((CHUNK // 2, n_chunks, MXU_DIM), F32)],
        compiler_params=pltpu.CompilerParams(
            dimension_semantics=("arbitrary", "arbitrary"), vmem_limit_bytes=VMEM_LIMIT),
        name="s5_chunked",
    )(u, u_meta, bb2, bc, cc_re, cc_im, loc, blk, dp)


def _ssm_weights(a_re, a_im, log_dt, b_re, b_im, c_re, c_im, d_skip, n_chunks):
    dt = jnp.exp(log_dt)[:, None]
    lam_re, lam_im = a_re * dt, a_im * dt

    def power(n, transposed=False):
        n = n[:, None, None]
        l_re, l_im = (lam_re.T, lam_im.T) if transposed else (lam_re, lam_im)
        mag = jnp.exp(n * l_re)
        return mag * jnp.cos(n * l_im), mag * jnp.sin(n * l_im)

    ab_re, ab_im = jnp.exp(lam_re) * jnp.cos(lam_im), jnp.exp(lam_re) * jnp.sin(lam_im)
    den = a_re * a_re + a_im * a_im
    nr, ni = ab_re - 1.0, ab_im
    f_re = (nr * a_re + ni * a_im) / den
    f_im = (ni * a_re - nr * a_im) / den
    bb_re = f_re[..., None] * b_re - f_im[..., None] * b_im
    bb_im = f_re[..., None] * b_im + f_im[..., None] * b_re

    p_re, p_im = power(jnp.arange(CHUNK + 1, dtype=F32))

    ct_re, ct_im = c_re.transpose(2, 0, 1)[None], c_im.transpose(2, 0, 1)[None]
    pt_re, pt_im = (t[..., None] for t in power(jnp.arange(CHUNK + 1, dtype=F32), transposed=True))
    ca_re = ct_re * pt_re - ct_im * pt_im
    ca_im = ct_re * pt_im + ct_im * pt_re
    cc_re = ca_re.reshape(CHUNK + 1, SSM_STATE, SSM_WIDTH)
    cc_im = ca_im.reshape(CHUNK + 1, SSM_STATE, SSM_WIDTH)

    r_re, r_im = p_re[CHUNK - 1::-1][:, :, None, :], p_im[CHUNK - 1::-1][:, :, None, :]
    bt_re, bt_im = bb_re.transpose(0, 2, 1)[None], bb_im.transpose(0, 2, 1)[None]
    bb2 = jnp.stack([jnp.concatenate([t, t], axis=-1).reshape(N_OCT, LANES, 2 * SSM_STATE) for t in (bt_re, bt_im)])
    bs_re = r_re * bt_re - r_im * bt_im
    bs_im = r_re * bt_im + r_im * bt_re
    bc = jnp.concatenate([bs_re, bs_re, bs_im, bs_im], axis=-1).reshape(CHUNK, N_OCT, LANES, 4 * SSM_STATE)

    rows = jnp.arange(SUBLANES)
    strides = 2 ** jnp.arange(LOCAL_STEPS)
    in_block = (rows[None, :] >= strides[:, None])[:, :, None, None]
    blk_steps = (n_chunks // SUBLANES).bit_length() - 1

    def pack(m):
        m = m.reshape(m.shape[:-2] + (N_OCT, OCT_STATE // 2))
        return jnp.moveaxis(m, -2, 0)

    loc, blk = [], []
    for carry, block_stride in zip(power(CHUNK * (rows + 1.0)), power(CHUNK * SUBLANES * 2.0 ** jnp.arange(blk_steps))):
        steps = jnp.where(in_block, carry[strides - 1][:, None], 0.0)
        loc.append(pack(jnp.concatenate([steps, carry[None]], axis=0)))
        blk.append(pack(block_stride))
    loc = jnp.stack(loc, axis=2)
    blk = jnp.concatenate(blk, axis=1)

    dp = d_skip.reshape(N_OCT, 1, LANES)
    return bb2, bc, cc_re, cc_im, loc, blk, dp


def _out_ffn_kernel(x_ref, o_ref, y_ref, wglu_ref, bglu_ref, sg_ref, wout_ref, pmg_ref, pfg_ref,
                    wg_ref, wu_ref, wd_ref, pog_ref, out_ref):
    y = y_ref[...]
    gate = jnp.dot(y.astype(BF16), wglu_ref[...], preferred_element_type=F32) + bglu_ref[...]
    y = _rms(y * jax.nn.sigmoid(gate), sg_ref[...]).astype(BF16)
    mix = (jnp.dot(o_ref[...], wout_ref[:ATTN_WIDTH, :], preferred_element_type=F32)
           + jnp.dot(y, wout_ref[ATTN_WIDTH:, :], preferred_element_type=F32))
    h1 = x_ref[...] + _rms(mix, pmg_ref[...])
    h2 = _rms(h1, pfg_ref[...]).astype(BF16)
    g = jnp.dot(h2, wg_ref[...], preferred_element_type=F32)
    up = jnp.dot(h2, wu_ref[...], preferred_element_type=F32)
    f = (g * jax.nn.sigmoid(g) * up).astype(BF16)
    f = jnp.dot(f, wd_ref[...], preferred_element_type=F32)
    out_ref[...] = h1 + _rms(f, pog_ref[...])


def _out_ffn(x2d, o, y, wglu, bglu, sg, wout, pmg, pfg, wg, wu, wd, pog, tm):
    n = x2d.shape[0]
    row = lambda i: (i, 0)
    const = lambda i: (0, 0)

    def resident(shape):
        return pl.BlockSpec(shape, const, pipeline_mode=pl.Buffered(1))

    return pl.pallas_call(
        _out_ffn_kernel,
        grid=(n // tm,),
        in_specs=[
            pl.BlockSpec((tm, D_MODEL), row),
            pl.BlockSpec((tm, ATTN_WIDTH), row),
            pl.BlockSpec((tm, SSM_WIDTH), row),
            resident((SSM_WIDTH, SSM_WIDTH)),
            resident((1, SSM_WIDTH)),
            resident((1, SSM_WIDTH)),
            resident((D_MODEL, D_MODEL)),
            resident((1, D_MODEL)),
            resident((1, D_MODEL)),
            resident((D_MODEL, D_FF)),
            resident((D_MODEL, D_FF)),
            resident((D_FF, D_MODEL)),
            resident((1, D_MODEL)),
        ],
        out_specs=pl.BlockSpec((tm, D_MODEL), row),
        out_shape=jax.ShapeDtypeStruct((n, D_MODEL), F32),
        compiler_params=pltpu.CompilerParams(
            dimension_semantics=("arbitrary",), vmem_limit_bytes=VMEM_LIMIT),
        name="out_ffn",
    )(x2d, o, y, wglu, bglu, sg, wout, pmg, pfg, wg, wu, wd, pog)


def _rope_angles(pos):
    d = jnp.arange(LANES) % QK_DIM
    inv_freq = ROPE_THETA ** (-(2 * (d % (ROT_DIM // 2))).astype(F32) / ROT_DIM)
    ang = pos[:, None] * jnp.where(d < ROT_DIM, inv_freq, 0.0)[None, :]
    return jnp.cos(ang), jnp.sin(ang)


def _scale_w_in(w):
    scale = math.log2(math.e) / math.sqrt(QK_DIM)
    col_scale = jnp.where(jnp.arange(w.shape[1]) < ATTN_WIDTH, scale, 1.0).astype(F32)
    return (w * col_scale[None, :]).astype(BF16)


def kernel(x, meta, pre_mix_g, w_in, lambda_q1, lambda_k1, lambda_q2, lambda_k2, subln_g, a_re, a_im, log_dt,
           b_re, b_im, c_re, c_im, d_skip, w_glu, b_glu, ssm_out_g, w_out, post_mix_g, pre_ffn_g, w_gate,
           w_up, w_down, post_ffn_g):
    bsz, seq, _ = x.shape
    n = bsz * seq
    x2d = x.reshape(n, D_MODEL)
    row = lambda t: t[0].reshape(1, -1)

    w_in_p = _scale_w_in(w_in[0])
    g0 = row(pre_mix_g)
    q, k, v, u = _in_proj(x2d, g0, w_in_p, N_META, seq, ROW_TILE)
    _, k_m, v_m, u_m = _in_proj(meta, g0, w_in_p, 0, N_META, N_META)

    pad_meta = lambda t: jnp.pad(t, ((0, LANES - N_META), (0, 0)))
    o = _attention(q, k, v, pad_meta(k_m), pad_meta(v_m), lambda_q1, lambda_k1, lambda_q2, lambda_k2,
                   row(subln_g), bsz, seq)

    ssm_w = _ssm_weights(a_re[0], a_im[0], log_dt[0], b_re[0], b_im[0], c_re[0], c_im[0], d_skip[0],
                         seq // CHUNK)
    y = _ssm(u, u_m, *ssm_w, bsz, seq)

    out = _out_ffn(x2d, o, y, w_glu[0].astype(BF16), row(b_glu), row(ssm_out_g), w_out[0].astype(BF16),
                   row(post_mix_g), row(pre_ffn_g), w_gate[0].astype(BF16), w_up[0].astype(BF16),
                   w_down[0].astype(BF16), row(post_ffn_g), ROW_TILE)
    return out.reshape(bsz, seq, D_MODEL)
```

```python
import functools
import math

import jax
import jax.numpy as jnp
from jax import lax
from jax.experimental import pallas as pl
from jax.experimental.pallas import tpu as pltpu

D_MODEL = 1024
N_META = 16
N_HEADS = 4
QK_DIM = 64
V_DIM = 128
ROT_DIM = 16
ROPE_THETA = 500000.0
SSM_GROUP = 16
N_GROUPS = 32
SSM_STATE = 64
SSM_WIDTH = 512
ATTN_WIDTH = 512
D_FF = 2816
EPS = 1e-6
LAM_INIT = 0.8 - 0.6 * math.exp(-0.3 * 0)

LANES = 128
SUBLANES = 8
MXU_DIM = 256
CHUNK = 16
OCT = LANES // SSM_GROUP
N_OCT = N_GROUPS // OCT
OCT_IN = CHUNK * LANES
OCT_STATE = OCT * 2 * SSM_STATE
SCAN_PAD = SUBLANES
LOCAL_STEPS = 3

ROW_TILE = 512
Q_TILE = 512
KV_TILE = 512
ATTN_UNROLL = 8
VMEM_LIMIT = 56 * 1024 * 1024

F32 = jnp.float32
BF16 = jnp.bfloat16


def _rms(x, g):
    return x * lax.rsqrt(jnp.mean(x * x, axis=-1, keepdims=True) + EPS) * g


def _in_proj_kernel(x_ref, g_ref, w_ref, cr_ref, sr_ref, cb_ref, sb_ref, q_ref, k_ref, v_ref, u_ref):
    h = _rms(x_ref[...], g_ref[...]).astype(BF16)
    proj = jnp.dot(h, w_ref[...], preferred_element_type=F32)
    cos = cb_ref[0] * cr_ref[...] - sb_ref[0] * sr_ref[...]
    sin = sb_ref[0] * cr_ref[...] + cb_ref[0] * sr_ref[...]
    d = lax.broadcasted_iota(jnp.int32, (1, LANES), 1) % QK_DIM
    sa = sin * jnp.where(d < ROT_DIM // 2, -1.0, 0.0)
    sb = sin * jnp.where((d >= ROT_DIM // 2) & (d < ROT_DIM), 1.0, 0.0)

    def rope(t):
        return (t * cos + pltpu.roll(t, LANES - ROT_DIM // 2, axis=1) * sa
                + pltpu.roll(t, ROT_DIM // 2, axis=1) * sb)

    for c in range(ATTN_WIDTH // LANES):
        sl = slice(c * LANES, (c + 1) * LANES)
        q_ref[:, sl] = rope(proj[:, c * LANES:(c + 1) * LANES]).astype(BF16)
        k_ref[:, sl] = rope(proj[:, ATTN_WIDTH + c * LANES:ATTN_WIDTH + (c + 1) * LANES]).astype(BF16)
    v_ref[...] = proj[:, 2 * ATTN_WIDTH:3 * ATTN_WIDTH].astype(BF16)
    u_ref[...] = proj[:, 3 * ATTN_WIDTH:]


def _in_proj(x2d, g, w, first_pos, seq, tm):
    n = x2d.shape[0]
    tiles_per_seq = seq // tm
    cos_r, sin_r = _rope_angles(jnp.arange(tm, dtype=F32))
    cos_b, sin_b = _rope_angles(first_pos + tm * jnp.arange(tiles_per_seq, dtype=F32))
    row = lambda i: (i, 0)
    const = lambda i: (0, 0)
    base = pl.BlockSpec((1, 1, LANES), lambda i: (i % tiles_per_seq, 0, 0))
    return pl.pallas_call(
        _in_proj_kernel,
        grid=(n // tm,),
        in_specs=[
            pl.BlockSpec((tm, D_MODEL), row),
            pl.BlockSpec((1, D_MODEL), const),
            pl.BlockSpec((D_MODEL, 4 * ATTN_WIDTH), const),
            pl.BlockSpec((tm, LANES), const),
            pl.BlockSpec((tm, LANES), const),
            base, base,
        ],
        out_specs=[pl.BlockSpec((tm, ATTN_WIDTH), row)] * 4,
        out_shape=[jax.ShapeDtypeStruct((n, ATTN_WIDTH), BF16)] * 3
        + [jax.ShapeDtypeStruct((n, SSM_WIDTH), F32)],
        compiler_params=pltpu.CompilerParams(
            dimension_semantics=("arbitrary",), vmem_limit_bytes=VMEM_LIMIT),
        name="in_proj",
    )(x2d, g, w, cos_r, sin_r, cos_b[:, None, :], sin_b[:, None, :])


def _attn_kernel(*refs):
    n_tiles = refs[0].shape[0] // Q_TILE

    def q_tile(qi, carry):
        _attn_q_tile(qi, *refs)
        return carry

    lax.fori_loop(0, n_tiles, q_tile, 0)


def _attn_q_tile(qi, q1_ref, q2_ref, k1_ref, k2_ref, v_ref, km1_ref, km2_ref, vm_ref,
                 lq1_ref, lk1_ref, lq2_ref, lk2_ref, sg_ref,
                 o_ref, m_ref, l_ref, acc_ref, pa_ref, pb_ref):
    q_rows = pl.ds(pl.multiple_of(qi * Q_TILE, Q_TILE), Q_TILE)
    own = lax.broadcasted_iota(jnp.int32, (Q_TILE, LANES), 1) // QK_DIM == pl.program_id(1) % 2
    qs = tuple(jnp.where(own, r[q_rows, :], jnp.zeros((Q_TILE, LANES), BF16)) for r in (q1_ref, q2_ref))
    k_refs, km_refs = (k1_ref, k2_ref), (km1_ref, km2_ref)
    nt = (((1,), (1,)), ((), ()))
    wide = (Q_TILE, LANES)

    def block(k, mask, p_out, prev):
        n_col = k[0].shape[0] // LANES
        for i in range(2):
            s = lax.dot_general(qs[i], k[i], nt, preferred_element_type=F32)
            if mask is not None:
                s = jnp.where(mask, s, -jnp.inf)
            cols = [s[:, c * LANES:(c + 1) * LANES] for c in range(n_col)]
            m_tile = functools.reduce(jnp.maximum, cols)
            m_new = jnp.broadcast_to(jnp.max(m_tile, axis=1, keepdims=True), wide)
            if prev is not None:
                m_old = m_ref[i]
                m_new = jnp.maximum(m_old, m_new)
                alpha = jnp.exp2(m_old - m_new)
            m_ref[i] = m_new
            ps = [jnp.exp2(c - m_new) for c in cols]
            l_tile = functools.reduce(jnp.add, ps)
            l_ref[i] = l_tile if prev is None else alpha * l_ref[i] + l_tile
            for c in range(n_col):
                p_out[i, :, c * LANES:(c + 1) * LANES] = ps[c].astype(BF16)
            if prev is not None:
                p_prev, v_prev = prev
                pv = jnp.dot(p_prev[i], v_prev, preferred_element_type=F32)
                acc_ref[i] = (acc_ref[i] + pv) * alpha

    def tile(ref, t):
        return ref[pl.ds(pl.multiple_of(t * KV_TILE, KV_TILE), KV_TILE), :]

    def ktile(t):
        return tuple(tile(r, t) for r in k_refs)

    def finish(pending, p_meta):
        meta_mask = lax.broadcasted_iota(jnp.int32, wide, 1) < N_META
        block(tuple(r[...] for r in km_refs), meta_mask, p_meta, pending)
        for i in range(2):
            acc_ref[i] = acc_ref[i] + jnp.dot(p_meta[i, :, :LANES], vm_ref[...], preferred_element_type=F32)

    acc_ref[...] = jnp.zeros(acc_ref.shape, F32)
    causal = (lax.broadcasted_iota(jnp.int32, (Q_TILE, KV_TILE), 1)
              <= lax.broadcasted_iota(jnp.int32, (Q_TILE, KV_TILE), 0))
    block(ktile(qi), causal, pa_ref, None)

    def run(t0, n):
        slots = (pa_ref, pb_ref)
        before = jnp.where(t0 == 0, qi, t0 - 1)
        for d in range(n):
            block(ktile(t0 + d), None, slots[(d + 1) % 2], (slots[d % 2], tile(v_ref, before)))
            before = t0 + d
        return before

    def unrolled(jj, carry):
        run(ATTN_UNROLL * jj, ATTN_UNROLL)
        return carry

    lax.fori_loop(0, qi // ATTN_UNROLL, unrolled, 0)
    done = (qi // ATTN_UNROLL) * ATTN_UNROLL

    def tail(left):
        before = run(done, left)
        slots = (pa_ref, pb_ref)
        finish((slots[left % 2], tile(v_ref, before)), slots[(left + 1) % 2])

    for left in range(ATTN_UNROLL):
        pl.when(qi - done == left)(functools.partial(tail, left))

    lam = (jnp.exp(jnp.sum(lq1_ref[...] * lk1_ref[...], axis=1, keepdims=True))
           - jnp.exp(jnp.sum(lq2_ref[...] * lk2_ref[...], axis=1, keepdims=True)) + LAM_INIT)
    l1 = jnp.sum(l_ref[0], axis=1, keepdims=True)
    l2 = jnp.sum(l_ref[1], axis=1, keepdims=True)
    o = acc_ref[0] / l1 - lam * (acc_ref[1] / l2)
    o_ref[q_rows, :] = (_rms(o, sg_ref[...]) * (1.0 - LAM_INIT)).astype(BF16)


def _attention(q, k, v, km, vm, lq1, lk1, lq2, lk2, sg, bsz, seq):
    assert Q_TILE == KV_TILE and seq % Q_TILE == 0
    pairs = N_HEADS // 2
    kvmap = lambda b, h: (b, h)
    map1 = pl.BlockSpec((seq, LANES), lambda b, h: (b, h // 2))
    map2 = pl.BlockSpec((seq, LANES), lambda b, h: (b, pairs + h // 2))
    meta1 = pl.BlockSpec((LANES, LANES), lambda b, h: (0, h // 2))
    meta2 = pl.BlockSpec((LANES, LANES), lambda b, h: (0, pairs + h // 2))
    const = lambda b, h: (0, 0)
    vec = pl.BlockSpec((1, QK_DIM), const)
    return pl.pallas_call(
        _attn_kernel,
        grid=(bsz, N_HEADS),
        in_specs=[
            map1, map2, map1, map2,
            pl.BlockSpec((seq, LANES), kvmap),
            meta1, meta2,
            pl.BlockSpec((LANES, LANES), lambda b, h: (0, h)),
            vec, vec, vec, vec,
            pl.BlockSpec((1, V_DIM), const),
        ],
        out_specs=pl.BlockSpec((seq, LANES), kvmap),
        out_shape=jax.ShapeDtypeStruct((bsz * seq, ATTN_WIDTH), BF16),
        scratch_shapes=[
            pltpu.VMEM((2, Q_TILE, LANES), F32),
            pltpu.VMEM((2, Q_TILE, LANES), F32),
            pltpu.VMEM((2, Q_TILE, V_DIM), F32),
            pltpu.VMEM((2, Q_TILE, KV_TILE), BF16),
            pltpu.VMEM((2, Q_TILE, KV_TILE), BF16),
        ],
        compiler_params=pltpu.CompilerParams(
            dimension_semantics=("arbitrary", "arbitrary"), vmem_limit_bytes=VMEM_LIMIT),
        name="diff_attention",
    )(q, q, k, k, v, km, km, vm, lq1, lk1, lq2, lk2, sg)


def _ssm_kernel(u_ref, um_ref, bb_ref, bc_ref, ccr_ref, cci_ref, loc_ref, blk_ref, d_ref, y_ref,
                tz_ref, bs_ref, cs_ref, ub_ref, buf_a, buf_b, cb_a, cb_b, rep_ref, yt_ref, *, n_chunks):
    n_tiles = OCT_STATE // LANES
    n_blocks = n_chunks // SUBLANES
    blk_steps = n_blocks.bit_length() - 1
    blk_pad = cb_a.shape[1] - n_blocks

    @pl.when((pl.program_id(0) == 0) & (pl.program_id(1) == 0))
    def _():
        for buf, pad in ((buf_a, SCAN_PAD), (buf_b, SCAN_PAD), (cb_a, blk_pad), (cb_b, blk_pad)):
            buf[:, :pad, :] = jnp.zeros((n_tiles, pad, LANES), F32)

    @pl.when(pl.program_id(1) == 0)
    def _():
        def group_of(shape, axis):
            return (lax.broadcasted_iota(jnp.int32, shape, axis) // SSM_GROUP) % OCT

        def split(x):
            head = x.astype(BF16)
            return head, (x - head.astype(F32)).astype(BF16)

        wide_shape = (LANES, OCT * SSM_STATE)
        own_cols = group_of(wide_shape, 0) == lax.broadcasted_iota(jnp.int32, wide_shape, 1) // SSM_STATE
        bbar = [split(jnp.where(own_cols, jnp.concatenate([bb_ref[part, 0]] * (OCT // 2), axis=1), 0.0))
                for part in range(2)]
        lag_blk = []
        for n in range(0, CHUNK, 2):
            terms = []
            for part, c_ref in enumerate((ccr_ref, cci_ref)):
                c_n = jnp.concatenate([c_ref[n], c_ref[n + 1]], axis=1)
                g_lane = group_of(c_n.shape, 1)
                ca_head, ca_rest = split(jnp.concatenate(
                    [jnp.where(g_lane == g, c_n, 0.0) for g in range(OCT)], axis=0))
                b_head, b_rest = bbar[part]
                terms.append(jnp.dot(b_head, ca_head, preferred_element_type=F32)
                             + jnp.dot(b_head, ca_rest, preferred_element_type=F32)
                             + jnp.dot(b_rest, ca_head, preferred_element_type=F32))
            both = (terms[0] - terms[1]).astype(BF16)
            lag_blk += [both[:, :LANES], both[:, LANES:]]
        for m in range(CHUNK // 2):
            tz_ref[m, :LANES, :LANES] = lag_blk[2 * m]
            tz_ref[m, :LANES, LANES:] = lag_blk[2 * m + 1]
            tz_ref[m, LANES:, :LANES] = lag_blk[2 * m - 1] if m else jnp.zeros((LANES, LANES), BF16)
            tz_ref[m, LANES:, LANES:] = lag_blk[2 * m]
        bc = bc_ref[:, 0].reshape(OCT_IN, 4 * SSM_STATE)
        g_row = group_of(bc.shape, 0)
        w_col = (lax.broadcasted_iota(jnp.int32, bc.shape, 1) // SSM_STATE) % 2
        for qq in range(OCT // 2):
            bs_ref[:, qq * MXU_DIM:(qq + 1) * MXU_DIM] = jnp.where(g_row == 2 * qq + w_col, bc, 0.0).astype(BF16)
        g_col = group_of((SSM_STATE, LANES), 1)
        for s in range(CHUNK):
            c_s = (ccr_ref[s + 1], -cci_ref[s + 1])
            for qq in range(OCT // 2):
                for part in range(2):
                    for w in range(2):
                        r0 = qq * MXU_DIM + part * LANES + w * SSM_STATE
                        cs_ref[r0:r0 + SSM_STATE, s * LANES:(s + 1) * LANES] = jnp.where(
                            g_col == 2 * qq + w, c_s[part], 0.0).astype(BF16)

    def slab(ref, t, rows):
        return ref[pl.ds(t, rows, stride=CHUNK), :]

    for t in range(CHUNK):
        ub_ref[t // 2, :, (t % 2) * LANES:(t % 2 + 1) * LANES] = slab(u_ref, t, n_chunks).astype(BF16)

    z = jnp.dot(ub_ref[0], bs_ref[:MXU_DIM, :], preferred_element_type=F32)
    for tt in range(1, CHUNK // 2):
        z = z + jnp.dot(ub_ref[tt], bs_ref[tt * MXU_DIM:(tt + 1) * MXU_DIM, :], preferred_element_type=F32)
    um = jnp.concatenate([jnp.broadcast_to(um_ref[t:t + 1, :], (SUBLANES, LANES)) for t in range(CHUNK)], axis=1)
    zm = jnp.dot(um.astype(BF16), bs_ref[...], preferred_element_type=F32)[0:1, :]
    for ct in range(n_tiles):
        cols = slice(ct * LANES, (ct + 1) * LANES)
        buf_a[ct, SCAN_PAD:, :] = z[:, cols]
        for buf, pad in ((buf_a, SCAN_PAD), (buf_b, SCAN_PAD), (cb_a, blk_pad), (cb_b, blk_pad)):
            buf[ct, pad - 1:pad, :] = zm[:, cols]

    def within_chunk(ss):
        acc = jnp.dot(ub_ref[0], tz_ref[ss], preferred_element_type=F32)
        for tt in range(1, ss + 1):
            acc = acc + jnp.dot(ub_ref[tt], tz_ref[ss - tt], preferred_element_type=F32)
        yt_ref[ss] = acc

    assert LOCAL_STEPS + blk_steps >= CHUNK // 2 and (1 << LOCAL_STEPS) == SUBLANES
    todo = list(range(CHUNK // 2 - 1, -1, -1))
    blocked = (n_blocks, SUBLANES, LANES)

    def lanes_of(pair):
        return slice(pair * LANES, (pair + 1) * LANES)

    def combine(x, p, ar, ai):
        return x[0] + ar * p[0] - ai * p[1], x[1] + ar * p[1] + ai * p[0]

    src, dst = buf_a, buf_b
    for step in range(LOCAL_STEPS):
        shift = 1 << step
        within_chunk(todo.pop(0))
        for pair in range(OCT // 2):
            ar, ai = (loc_ref[0, step, part, :, lanes_of(pair)] for part in range(2))
            x = [src[2 * pair + part, SCAN_PAD:, :].reshape(blocked) for part in range(2)]
            p = [src[2 * pair + part, SCAN_PAD - shift:SCAN_PAD - shift + n_chunks, :].reshape(blocked)
                 for part in range(2)]
            for part, val in enumerate(combine(x, p, ar, ai)):
                dst[2 * pair + part, SCAN_PAD:, :] = val.reshape(n_chunks, LANES)
        src, dst = dst, src

    for ct in range(n_tiles):
        cb_a[ct, blk_pad:, :] = src[ct, pl.ds(SCAN_PAD + SUBLANES - 1, n_blocks, stride=SUBLANES), :]
    csrc, cdst = cb_a, cb_b
    for step in range(blk_steps):
        shift = 1 << step
        if todo:
            within_chunk(todo.pop(0))
        for pair in range(OCT // 2):
            ar = blk_ref[0, step:step + 1, lanes_of(pair)]
            ai = blk_ref[0, blk_steps + step:blk_steps + step + 1, lanes_of(pair)]
            x = [csrc[2 * pair + part, blk_pad:, :] for part in range(2)]
            p = [csrc[2 * pair + part, blk_pad - shift:blk_pad - shift + n_blocks, :] for part in range(2)]
            for part, val in enumerate(combine(x, p, ar, ai)):
                cdst[2 * pair + part, blk_pad:, :] = val
        csrc, cdst = cdst, csrc

    for pair in range(OCT // 2):
        for part in range(2):
            entering = csrc[2 * pair + part, blk_pad - 1:blk_pad - 1 + n_blocks, :]
            for row in range(SUBLANES):
                rep_ref[pair, part, pl.ds(row, n_blocks, stride=SUBLANES), :] = entering
        ar, ai = (loc_ref[0, LOCAL_STEPS, part, :, lanes_of(pair)] for part in range(2))
        x = [src[2 * pair + part, SCAN_PAD:, :].reshape(blocked) for part in range(2)]
        p = [rep_ref[pair, part].reshape(blocked) for part in range(2)]
        for part, val in enumerate(combine(x, p, ar, ai)):
            dst[2 * pair + part, SCAN_PAD:, :] = val.reshape(n_chunks, LANES)

    start_state = jnp.concatenate(
        [dst[ct, SCAN_PAD - 1:SCAN_PAD - 1 + n_chunks, :] for ct in range(n_tiles)], axis=1).astype(BF16)
    for ss in range(CHUNK // 2):
        cols = slice(ss * MXU_DIM, (ss + 1) * MXU_DIM)
        acc = yt_ref[ss] + jnp.dot(start_state, cs_ref[:, cols], preferred_element_type=F32)
        for half in range(2):
            t = 2 * ss + half
            y = acc[:, half * LANES:(half + 1) * LANES] + d_ref[0] * slab(u_ref, t, n_chunks)
            y_ref[pl.ds(t, n_chunks, stride=CHUNK), :] = 0.5 * y * (1.0 + lax.erf(y * (1.0 / math.sqrt(2.0))))


def _ssm(u, u_meta, bb2, bc, cc_re, cc_im, loc, blk, dp, bsz, seq):
    n_chunks = seq // CHUNK
    n_blocks = n_chunks // SUBLANES
    assert n_blocks & (n_blocks - 1) == 0 and N_META == CHUNK
    octet = lambda o, b: (o, 0, 0)
    n_tiles = OCT_STATE // LANES
    scan_buf = pltpu.VMEM((n_tiles, SCAN_PAD + n_chunks, LANES), F32)
    block_buf = pltpu.VMEM((n_tiles, n_blocks // 2 + n_blocks, LANES), F32)
    return pl.pallas_call(
        functools.partial(_ssm_kernel, n_chunks=n_chunks),
        grid=(N_OCT, bsz),
        in_specs=[
            pl.BlockSpec((seq, LANES), lambda o, b: (b, o)),
            pl.BlockSpec((N_META, LANES), lambda o, b: (0, o)),
            pl.BlockSpec((2, 1, LANES, 2 * SSM_STATE), lambda o, b: (0, o, 0, 0)),
            pl.BlockSpec((CHUNK, 1, LANES, 4 * SSM_STATE), lambda o, b: (0, o, 0, 0)),
            pl.BlockSpec((CHUNK + 1, SSM_STATE, LANES), lambda o, b: (0, 0, o)),
            pl.BlockSpec((CHUNK + 1, SSM_STATE, LANES), lambda o, b: (0, 0, o)),
            pl.BlockSpec((1,) + loc.shape[1:], lambda o, b: (o, 0, 0, 0, 0)),
            pl.BlockSpec((1,) + blk.shape[1:], octet),
            pl.BlockSpec((1, 1, LANES), octet),
        ],
        out_specs=pl.BlockSpec((seq, LANES), lambda o, b: (b, o)),
        out_shape=jax.ShapeDtypeStruct((bsz * seq, SSM_WIDTH), F32),
        scratch_shapes=[
            pltpu.VMEM((CHUNK // 2, MXU_DIM, MXU_DIM), BF16),
            pltpu.VMEM((OCT_IN, OCT_STATE), BF16),
            pltpu.VMEM((OCT_STATE, OCT_IN), BF16),
            pltpu.VMEM((CHUNK // 2, n_chunks, MXU_DIM), BF16), scan_buf, scan_buf, block_buf, block_buf,
            pltpu.VMEM((OCT // 2, 2, n_chunks, LANES), F32),
            pltpu.VMEM((CHUNK // 2, n_chunks, MXU_DIM), F32)],
        compiler_params=pltpu.CompilerParams(
            dimension_semantics=("arbitrary", "arbitrary"), vmem_limit_bytes=VMEM_LIMIT),
        name="s5_chunked",
    )(u, u_meta, bb2, bc, cc_re, cc_im, loc, blk, dp)


def _ssm_weights(a_re, a_im, log_dt, b_re, b_im, c_re, c_im, d_skip, n_chunks):
    dt = jnp.exp(log_dt)[:, None]
    lam_re, lam_im = a_re * dt, a_im * dt

    def power(n, transposed=False):
        n = n[:, None, None]
        l_re, l_im = (lam_re.T, lam_im.T) if transposed else (lam_re, lam_im)
        mag = jnp.exp(n * l_re)
        return mag * jnp.cos(n * l_im), mag * jnp.sin(n * l_im)

    ab_re, ab_im = jnp.exp(lam_re) * jnp.cos(lam_im), jnp.exp(lam_re) * jnp.sin(lam_im)
    den = a_re * a_re + a_im * a_im
    nr, ni = ab_re - 1.0, ab_im
    f_re = (nr * a_re + ni * a_im) / den
    f_im = (ni * a_re - nr * a_im) / den
    bb_re = f_re[..., None] * b_re - f_im[..., None] * b_im
    bb_im = f_re[..., None] * b_im + f_im[..., None] * b_re

    p_re, p_im = power(jnp.arange(CHUNK + 1, dtype=F32))

    ct_re, ct_im = c_re.transpose(2, 0, 1)[None], c_im.transpose(2, 0, 1)[None]
    pt_re, pt_im = (t[..., None] for t in power(jnp.arange(CHUNK + 1, dtype=F32), transposed=True))
    ca_re = ct_re * pt_re - ct_im * pt_im
    ca_im = ct_re * pt_im + ct_im * pt_re
    cc_re = ca_re.reshape(CHUNK + 1, SSM_STATE, SSM_WIDTH)
    cc_im = ca_im.reshape(CHUNK + 1, SSM_STATE, SSM_WIDTH)

    r_re, r_im = p_re[CHUNK - 1::-1][:, :, None, :], p_im[CHUNK - 1::-1][:, :, None, :]
    bt_re, bt_im = bb_re.transpose(0, 2, 1)[None], bb_im.transpose(0, 2, 1)[None]
    bb2 = jnp.stack([jnp.concatenate([t, t], axis=-1).reshape(N_OCT, LANES, 2 * SSM_STATE) for t in (bt_re, bt_im)])
    bs_re = r_re * bt_re - r_im * bt_im
    bs_im = r_re * bt_im + r_im * bt_re
    bc = jnp.concatenate([bs_re, bs_re, bs_im, bs_im], axis=-1).reshape(CHUNK, N_OCT, LANES, 4 * SSM_STATE)

    rows = jnp.arange(SUBLANES)
    strides = 2 ** jnp.arange(LOCAL_STEPS)
    in_block = (rows[None, :] >= strides[:, None])[:, :, None, None]
    blk_steps = (n_chunks // SUBLANES).bit_length() - 1

    def pack(m):
        m = m.reshape(m.shape[:-2] + (N_OCT, OCT_STATE // 2))
        return jnp.moveaxis(m, -2, 0)

    loc, blk = [], []
    for carry, block_stride in zip(power(CHUNK * (rows + 1.0)), power(CHUNK * SUBLANES * 2.0 ** jnp.arange(blk_steps))):
        steps = jnp.where(in_block, carry[strides - 1][:, None], 0.0)
        loc.append(pack(jnp.concatenate([steps, carry[None]], axis=0)))
        blk.append(pack(block_stride))
    loc = jnp.stack(loc, axis=2)
    blk = jnp.concatenate(blk, axis=1)

    dp = d_skip.reshape(N_OCT, 1, LANES)
    return bb2, bc, cc_re, cc_im, loc, blk, dp


def _out_ffn_kernel(x_ref, o_ref, y_ref, wglu_ref, bglu_ref, sg_ref, wout_ref, pmg_ref, pfg_ref,
                    wg_ref, wu_ref, wd_ref, pog_ref, out_ref):
    y = y_ref[...]
    gate = jnp.dot(y.astype(BF16), wglu_ref[...], preferred_element_type=F32) + bglu_ref[...]
    y = _rms(y * jax.nn.sigmoid(gate), sg_ref[...]).astype(BF16)
    mix = (jnp.dot(o_ref[...], wout_ref[:ATTN_WIDTH, :], preferred_element_type=F32)
           + jnp.dot(y, wout_ref[ATTN_WIDTH:, :], preferred_element_type=F32))
    h1 = x_ref[...] + _rms(mix, pmg_ref[...])
    h2 = _rms(h1, pfg_ref[...]).astype(BF16)
    g = jnp.dot(h2, wg_ref[...], preferred_element_type=F32)
    up = jnp.dot(h2, wu_ref[...], preferred_element_type=F32)
    f = (g * jax.nn.sigmoid(g) * up).astype(BF16)
    f = jnp.dot(f, wd_ref[...], preferred_element_type=F32)
    out_ref[...] = h1 + _rms(f, pog_ref[...])


def _out_ffn(x2d, o, y, wglu, bglu, sg, wout, pmg, pfg, wg, wu, wd, pog, tm):
    n = x2d.shape[0]
    row = lambda i: (i, 0)
    const = lambda i: (0, 0)

    def resident(shape):
        return pl.BlockSpec(shape, const, pipeline_mode=pl.Buffered(1))

    return pl.pallas_call(
        _out_ffn_kernel,
        grid=(n // tm,),
        in_specs=[
            pl.BlockSpec((tm, D_MODEL), row),
            pl.BlockSpec((tm, ATTN_WIDTH), row),
            pl.BlockSpec((tm, SSM_WIDTH), row),
            resident((SSM_WIDTH, SSM_WIDTH)),
            resident((1, SSM_WIDTH)),
            resident((1, SSM_WIDTH)),
            resident((D_MODEL, D_MODEL)),
            resident((1, D_MODEL)),
            resident((1, D_MODEL)),
            resident((D_MODEL, D_FF)),
            resident((D_MODEL, D_FF)),
            resident((D_FF, D_MODEL)),
            resident((1, D_MODEL)),
        ],
        out_specs=pl.BlockSpec((tm, D_MODEL), row),
        out_shape=jax.ShapeDtypeStruct((n, D_MODEL), F32),
        compiler_params=pltpu.CompilerParams(
            dimension_semantics=("arbitrary",), vmem_limit_bytes=VMEM_LIMIT),
        name="out_ffn",
    )(x2d, o, y, wglu, bglu, sg, wout, pmg, pfg, wg, wu, wd, pog)


def _rope_angles(pos):
    d = jnp.arange(LANES) % QK_DIM
    inv_freq = ROPE_THETA ** (-(2 * (d % (ROT_DIM // 2))).astype(F32) / ROT_DIM)
    ang = pos[:, None] * jnp.where(d < ROT_DIM, inv_freq, 0.0)[None, :]
    return jnp.cos(ang), jnp.sin(ang)


def _scale_w_in(w):
    scale = math.log2(math.e) / math.sqrt(QK_DIM)
    col_scale = jnp.where(jnp.arange(w.shape[1]) < ATTN_WIDTH, scale, 1.0).astype(F32)
    return (w * col_scale[None, :]).astype(BF16)


def kernel(x, meta, pre_mix_g, w_in, lambda_q1, lambda_k1, lambda_q2, lambda_k2, subln_g, a_re, a_im, log_dt,
           b_re, b_im, c_re, c_im, d_skip, w_glu, b_glu, ssm_out_g, w_out, post_mix_g, pre_ffn_g, w_gate,
           w_up, w_down, post_ffn_g):
    bsz, seq, _ = x.shape
    n = bsz * seq
    x2d = x.reshape(n, D_MODEL)
    row = lambda t: t[0].reshape(1, -1)

    w_in_p = _scale_w_in(w_in[0])
    g0 = row(pre_mix_g)
    q, k, v, u = _in_proj(x2d, g0, w_in_p, N_META, seq, ROW_TILE)
    _, k_m, v_m, u_m = _in_proj(meta, g0, w_in_p, 0, N_META, N_META)

    pad_meta = lambda t: jnp.pad(t, ((0, LANES - N_META), (0, 0)))
    o = _attention(q, k, v, pad_meta(k_m), pad_meta(v_m), lambda_q1, lambda_k1, lambda_q2, lambda_k2,
                   row(subln_g), bsz, seq)

    ssm_w = _ssm_weights(a_re[0], a_im[0], log_dt[0], b_re[0], b_im[0], c_re[0], c_im[0], d_skip[0],
                         seq // CHUNK)
    y = _ssm(u, u_m, *ssm_w, bsz, seq)

    out = _out_ffn(x2d, o, y, w_glu[0].astype(BF16), row(b_glu), row(ssm_out_g), w_out[0].astype(BF16),
                   row(post_mix_g), row(pre_ffn_g), w_gate[0].astype(BF16), w_up[0].astype(BF16),
                   w_down[0].astype(BF16), row(post_ffn_g), ROW_TILE)
    return out.reshape(bsz, seq, D_MODEL)
```

```python
import functools
import math

import jax
import jax.numpy as jnp
from jax import lax
from jax.experimental import pallas as pl
from jax.experimental.pallas import tpu as pltpu

D_MODEL = 1024
N_META = 16
N_HEADS = 4
QK_DIM = 64
V_DIM = 128
ROT_DIM = 16
ROPE_THETA = 500000.0
SSM_GROUP = 16
N_GROUPS = 32
SSM_STATE = 64
SSM_WIDTH = 512
ATTN_WIDTH = 512
D_FF = 2816
EPS = 1e-6
LAM_INIT = 0.8 - 0.6 * math.exp(-0.3 * 0)

LANES = 128
SUBLANES = 8
MXU_DIM = 256
CHUNK = 16
OCT = LANES // SSM_GROUP
N_OCT = N_GROUPS // OCT
OCT_IN = CHUNK * LANES
OCT_STATE = OCT * 2 * SSM_STATE
SCAN_PAD = SUBLANES
LOCAL_STEPS = 3

ROW_TILE = 512
Q_TILE = 512
KV_TILE = 512
ATTN_UNROLL = 4
VMEM_LIMIT = 56 * 1024 * 1024

F32 = jnp.float32
BF16 = jnp.bfloat16


def _rms(x, g):
    return x * lax.rsqrt(jnp.mean(x * x, axis=-1, keepdims=True) + EPS) * g


def _in_proj_kernel(x_ref, g_ref, w_ref, cr_ref, sr_ref, cb_ref, sb_ref, q_ref, k_ref, v_ref, u_ref):
    h = _rms(x_ref[...], g_ref[...]).astype(BF16)
    proj = jnp.dot(h, w_ref[...], preferred_element_type=F32)
    cos = cb_ref[0] * cr_ref[...] - sb_ref[0] * sr_ref[...]
    sin = sb_ref[0] * cr_ref[...] + cb_ref[0] * sr_ref[...]
    d = lax.broadcasted_iota(jnp.int32, (1, LANES), 1) % QK_DIM
    sa = sin * jnp.where(d < ROT_DIM // 2, -1.0, 0.0)
    sb = sin * jnp.where((d >= ROT_DIM // 2) & (d < ROT_DIM), 1.0, 0.0)

    def rope(t):
        return (t * cos + pltpu.roll(t, LANES - ROT_DIM // 2, axis=1) * sa
                + pltpu.roll(t, ROT_DIM // 2, axis=1) * sb)

    for c in range(ATTN_WIDTH // LANES):
        sl = slice(c * LANES, (c + 1) * LANES)
        q_ref[:, sl] = rope(proj[:, c * LANES:(c + 1) * LANES]).astype(BF16)
        k_ref[:, sl] = rope(proj[:, ATTN_WIDTH + c * LANES:ATTN_WIDTH + (c + 1) * LANES]).astype(BF16)
    v_ref[...] = proj[:, 2 * ATTN_WIDTH:3 * ATTN_WIDTH].astype(BF16)
    u_ref[...] = proj[:, 3 * ATTN_WIDTH:]


def _in_proj(x2d, g, w, first_pos, seq, tm):
    n = x2d.shape[0]
    tiles_per_seq = seq // tm
    cos_r, sin_r = _rope_angles(jnp.arange(tm, dtype=F32))
    cos_b, sin_b = _rope_angles(first_pos + tm * jnp.arange(tiles_per_seq, dtype=F32))
    row = lambda i: (i, 0)
    const = lambda i: (0, 0)
    base = pl.BlockSpec((1, 1, LANES), lambda i: (i % tiles_per_seq, 0, 0))
    return pl.pallas_call(
        _in_proj_kernel,
        grid=(n // tm,),
        in_specs=[
            pl.BlockSpec((tm, D_MODEL), row),
            pl.BlockSpec((1, D_MODEL), const),
            pl.BlockSpec((D_MODEL, 4 * ATTN_WIDTH), const),
            pl.BlockSpec((tm, LANES), const),
            pl.BlockSpec((tm, LANES), const),
            base, base,
        ],
        out_specs=[pl.BlockSpec((tm, ATTN_WIDTH), row)] * 4,
        out_shape=[jax.ShapeDtypeStruct((n, ATTN_WIDTH), BF16)] * 3
        + [jax.ShapeDtypeStruct((n, SSM_WIDTH), F32)],
        compiler_params=pltpu.CompilerParams(
            dimension_semantics=("arbitrary",), vmem_limit_bytes=VMEM_LIMIT),
        name="in_proj",
    )(x2d, g, w, cos_r, sin_r, cos_b[:, None, :], sin_b[:, None, :])


def _attn_kernel(*refs):
    n_pairs = refs[0].shape[0] // (2 * Q_TILE)

    def q_pair(sp, carry):
        _attn_q_pair(sp, *refs)
        return carry

    lax.fori_loop(0, n_pairs, q_pair, 0)


def _attn_q_pair(sp, q1_ref, q2_ref, k1_ref, k2_ref, v_ref, km1_ref, km2_ref, vm_ref,
                 lq1_ref, lk1_ref, lq2_ref, lk2_ref, sg_ref,
                 o_ref, m_ref, l_ref, acc_ref, p_ref):
    own = lax.broadcasted_iota(jnp.int32, (Q_TILE, LANES), 1) // QK_DIM == pl.program_id(1) % 2
    q_rows = [pl.ds(pl.multiple_of((2 * sp + c) * Q_TILE, Q_TILE), Q_TILE) for c in range(2)]
    qs = [tuple(jnp.where(own, r[rows, :], jnp.zeros((Q_TILE, LANES), BF16)) for r in (q1_ref, q2_ref))
          for rows in q_rows]
    k_refs, km_refs = (k1_ref, k2_ref), (km1_ref, km2_ref)
    nt = (((1,), (1,)), ((), ()))
    wide = (Q_TILE, LANES)

    def block(c, k, mask, slot, prev):
        n_col = k[0].shape[0] // LANES
        for i in range(2):
            s = lax.dot_general(qs[c][i], k[i], nt, preferred_element_type=F32)
            if mask is not None:
                s = jnp.where(mask, s, -jnp.inf)
            cols = [s[:, j * LANES:(j + 1) * LANES] for j in range(n_col)]
            m_tile = functools.reduce(jnp.maximum, cols)
            m_new = jnp.broadcast_to(jnp.max(m_tile, axis=1, keepdims=True), wide)
            if prev is not None:
                m_old = m_ref[c, i]
                m_new = jnp.maximum(m_old, m_new)
                alpha = jnp.exp2(m_old - m_new)
            m_ref[c, i] = m_new
            ps = [jnp.exp2(col - m_new) for col in cols]
            l_tile = functools.reduce(jnp.add, ps)
            l_ref[c, i] = l_tile if prev is None else alpha * l_ref[c, i] + l_tile
            for j in range(n_col):
                p_ref[c, slot, i, :, j * LANES:(j + 1) * LANES] = ps[j].astype(BF16)
            if prev is not None:
                slot_prev, v_prev = prev
                pv = jnp.dot(p_ref[c, slot_prev, i], v_prev, preferred_element_type=F32)
                acc_ref[c, i] = (acc_ref[c, i] + pv) * alpha

    def tile(ref, t):
        return ref[pl.ds(pl.multiple_of(t * KV_TILE, KV_TILE), KV_TILE), :]

    def ktile(t):
        return tuple(tile(r, t) for r in k_refs)

    acc_ref[...] = jnp.zeros(acc_ref.shape, F32)
    d0 = 2 * sp
    causal = (lax.broadcasted_iota(jnp.int32, (Q_TILE, KV_TILE), 1)
              <= lax.broadcasted_iota(jnp.int32, (Q_TILE, KV_TILE), 0))
    k_d0 = ktile(d0)
    block(0, k_d0, causal, 0, None)
    block(1, k_d0, None, 0, None)
    block(1, ktile(d0 + 1), causal, 1, (0, tile(v_ref, d0)))
    pending = (0, 1)

    def earlier(t0, n):
        for d in range(n):
            k_t = ktile(t0 + d)
            for c in range(2):
                before = jnp.where(t0 == 0, d0 + c, t0 - 1) if d == 0 else t0 + d - 1
                src = pending[c] if d % 2 == 0 else 1 - pending[c]
                block(c, k_t, None, 1 - src, (src, tile(v_ref, before)))

    def unrolled(jj, carry):
        earlier(ATTN_UNROLL * jj, ATTN_UNROLL)
        return carry

    lax.fori_loop(0, d0 // ATTN_UNROLL, unrolled, 0)
    for n in (ATTN_UNROLL // 2, ATTN_UNROLL // 4):
        if n >= 2:
            done = (d0 // (2 * n)) * (2 * n)
            pl.when(d0 - done >= n)(functools.partial(earlier, done, n))

    meta_mask = lax.broadcasted_iota(jnp.int32, wide, 1) < N_META
    k_meta = tuple(r[...] for r in km_refs)
    for c in range(2):
        before = jnp.where(sp == 0, d0 + c, d0 - 1)
        block(c, k_meta, meta_mask, 1 - pending[c], (pending[c], tile(v_ref, before)))
    for c in range(2):
        for i in range(2):
            acc_ref[c, i] = acc_ref[c, i] + jnp.dot(p_ref[c, 1 - pending[c], i, :, :LANES], vm_ref[...],
                                                    preferred_element_type=F32)

    lam = (jnp.exp(jnp.sum(lq1_ref[...] * lk1_ref[...], axis=1, keepdims=True))
           - jnp.exp(jnp.sum(lq2_ref[...] * lk2_ref[...], axis=1, keepdims=True)) + LAM_INIT)
    for c in range(2):
        l1 = jnp.sum(l_ref[c, 0], axis=1, keepdims=True)
        l2 = jnp.sum(l_ref[c, 1], axis=1, keepdims=True)
        o = acc_ref[c, 0] / l1 - lam * (acc_ref[c, 1] / l2)
        o_ref[q_rows[c], :] = (_rms(o, sg_ref[...]) * (1.0 - LAM_INIT)).astype(BF16)


def _attention(q, k, v, km, vm, lq1, lk1, lq2, lk2, sg, bsz, seq):
    assert Q_TILE == KV_TILE and seq % (2 * Q_TILE) == 0
    pairs = N_HEADS // 2
    kvmap = lambda b, h: (b, h)
    map1 = pl.BlockSpec((seq, LANES), lambda b, h: (b, h // 2))
    map2 = pl.BlockSpec((seq, LANES), lambda b, h: (b, pairs + h // 2))
    meta1 = pl.BlockSpec((LANES, LANES), lambda b, h: (0, h // 2))
    meta2 = pl.BlockSpec((LANES, LANES), lambda b, h: (0, pairs + h // 2))
    const = lambda b, h: (0, 0)
    vec = pl.BlockSpec((1, QK_DIM), const)
    return pl.pallas_call(
        _attn_kernel,
        grid=(bsz, N_HEADS),
        in_specs=[
            map1, map2, map1, map2,
            pl.BlockSpec((seq, LANES), kvmap),
            meta1, meta2,
            pl.BlockSpec((LANES, LANES), lambda b, h: (0, h)),
            vec, vec, vec, vec,
            pl.BlockSpec((1, V_DIM), const),
        ],
        out_specs=pl.BlockSpec((seq, LANES), kvmap),
        out_shape=jax.ShapeDtypeStruct((bsz * seq, ATTN_WIDTH), BF16),
        scratch_shapes=[
            pltpu.VMEM((2, 2, Q_TILE, LANES), F32),
            pltpu.VMEM((2, 2, Q_TILE, LANES), F32),
            pltpu.VMEM((2, 2, Q_TILE, V_DIM), F32),
            pltpu.VMEM((2, 2, 2, Q_TILE, KV_TILE), BF16),
        ],
        compiler_params=pltpu.CompilerParams(
            dimension_semantics=("arbitrary", "arbitrary"), vmem_limit_bytes=VMEM_LIMIT),
        name="diff_attention",
    )(q, q, k, k, v, km, km, vm, lq1, lk1, lq2, lk2, sg)


def _ssm_kernel(u_ref, um_ref, bb_ref, bc_ref, ccr_ref, cci_ref, loc_ref, blk_ref, d_ref, y_ref,
                tz_ref, bs_ref, cs_ref, ub_ref, buf_a, buf_b, cb_a, cb_b, rep_ref, yt_ref, *, n_chunks):
    n_tiles = OCT_STATE // LANES
    n_blocks = n_chunks // SUBLANES
    blk_steps = n_blocks.bit_length() - 1
    blk_pad = cb_a.shape[1] - n_blocks

    @pl.when((pl.program_id(0) == 0) & (pl.program_id(1) == 0))
    def _():
        for buf, pad in ((buf_a, SCAN_PAD), (buf_b, SCAN_PAD), (cb_a, blk_pad), (cb_b, blk_pad)):
            buf[:, :pad, :] = jnp.zeros((n_tiles, pad, LANES), F32)

    @pl.when(pl.program_id(1) == 0)
    def _():
        def group_of(shape, axis):
            return (lax.broadcasted_iota(jnp.int32, shape, axis) // SSM_GROUP) % OCT

        def split(x):
            head = x.astype(BF16)
            return head, (x - head.astype(F32)).astype(BF16)

        wide_shape = (LANES, OCT * SSM_STATE)
        own_cols = group_of(wide_shape, 0) == lax.broadcasted_iota(jnp.int32, wide_shape, 1) // SSM_STATE
        bbar = [split(jnp.where(own_cols, jnp.concatenate([bb_ref[part, 0]] * (OCT // 2), axis=1), 0.0))
                for part in range(2)]
        lag_blk = []
        for n in range(0, CHUNK, 2):
            terms = []
            for part, c_ref in enumerate((ccr_ref, cci_ref)):
                c_n = jnp.concatenate([c_ref[n], c_ref[n + 1]], axis=1)
                g_lane = group_of(c_n.shape, 1)
                ca_head, ca_rest = split(jnp.concatenate(
                    [jnp.where(g_lane == g, c_n, 0.0) for g in range(OCT)], axis=0))
                b_head, b_rest = bbar[part]
                terms.append(jnp.dot(b_head, ca_head, preferred_element_type=F32)
                             + jnp.dot(b_head, ca_rest, preferred_element_type=F32)
                             + jnp.dot(b_rest, ca_head, preferred_element_type=F32))
            both = (terms[0] - terms[1]).astype(BF16)
            lag_blk += [both[:, :LANES], both[:, LANES:]]
        for m in range(CHUNK // 2):
            tz_ref[m, :LANES, :LANES] = lag_blk[2 * m]
            tz_ref[m, :LANES, LANES:] = lag_blk[2 * m + 1]
            tz_ref[m, LANES:, :LANES] = lag_blk[2 * m - 1] if m else jnp.zeros((LANES, LANES), BF16)
            tz_ref[m, LANES:, LANES:] = lag_blk[2 * m]
        bc = bc_ref[:, 0].reshape(OCT_IN, 4 * SSM_STATE)
        g_row = group_of(bc.shape, 0)
        w_col = (lax.broadcasted_iota(jnp.int32, bc.shape, 1) // SSM_STATE) % 2
        for qq in range(OCT // 2):
            bs_ref[:, qq * MXU_DIM:(qq + 1) * MXU_DIM] = jnp.where(g_row == 2 * qq + w_col, bc, 0.0).astype(BF16)
        g_col = group_of((SSM_STATE, LANES), 1)
        for s in range(CHUNK):
            c_s = (ccr_ref[s + 1], -cci_ref[s + 1])
            for qq in range(OCT // 2):
                for part in range(2):
                    for w in range(2):
                        r0 = qq * MXU_DIM + part * LANES + w * SSM_STATE
                        cs_ref[r0:r0 + SSM_STATE, s * LANES:(s + 1) * LANES] = jnp.where(
                            g_col == 2 * qq + w, c_s[part], 0.0).astype(BF16)

    def slab(ref, t, rows):
        return ref[pl.ds(t, rows, stride=CHUNK), :]

    for t in range(CHUNK):
        ub_ref[t // 2, :, (t % 2) * LANES:(t % 2 + 1) * LANES] = slab(u_ref, t, n_chunks).astype(BF16)

    z = jnp.dot(ub_ref[0], bs_ref[:MXU_DIM, :], preferred_element_type=F32)
    for tt in range(1, CHUNK // 2):
        z = z + jnp.dot(ub_ref[tt], bs_ref[tt * MXU_DIM:(tt + 1) * MXU_DIM, :], preferred_element_type=F32)
    um = jnp.concatenate([jnp.broadcast_to(um_ref[t:t + 1, :], (SUBLANES, LANES)) for t in range(CHUNK)], axis=1)
    zm = jnp.dot(um.astype(BF16), bs_ref[...], preferred_element_type=F32)[0:1, :]
    for ct in range(n_tiles):
        cols = slice(ct * LANES, (ct + 1) * LANES)
        buf_a[ct, SCAN_PAD:, :] = z[:, cols]
        for buf, pad in ((buf_a, SCAN_PAD), (buf_b, SCAN_PAD), (cb_a, blk_pad), (cb_b, blk_pad)):
            buf[ct, pad - 1:pad, :] = zm[:, cols]

    def within_chunk(ss):
        acc = jnp.dot(ub_ref[0], tz_ref[ss], preferred_element_type=F32)
        for tt in range(1, ss + 1):
            acc = acc + jnp.dot(ub_ref[tt], tz_ref[ss - tt], preferred_element_type=F32)
        yt_ref[ss] = acc

    assert LOCAL_STEPS + blk_steps >= CHUNK // 2 and (1 << LOCAL_STEPS) == SUBLANES
    todo = list(range(CHUNK // 2 - 1, -1, -1))
    blocked = (n_blocks, SUBLANES, LANES)

    def lanes_of(pair):
        return slice(pair * LANES, (pair + 1) * LANES)

    def combine(x, p, ar, ai):
        return x[0] + ar * p[0] - ai * p[1], x[1] + ar * p[1] + ai * p[0]

    src, dst = buf_a, buf_b
    for step in range(LOCAL_STEPS):
        shift = 1 << step
        within_chunk(todo.pop(0))
        for pair in range(OCT // 2):
            ar, ai = (loc_ref[0, step, part, :, lanes_of(pair)] for part in range(2))
            x = [src[2 * pair + part, SCAN_PAD:, :].reshape(blocked) for part in range(2)]
            p = [src[2 * pair + part, SCAN_PAD - shift:SCAN_PAD - shift + n_chunks, :].reshape(blocked)
                 for part in range(2)]
            for part, val in enumerate(combine(x, p, ar, ai)):
                dst[2 * pair + part, SCAN_PAD:, :] = val.reshape(n_chunks, LANES)
        src, dst = dst, src

    for ct in range(n_tiles):
        cb_a[ct, blk_pad:, :] = src[ct, pl.ds(SCAN_PAD + SUBLANES - 1, n_blocks, stride=SUBLANES), :]
    csrc, cdst = cb_a, cb_b
    for step in range(blk_steps):
        shift = 1 << step
        if todo:
            within_chunk(todo.pop(0))
        for pair in range(OCT // 2):
            ar = blk_ref[0, step:step + 1, lanes_of(pair)]
            ai = blk_ref[0, blk_steps + step:blk_steps + step + 1, lanes_of(pair)]
            x = [csrc[2 * pair + part, blk_pad:, :] for part in range(2)]
            p = [csrc[2 * pair + part, blk_pad - shift:blk_pad - shift + n_blocks, :] for part in range(2)]
            for part, val in enumerate(combine(x, p, ar, ai)):
                cdst[2 * pair + part, blk_pad:, :] = val
        csrc, cdst = cdst, csrc

    for pair in range(OCT // 2):
        for part in range(2):
            entering = csrc[2 * pair + part, blk_pad - 1:blk_pad - 1 + n_blocks, :]
            for row in range(SUBLANES):
                rep_ref[pair, part, pl.ds(row, n_blocks, stride=SUBLANES), :] = entering
        ar, ai = (loc_ref[0, LOCAL_STEPS, part, :, lanes_of(pair)] for part in range(2))
        x = [src[2 * pair + part, SCAN_PAD:, :].reshape(blocked) for part in range(2)]
        p = [rep_ref[pair, part].reshape(blocked) for part in range(2)]
        for part, val in enumerate(combine(x, p, ar, ai)):
            dst[2 * pair + part, SCAN_PAD:, :] = val.reshape(n_chunks, LANES)

    start_state = jnp.concatenate(
        [dst[ct, SCAN_PAD - 1:SCAN_PAD - 1 + n_chunks, :] for ct in range(n_tiles)], axis=1).astype(BF16)
    for ss in range(CHUNK // 2):
        cols = slice(ss * MXU_DIM, (ss + 1) * MXU_DIM)
        acc = yt_ref[ss] + jnp.dot(start_state, cs_ref[:, cols], preferred_element_type=F32)
        for half in range(2):
            t = 2 * ss + half
            y = acc[:, half * LANES:(half + 1) * LANES] + d_ref[0] * slab(u_ref, t, n_chunks)
            y_ref[pl.ds(t, n_chunks, stride=CHUNK), :] = 0.5 * y * (1.0 + lax.erf(y * (1.0 / math.sqrt(2.0))))


def _ssm(u, u_meta, bb2, bc, cc_re, cc_im, loc, blk, dp, bsz, seq):
    n_chunks = seq // CHUNK
    n_blocks = n_chunks // SUBLANES
    assert n_blocks & (n_blocks - 1) == 0 and N_META == CHUNK
    octet = lambda o, b: (o, 0, 0)
    n_tiles = OCT_STATE // LANES
    scan_buf = pltpu.VMEM((n_tiles, SCAN_PAD + n_chunks, LANES), F32)
    block_buf = pltpu.VMEM((n_tiles, n_blocks // 2 + n_blocks, LANES), F32)
    return pl.pallas_call(
        functools.partial(_ssm_kernel, n_chunks=n_chunks),
        grid=(N_OCT, bsz),
        in_specs=[
            pl.BlockSpec((seq, LANES), lambda o, b: (b, o)),
            pl.BlockSpec((N_META, LANES), lambda o, b: (0, o)),
            pl.BlockSpec((2, 1, LANES, 2 * SSM_STATE), lambda o, b: (0, o, 0, 0)),
            pl.BlockSpec((CHUNK, 1, LANES, 4 * SSM_STATE), lambda o, b: (0, o, 0, 0)),
            pl.BlockSpec((CHUNK + 1, SSM_STATE, LANES), lambda o, b: (0, 0, o)),
            pl.BlockSpec((CHUNK + 1, SSM_STATE, LANES), lambda o, b: (0, 0, o)),
            pl.BlockSpec((1,) + loc.shape[1:], lambda o, b: (o, 0, 0, 0, 0)),
            pl.BlockSpec((1,) + blk.shape[1:], octet),
            pl.BlockSpec((1, 1, LANES), octet),
        ],
        out_specs=pl.BlockSpec((seq, LANES), lambda o, b: (b, o)),
        out_shape=jax.ShapeDtypeStruct((bsz * seq, SSM_WIDTH), F32),
        scratch_shapes=[
            pltpu.VMEM((CHUNK // 2, MXU_DIM, MXU_DIM), BF16),
            pltpu.VMEM((OCT_IN, OCT_STATE), BF16),
            pltpu.VMEM((OCT_STATE, OCT_IN), BF16),
            pltpu.VMEM((CHUNK // 2, n_chunks, MXU_DIM), BF16), scan_buf, scan_buf, block_buf, block_buf,
            pltpu.VMEM((OCT // 2, 2, n_chunks, LANES), F32),
            pltpu.VMEM((CHUNK // 2, n_chunks, MXU_DIM), F32)],
        compiler_params=pltpu.CompilerParams(
            dimension_semantics=("arbitrary", "arbitrary"), vmem_limit_bytes=VMEM_LIMIT),
        name="s5_chunked",
    )(u, u_meta, bb2, bc, cc_re, cc_im, loc, blk, dp)


def _ssm_weights(a_re, a_im, log_dt, b_re, b_im, c_re, c_im, d_skip, n_chunks):
    dt = jnp.exp(log_dt)[:, None]
    lam_re, lam_im = a_re * dt, a_im * dt

    def power(n, transposed=False):
        n = n[:, None, None]
        l_re, l_im = (lam_re.T, lam_im.T) if transposed else (lam_re, lam_im)
        mag = jnp.exp(n * l_re)
        return mag * jnp.cos(n * l_im), mag * jnp.sin(n * l_im)

    ab_re, ab_im = jnp.exp(lam_re) * jnp.cos(lam_im), jnp.exp(lam_re) * jnp.sin(lam_im)
    den = a_re * a_re + a_im * a_im
    nr, ni = ab_re - 1.0, ab_im
    f_re = (nr * a_re + ni * a_im) / den
    f_im = (ni * a_re - nr * a_im) / den
    bb_re = f_re[..., None] * b_re - f_im[..., None] * b_im
    bb_im = f_re[..., None] * b_im + f_im[..., None] * b_re

    p_re, p_im = power(jnp.arange(CHUNK + 1, dtype=F32))

    ct_re, ct_im = c_re.transpose(2, 0, 1)[None], c_im.transpose(2, 0, 1)[None]
    pt_re, pt_im = (t[..., None] for t in power(jnp.arange(CHUNK + 1, dtype=F32), transposed=True))
    ca_re = ct_re * pt_re - ct_im * pt_im
    ca_im = ct_re * pt_im + ct_im * pt_re
    cc_re = ca_re.reshape(CHUNK + 1, SSM_STATE, SSM_WIDTH)
    cc_im = ca_im.reshape(CHUNK + 1, SSM_STATE, SSM_WIDTH)

    r_re, r_im = p_re[CHUNK - 1::-1][:, :, None, :], p_im[CHUNK - 1::-1][:, :, None, :]
    bt_re, bt_im = bb_re.transpose(0, 2, 1)[None], bb_im.transpose(0, 2, 1)[None]
    bb2 = jnp.stack([jnp.concatenate([t, t], axis=-1).reshape(N_OCT, LANES, 2 * SSM_STATE) for t in (bt_re, bt_im)])
    bs_re = r_re * bt_re - r_im * bt_im
    bs_im = r_re * bt_im + r_im * bt_re
    bc = jnp.concatenate([bs_re, bs_re, bs_im, bs_im], axis=-1).reshape(CHUNK, N_OCT, LANES, 4 * SSM_STATE)

    rows = jnp.arange(SUBLANES)
    strides = 2 ** jnp.arange(LOCAL_STEPS)
    in_block = (rows[None, :] >= strides[:, None])[:, :, None, None]
    blk_steps = (n_chunks // SUBLANES).bit_length() - 1

    def pack(m):
        m = m.reshape(m.shape[:-2] + (N_OCT, OCT_STATE // 2))
        return jnp.moveaxis(m, -2, 0)

    loc, blk = [], []
    for carry, block_stride in zip(power(CHUNK * (rows + 1.0)), power(CHUNK * SUBLANES * 2.0 ** jnp.arange(blk_steps))):
        steps = jnp.where(in_block, carry[strides - 1][:, None], 0.0)
        loc.append(pack(jnp.concatenate([steps, carry[None]], axis=0)))
        blk.append(pack(block_stride))
    loc = jnp.stack(loc, axis=2)
    blk = jnp.concatenate(blk, axis=1)

    dp = d_skip.reshape(N_OCT, 1, LANES)
    return bb2, bc, cc_re, cc_im, loc, blk, dp


def _out_ffn_kernel(x_ref, o_ref, y_ref, wglu_ref, bglu_ref, sg_ref, wout_ref, pmg_ref, pfg_ref,
                    wg_ref, wu_ref, wd_ref, pog_ref, out_ref):
    y = y_ref[...]
    gate = jnp.dot(y.astype(BF16), wglu_ref[...], preferred_element_type=F32) + bglu_ref[...]
    y = _rms(y * jax.nn.sigmoid(gate), sg_ref[...]).astype(BF16)
    mix = (jnp.dot(o_ref[...], wout_ref[:ATTN_WIDTH, :], preferred_element_type=F32)
           + jnp.dot(y, wout_ref[ATTN_WIDTH:, :], preferred_element_type=F32))
    h1 = x_ref[...] + _rms(mix, pmg_ref[...])
    h2 = _rms(h1, pfg_ref[...]).astype(BF16)
    g = jnp.dot(h2, wg_ref[...], preferred_element_type=F32)
    up = jnp.dot(h2, wu_ref[...], preferred_element_type=F32)
    f = (g * jax.nn.sigmoid(g) * up).astype(BF16)
    f = jnp.dot(f, wd_ref[...], preferred_element_type=F32)
    out_ref[...] = h1 + _rms(f, pog_ref[...])


def _out_ffn(x2d, o, y, wglu, bglu, sg, wout, pmg, pfg, wg, wu, wd, pog, tm):
    n = x2d.shape[0]
    row = lambda i: (i, 0)
    const = lambda i: (0, 0)

    def resident(shape):
        return pl.BlockSpec(shape, const, pipeline_mode=pl.Buffered(1))

    return pl.pallas_call(
        _out_ffn_kernel,
        grid=(n // tm,),
        in_specs=[
            pl.BlockSpec((tm, D_MODEL), row),
            pl.BlockSpec((tm, ATTN_WIDTH), row),
            pl.BlockSpec((tm, SSM_WIDTH), row),
            resident((SSM_WIDTH, SSM_WIDTH)),
            resident((1, SSM_WIDTH)),
            resident((1, SSM_WIDTH)),
            resident((D_MODEL, D_MODEL)),
            resident((1, D_MODEL)),
            resident((1, D_MODEL)),
            resident((D_MODEL, D_FF)),
            resident((D_MODEL, D_FF)),
            resident((D_FF, D_MODEL)),
            resident((1, D_MODEL)),
        ],
        out_specs=pl.BlockSpec((tm, D_MODEL), row),
        out_shape=jax.ShapeDtypeStruct((n, D_MODEL), F32),
        compiler_params=pltpu.CompilerParams(
            dimension_semantics=("arbitrary",), vmem_limit_bytes=VMEM_LIMIT),
        name="out_ffn",
    )(x2d, o, y, wglu, bglu, sg, wout, pmg, pfg, wg, wu, wd, pog)


def _rope_angles(pos):
    d = jnp.arange(LANES) % QK_DIM
    inv_freq = ROPE_THETA ** (-(2 * (d % (ROT_DIM // 2))).astype(F32) / ROT_DIM)
    ang = pos[:, None] * jnp.where(d < ROT_DIM, inv_freq, 0.0)[None, :]
    return jnp.cos(ang), jnp.sin(ang)


def _scale_w_in(w):
    scale = math.log2(math.e) / math.sqrt(QK_DIM)
    col_scale = jnp.where(jnp.arange(w.shape[1]) < ATTN_WIDTH, scale, 1.0).astype(F32)
    return (w * col_scale[None, :]).astype(BF16)


def kernel(x, meta, pre_mix_g, w_in, lambda_q1, lambda_k1, lambda_q2, lambda_k2, subln_g, a_re, a_im, log_dt,
           b_re, b_im, c_re, c_im, d_skip, w_glu, b_glu, ssm_out_g, w_out, post_mix_g, pre_ffn_g, w_gate,
           w_up, w_down, post_ffn_g):
    bsz, seq, _ = x.shape
    n = bsz * seq
    x2d = x.reshape(n, D_MODEL)
    row = lambda t: t[0].reshape(1, -1)

    w_in_p = _scale_w_in(w_in[0])
    g0 = row(pre_mix_g)
    q, k, v, u = _in_proj(x2d, g0, w_in_p, N_META, seq, ROW_TILE)
    _, k_m, v_m, u_m = _in_proj(meta, g0, w_in_p, 0, N_META, N_META)

    pad_meta = lambda t: jnp.pad(t, ((0, LANES - N_META), (0, 0)))
    o = _attention(q, k, v, pad_meta(k_m), pad_meta(v_m), lambda_q1, lambda_k1, lambda_q2, lambda_k2,
                   row(subln_g), bsz, seq)

    ssm_w = _ssm_weights(a_re[0], a_im[0], log_dt[0], b_re[0], b_im[0], c_re[0], c_im[0], d_skip[0],
                         seq // CHUNK)
    y = _ssm(u, u_m, *ssm_w, bsz, seq)

    out = _out_ffn(x2d, o, y, w_glu[0].astype(BF16), row(b_glu), row(ssm_out_g), w_out[0].astype(BF16),
                   row(post_mix_g), row(pre_ffn_g), w_gate[0].astype(BF16), w_up[0].astype(BF16),
                   w_down[0].astype(BF16), row(post_ffn_g), ROW_TILE)
    return out.reshape(bsz, seq, D_MODEL)
```

```python
import functools
import math

import jax
import jax.numpy as jnp
from jax import lax
from jax.experimental import pallas as pl
from jax.experimental.pallas import tpu as pltpu

D_MODEL = 1024
N_META = 16
N_HEADS = 4
QK_DIM = 64
V_DIM = 128
ROT_DIM = 16
ROPE_THETA = 500000.0
SSM_GROUP = 16
N_GROUPS = 32
SSM_STATE = 64
SSM_WIDTH = 512
ATTN_WIDTH = 512
D_FF = 2816
EPS = 1e-6
LAM_INIT = 0.8 - 0.6 * math.exp(-0.3 * 0)

LANES = 128
SUBLANES = 8
MXU_DIM = 256
CHUNK = 16
OCT = LANES // SSM_GROUP
N_OCT = N_GROUPS // OCT
OCT_IN = CHUNK * LANES
OCT_STATE = OCT * 2 * SSM_STATE
SCAN_PAD = SUBLANES
LOCAL_STEPS = 3

ROW_TILE = 512
IN_ROW_TILE = 1024
Q_TILE = 512
KV_TILE = 512
ATTN_UNROLL = 4
VMEM_LIMIT = 56 * 1024 * 1024

F32 = jnp.float32
BF16 = jnp.bfloat16


def _rms(x, g):
    return x * lax.rsqrt(jnp.mean(x * x, axis=-1, keepdims=True) + EPS) * g


def _in_proj_kernel(x_ref, g_ref, w_ref, cr_ref, sr_ref, cb_ref, sb_ref, q_ref, k_ref, v_ref, u_ref):
    h = _rms(x_ref[...], g_ref[...]).astype(BF16)
    proj = jnp.dot(h, w_ref[...], preferred_element_type=F32)
    cos = cb_ref[0] * cr_ref[...] - sb_ref[0] * sr_ref[...]
    sin = sb_ref[0] * cr_ref[...] + cb_ref[0] * sr_ref[...]
    d = lax.broadcasted_iota(jnp.int32, (1, LANES), 1) % QK_DIM
    sa = sin * jnp.where(d < ROT_DIM // 2, -1.0, 0.0)
    sb = sin * jnp.where((d >= ROT_DIM // 2) & (d < ROT_DIM), 1.0, 0.0)

    def rope(t):
        return (t * cos + pltpu.roll(t, LANES - ROT_DIM // 2, axis=1) * sa
                + pltpu.roll(t, ROT_DIM // 2, axis=1) * sb)

    for c in range(ATTN_WIDTH // LANES):
        sl = slice(c * LANES, (c + 1) * LANES)
        q_ref[:, sl] = rope(proj[:, c * LANES:(c + 1) * LANES]).astype(BF16)
        k_ref[:, sl] = rope(proj[:, ATTN_WIDTH + c * LANES:ATTN_WIDTH + (c + 1) * LANES]).astype(BF16)
    v_ref[...] = proj[:, 2 * ATTN_WIDTH:3 * ATTN_WIDTH].astype(BF16)
    u_ref[...] = proj[:, 3 * ATTN_WIDTH:]


def _in_proj(x2d, g, w, first_pos, seq, tm):
    n = x2d.shape[0]
    tiles_per_seq = seq // tm
    cos_r, sin_r = _rope_angles(jnp.arange(tm, dtype=F32))
    cos_b, sin_b = _rope_angles(first_pos + tm * jnp.arange(tiles_per_seq, dtype=F32))
    row = lambda i: (i, 0)
    const = lambda i: (0, 0)
    base = pl.BlockSpec((1, 1, LANES), lambda i: (i % tiles_per_seq, 0, 0))
    return pl.pallas_call(
        _in_proj_kernel,
        grid=(n // tm,),
        in_specs=[
            pl.BlockSpec((tm, D_MODEL), row),
            pl.BlockSpec((1, D_MODEL), const),
            pl.BlockSpec((D_MODEL, 4 * ATTN_WIDTH), const),
            pl.BlockSpec((tm, LANES), const),
            pl.BlockSpec((tm, LANES), const),
            base, base,
        ],
        out_specs=[pl.BlockSpec((tm, ATTN_WIDTH), row)] * 4,
        out_shape=[jax.ShapeDtypeStruct((n, ATTN_WIDTH), BF16)] * 3
        + [jax.ShapeDtypeStruct((n, SSM_WIDTH), F32)],
        compiler_params=pltpu.CompilerParams(
            dimension_semantics=("arbitrary",), vmem_limit_bytes=VMEM_LIMIT),
        name="in_proj",
    )(x2d, g, w, cos_r, sin_r, cos_b[:, None, :], sin_b[:, None, :])


def _attn_kernel(*refs):
    n_pairs = refs[0].shape[0] // (2 * Q_TILE)

    def q_pair(sp, carry):
        _attn_q_pair(sp, *refs)
        return carry

    lax.fori_loop(0, n_pairs, q_pair, 0)


def _attn_q_pair(sp, q1_ref, q2_ref, k1_ref, k2_ref, v_ref, km1_ref, km2_ref, vm_ref,
                 lq1_ref, lk1_ref, lq2_ref, lk2_ref, sg_ref,
                 o_ref, m_ref, l_ref, acc_ref, p_ref):
    own = lax.broadcasted_iota(jnp.int32, (Q_TILE, LANES), 1) // QK_DIM == pl.program_id(1) % 2
    q_rows = [pl.ds(pl.multiple_of((2 * sp + c) * Q_TILE, Q_TILE), Q_TILE) for c in range(2)]
    qs = [tuple(jnp.where(own, r[rows, :], jnp.zeros((Q_TILE, LANES), BF16)) for r in (q1_ref, q2_ref))
          for rows in q_rows]
    k_refs, km_refs = (k1_ref, k2_ref), (km1_ref, km2_ref)
    nt = (((1,), (1,)), ((), ()))
    wide = (Q_TILE, LANES)

    def block(c, k, mask, slot, prev):
        n_col = k[0].shape[0] // LANES
        for i in range(2):
            s = lax.dot_general(qs[c][i], k[i], nt, preferred_element_type=F32)
            if mask is not None:
                s = jnp.where(mask, s, -jnp.inf)
            cols = [s[:, j * LANES:(j + 1) * LANES] for j in range(n_col)]
            m_tile = functools.reduce(jnp.maximum, cols)
            m_new = jnp.broadcast_to(jnp.max(m_tile, axis=1, keepdims=True), wide)
            if prev is not None:
                m_old = m_ref[c, i]
                m_new = jnp.maximum(m_old, m_new)
                alpha = jnp.exp2(m_old - m_new)
            m_ref[c, i] = m_new
            ps = [jnp.exp2(col - m_new) for col in cols]
            l_tile = functools.reduce(jnp.add, ps)
            l_ref[c, i] = l_tile if prev is None else alpha * l_ref[c, i] + l_tile
            for j in range(n_col):
                p_ref[c, slot, i, :, j * LANES:(j + 1) * LANES] = ps[j].astype(BF16)
            if prev is not None:
                slot_prev, v_prev = prev
                pv = jnp.dot(p_ref[c, slot_prev, i], v_prev, preferred_element_type=F32)
                acc_ref[c, i] = (acc_ref[c, i] + pv) * alpha

    def tile(ref, t):
        return ref[pl.ds(pl.multiple_of(t * KV_TILE, KV_TILE), KV_TILE), :]

    def ktile(t):
        return tuple(tile(r, t) for r in k_refs)

    acc_ref[...] = jnp.zeros(acc_ref.shape, F32)
    d0 = 2 * sp
    causal = (lax.broadcasted_iota(jnp.int32, (Q_TILE, KV_TILE), 1)
              <= lax.broadcasted_iota(jnp.int32, (Q_TILE, KV_TILE), 0))
    k_d0 = ktile(d0)
    block(0, k_d0, causal, 0, None)
    block(1, k_d0, None, 0, None)
    block(1, ktile(d0 + 1), causal, 1, (0, tile(v_ref, d0)))
    pending = (0, 1)

    def earlier(t0, n):
        for d in range(n):
            k_t = ktile(t0 + d)
            for c in range(2):
                before = jnp.where(t0 == 0, d0 + c, t0 - 1) if d == 0 else t0 + d - 1
                src = pending[c] if d % 2 == 0 else 1 - pending[c]
                block(c, k_t, None, 1 - src, (src, tile(v_ref, before)))

    def unrolled(jj, carry):
        earlier(ATTN_UNROLL * jj, ATTN_UNROLL)
        return carry

    lax.fori_loop(0, d0 // ATTN_UNROLL, unrolled, 0)
    for n in (ATTN_UNROLL // 2, ATTN_UNROLL // 4):
        if n >= 2:
            done = (d0 // (2 * n)) * (2 * n)
            pl.when(d0 - done >= n)(functools.partial(earlier, done, n))

    meta_mask = lax.broadcasted_iota(jnp.int32, wide, 1) < N_META
    k_meta = tuple(r[...] for r in km_refs)
    for c in range(2):
        before = jnp.where(sp == 0, d0 + c, d0 - 1)
        block(c, k_meta, meta_mask, 1 - pending[c], (pending[c], tile(v_ref, before)))
    for c in range(2):
        for i in range(2):
            acc_ref[c, i] = acc_ref[c, i] + jnp.dot(p_ref[c, 1 - pending[c], i, :, :LANES], vm_ref[...],
                                                    preferred_element_type=F32)

    lam = (jnp.exp(jnp.sum(lq1_ref[...] * lk1_ref[...], axis=1, keepdims=True))
           - jnp.exp(jnp.sum(lq2_ref[...] * lk2_ref[...], axis=1, keepdims=True)) + LAM_INIT)
    for c in range(2):
        l1 = jnp.sum(l_ref[c, 0], axis=1, keepdims=True)
        l2 = jnp.sum(l_ref[c, 1], axis=1, keepdims=True)
        o = acc_ref[c, 0] / l1 - lam * (acc_ref[c, 1] / l2)
        o_ref[q_rows[c], :] = (_rms(o, sg_ref[...]) * (1.0 - LAM_INIT)).astype(BF16)


def _attention(q, k, v, km, vm, lq1, lk1, lq2, lk2, sg, bsz, seq):
    assert Q_TILE == KV_TILE and seq % (2 * Q_TILE) == 0
    pairs = N_HEADS // 2
    kvmap = lambda b, h: (b, h)
    map1 = pl.BlockSpec((seq, LANES), lambda b, h: (b, h // 2))
    map2 = pl.BlockSpec((seq, LANES), lambda b, h: (b, pairs + h // 2))
    meta1 = pl.BlockSpec((LANES, LANES), lambda b, h: (0, h // 2))
    meta2 = pl.BlockSpec((LANES, LANES), lambda b, h: (0, pairs + h // 2))
    const = lambda b, h: (0, 0)
    vec = pl.BlockSpec((1, QK_DIM), const)
    return pl.pallas_call(
        _attn_kernel,
        grid=(bsz, N_HEADS),
        in_specs=[
            map1, map2, map1, map2,
            pl.BlockSpec((seq, LANES), kvmap),
            meta1, meta2,
            pl.BlockSpec((LANES, LANES), lambda b, h: (0, h)),
            vec, vec, vec, vec,
            pl.BlockSpec((1, V_DIM), const),
        ],
        out_specs=pl.BlockSpec((seq, LANES), kvmap),
        out_shape=jax.ShapeDtypeStruct((bsz * seq, ATTN_WIDTH), BF16),
        scratch_shapes=[
            pltpu.VMEM((2, 2, Q_TILE, LANES), F32),
            pltpu.VMEM((2, 2, Q_TILE, LANES), F32),
            pltpu.VMEM((2, 2, Q_TILE, V_DIM), F32),
            pltpu.VMEM((2, 2, 2, Q_TILE, KV_TILE), BF16),
        ],
        compiler_params=pltpu.CompilerParams(
            dimension_semantics=("arbitrary", "arbitrary"), vmem_limit_bytes=VMEM_LIMIT),
        name="diff_attention",
    )(q, q, k, k, v, km, km, vm, lq1, lk1, lq2, lk2, sg)


def _ssm_kernel(u_ref, um_ref, bb_ref, bc_ref, ccr_ref, cci_ref, loc_ref, blk_ref, d_ref, y_ref,
                tz_ref, bs_ref, cs_ref, ub_ref, buf_a, buf_b, cb_a, cb_b, rep_ref, yt_ref, *, n_chunks):
    n_tiles = OCT_STATE // LANES
    n_blocks = n_chunks // SUBLANES
    blk_steps = n_blocks.bit_length() - 1
    blk_pad = cb_a.shape[1] - n_blocks

    @pl.when((pl.program_id(0) == 0) & (pl.program_id(1) == 0))
    def _():
        for buf, pad in ((buf_a, SCAN_PAD), (buf_b, SCAN_PAD), (cb_a, blk_pad), (cb_b, blk_pad)):
            buf[:, :pad, :] = jnp.zeros((n_tiles, pad, LANES), F32)

    @pl.when(pl.program_id(1) == 0)
    def _():
        def group_of(shape, axis):
            return (lax.broadcasted_iota(jnp.int32, shape, axis) // SSM_GROUP) % OCT

        def split(x):
            head = x.astype(BF16)
            return head, (x - head.astype(F32)).astype(BF16)

        wide_shape = (LANES, OCT * SSM_STATE)
        own_cols = group_of(wide_shape, 0) == lax.broadcasted_iota(jnp.int32, wide_shape, 1) // SSM_STATE
        bbar = [split(jnp.where(own_cols, jnp.concatenate([bb_ref[part, 0]] * (OCT // 2), axis=1), 0.0))
                for part in range(2)]
        lag_blk = []
        for n in range(0, CHUNK, 2):
            terms = []
            for part, c_ref in enumerate((ccr_ref, cci_ref)):
                c_n = jnp.concatenate([c_ref[n], c_ref[n + 1]], axis=1)
                g_lane = group_of(c_n.shape, 1)
                ca_head, ca_rest = split(jnp.concatenate(
                    [jnp.where(g_lane == g, c_n, 0.0) for g in range(OCT)], axis=0))
                b_head, b_rest = bbar[part]
                terms.append(jnp.dot(b_head, ca_head, preferred_element_type=F32)
                             + jnp.dot(b_head, ca_rest, preferred_element_type=F32)
                             + jnp.dot(b_rest, ca_head, preferred_element_type=F32))
            both = (terms[0] - terms[1]).astype(BF16)
            lag_blk += [both[:, :LANES], both[:, LANES:]]
        for m in range(CHUNK // 2):
            tz_ref[m, :LANES, :LANES] = lag_blk[2 * m]
            tz_ref[m, :LANES, LANES:] = lag_blk[2 * m + 1]
            tz_ref[m, LANES:, :LANES] = lag_blk[2 * m - 1] if m else jnp.zeros((LANES, LANES), BF16)
            tz_ref[m, LANES:, LANES:] = lag_blk[2 * m]
        bc = bc_ref[:, 0].reshape(OCT_IN, 4 * SSM_STATE)
        g_row = group_of(bc.shape, 0)
        w_col = (lax.broadcasted_iota(jnp.int32, bc.shape, 1) // SSM_STATE) % 2
        for qq in range(OCT // 2):
            bs_ref[:, qq * MXU_DIM:(qq + 1) * MXU_DIM] = jnp.where(g_row == 2 * qq + w_col, bc, 0.0).astype(BF16)
        g_col = group_of((SSM_STATE, LANES), 1)
        for s in range(CHUNK):
            c_s = (ccr_ref[s + 1], -cci_ref[s + 1])
            for qq in range(OCT // 2):
                for part in range(2):
                    for w in range(2):
                        r0 = qq * MXU_DIM + part * LANES + w * SSM_STATE
                        cs_ref[r0:r0 + SSM_STATE, s * LANES:(s + 1) * LANES] = jnp.where(
                            g_col == 2 * qq + w, c_s[part], 0.0).astype(BF16)

    def slab(ref, t, rows):
        return ref[pl.ds(t, rows, stride=CHUNK), :]

    for t in range(CHUNK):
        ub_ref[t // 2, :, (t % 2) * LANES:(t % 2 + 1) * LANES] = slab(u_ref, t, n_chunks).astype(BF16)

    z = jnp.dot(ub_ref[0], bs_ref[:MXU_DIM, :], preferred_element_type=F32)
    for tt in range(1, CHUNK // 2):
        z = z + jnp.dot(ub_ref[tt], bs_ref[tt * MXU_DIM:(tt + 1) * MXU_DIM, :], preferred_element_type=F32)
    um = jnp.concatenate([jnp.broadcast_to(um_ref[t:t + 1, :], (SUBLANES, LANES)) for t in range(CHUNK)], axis=1)
    zm = jnp.dot(um.astype(BF16), bs_ref[...], preferred_element_type=F32)[0:1, :]
    for ct in range(n_tiles):
        cols = slice(ct * LANES, (ct + 1) * LANES)
        buf_a[ct, SCAN_PAD:, :] = z[:, cols]
        for buf, pad in ((buf_a, SCAN_PAD), (buf_b, SCAN_PAD), (cb_a, blk_pad), (cb_b, blk_pad)):
            buf[ct, pad - 1:pad, :] = zm[:, cols]

    def within_chunk(ss):
        acc = jnp.dot(ub_ref[0], tz_ref[ss], preferred_element_type=F32)
        for tt in range(1, ss + 1):
            acc = acc + jnp.dot(ub_ref[tt], tz_ref[ss - tt], preferred_element_type=F32)
        yt_ref[ss] = acc

    assert LOCAL_STEPS + blk_steps >= CHUNK // 2 and (1 << LOCAL_STEPS) == SUBLANES
    todo = list(range(CHUNK // 2 - 1, -1, -1))
    blocked = (n_blocks, SUBLANES, LANES)

    def lanes_of(pair):
        return slice(pair * LANES, (pair + 1) * LANES)

    def combine(x, p, ar, ai):
        return x[0] + ar * p[0] - ai * p[1], x[1] + ar * p[1] + ai * p[0]

    src, dst = buf_a, buf_b
    for step in range(LOCAL_STEPS):
        shift = 1 << step
        within_chunk(todo.pop(0))
        for pair in range(OCT // 2):
            ar, ai = (loc_ref[0, step, part, :, lanes_of(pair)] for part in range(2))
            x = [src[2 * pair + part, SCAN_PAD:, :].reshape(blocked) for part in range(2)]
            p = [src[2 * pair + part, SCAN_PAD - shift:SCAN_PAD - shift + n_chunks, :].reshape(blocked)
                 for part in range(2)]
            for part, val in enumerate(combine(x, p, ar, ai)):
                dst[2 * pair + part, SCAN_PAD:, :] = val.reshape(n_chunks, LANES)
        src, dst = dst, src

    for ct in range(n_tiles):
        cb_a[ct, blk_pad:, :] = src[ct, pl.ds(SCAN_PAD + SUBLANES - 1, n_blocks, stride=SUBLANES), :]
    csrc, cdst = cb_a, cb_b
    for step in range(blk_steps):
        shift = 1 << step
        if todo:
            within_chunk(todo.pop(0))
        for pair in range(OCT // 2):
            ar = blk_ref[0, step:step + 1, lanes_of(pair)]
            ai = blk_ref[0, blk_steps + step:blk_steps + step + 1, lanes_of(pair)]
            x = [csrc[2 * pair + part, blk_pad:, :] for part in range(2)]
            p = [csrc[2 * pair + part, blk_pad - shift:blk_pad - shift + n_blocks, :] for part in range(2)]
            for part, val in enumerate(combine(x, p, ar, ai)):
                cdst[2 * pair + part, blk_pad:, :] = val
        csrc, cdst = cdst, csrc

    for pair in range(OCT // 2):
        for part in range(2):
            entering = csrc[2 * pair + part, blk_pad - 1:blk_pad - 1 + n_blocks, :]
            for row in range(SUBLANES):
                rep_ref[pair, part, pl.ds(row, n_blocks, stride=SUBLANES), :] = entering
        ar, ai = (loc_ref[0, LOCAL_STEPS, part, :, lanes_of(pair)] for part in range(2))
        x = [src[2 * pair + part, SCAN_PAD:, :].reshape(blocked) for part in range(2)]
        p = [rep_ref[pair, part].reshape(blocked) for part in range(2)]
        for part, val in enumerate(combine(x, p, ar, ai)):
            dst[2 * pair + part, SCAN_PAD:, :] = val.reshape(n_chunks, LANES)

    start_state = jnp.concatenate(
        [dst[ct, SCAN_PAD - 1:SCAN_PAD - 1 + n_chunks, :] for ct in range(n_tiles)], axis=1).astype(BF16)
    for ss in range(CHUNK // 2):
        cols = slice(ss * MXU_DIM, (ss + 1) * MXU_DIM)
        acc = yt_ref[ss] + jnp.dot(start_state, cs_ref[:, cols], preferred_element_type=F32)
        for half in range(2):
            t = 2 * ss + half
            y = acc[:, half * LANES:(half + 1) * LANES] + d_ref[0] * slab(u_ref, t, n_chunks)
            y_ref[pl.ds(t, n_chunks, stride=CHUNK), :] = 0.5 * y * (1.0 + lax.erf(y * (1.0 / math.sqrt(2.0))))


def _ssm(u, u_meta, bb2, bc, cc_re, cc_im, loc, blk, dp, bsz, seq):
    n_chunks = seq // CHUNK
    n_blocks = n_chunks // SUBLANES
    assert n_blocks & (n_blocks - 1) == 0 and N_META == CHUNK
    octet = lambda o, b: (o, 0, 0)
    n_tiles = OCT_STATE // LANES
    scan_buf = pltpu.VMEM((n_tiles, SCAN_PAD + n_chunks, LANES), F32)
    block_buf = pltpu.VMEM((n_tiles, n_blocks // 2 + n_blocks, LANES), F32)
    return pl.pallas_call(
        functools.partial(_ssm_kernel, n_chunks=n_chunks),
        grid=(N_OCT, bsz),
        in_specs=[
            pl.BlockSpec((seq, LANES), lambda o, b: (b, o)),
            pl.BlockSpec((N_META, LANES), lambda o, b: (0, o)),
            pl.BlockSpec((2, 1, LANES, 2 * SSM_STATE), lambda o, b: (0, o, 0, 0)),
            pl.BlockSpec((CHUNK, 1, LANES, 4 * SSM_STATE), lambda o, b: (0, o, 0, 0)),
            pl.BlockSpec((CHUNK + 1, SSM_STATE, LANES), lambda o, b: (0, 0, o)),
            pl.BlockSpec((CHUNK + 1, SSM_STATE, LANES), lambda o, b: (0, 0, o)),
            pl.BlockSpec((1,) + loc.shape[1:], lambda o, b: (o, 0, 0, 0, 0)),
            pl.BlockSpec((1,) + blk.shape[1:], octet),
            pl.BlockSpec((1, 1, LANES), octet),
        ],
        out_specs=pl.BlockSpec((seq, LANES), lambda o, b: (b, o)),
        out_shape=jax.ShapeDtypeStruct((bsz * seq, SSM_WIDTH), F32),
        scratch_shapes=[
            pltpu.VMEM((CHUNK // 2, MXU_DIM, MXU_DIM), BF16),
            pltpu.VMEM((OCT_IN, OCT_STATE), BF16),
            pltpu.VMEM((OCT_STATE, OCT_IN), BF16),
            pltpu.VMEM((CHUNK // 2, n_chunks, MXU_DIM), BF16), scan_buf, scan_buf, block_buf, block_buf,
            pltpu.VMEM((OCT // 2, 2, n_chunks, LANES), F32),
            pltpu.VMEM((CHUNK // 2, n_chunks, MXU_DIM), F32)],
        compiler_params=pltpu.CompilerParams(
            dimension_semantics=("arbitrary", "arbitrary"), vmem_limit_bytes=VMEM_LIMIT),
        name="s5_chunked",
    )(u, u_meta, bb2, bc, cc_re, cc_im, loc, blk, dp)


def _ssm_weights(a_re, a_im, log_dt, b_re, b_im, c_re, c_im, d_skip, n_chunks):
    dt = jnp.exp(log_dt)[:, None]
    lam_re, lam_im = a_re * dt, a_im * dt

    def power(n, transposed=False):
        n = n[:, None, None]
        l_re, l_im = (lam_re.T, lam_im.T) if transposed else (lam_re, lam_im)
        mag = jnp.exp(n * l_re)
        return mag * jnp.cos(n * l_im), mag * jnp.sin(n * l_im)

    ab_re, ab_im = jnp.exp(lam_re) * jnp.cos(lam_im), jnp.exp(lam_re) * jnp.sin(lam_im)
    den = a_re * a_re + a_im * a_im
    nr, ni = ab_re - 1.0, ab_im
    f_re = (nr * a_re + ni * a_im) / den
    f_im = (ni * a_re - nr * a_im) / den
    bb_re = f_re[..., None] * b_re - f_im[..., None] * b_im
    bb_im = f_re[..., None] * b_im + f_im[..., None] * b_re

    p_re, p_im = power(jnp.arange(CHUNK + 1, dtype=F32))

    ct_re, ct_im = c_re.transpose(2, 0, 1)[None], c_im.transpose(2, 0, 1)[None]
    pt_re, pt_im = (t[..., None] for t in power(jnp.arange(CHUNK + 1, dtype=F32), transposed=True))
    ca_re = ct_re * pt_re - ct_im * pt_im
    ca_im = ct_re * pt_im + ct_im * pt_re
    cc_re = ca_re.reshape(CHUNK + 1, SSM_STATE, SSM_WIDTH)
    cc_im = ca_im.reshape(CHUNK + 1, SSM_STATE, SSM_WIDTH)

    r_re, r_im = p_re[CHUNK - 1::-1][:, :, None, :], p_im[CHUNK - 1::-1][:, :, None, :]
    bt_re, bt_im = bb_re.transpose(0, 2, 1)[None], bb_im.transpose(0, 2, 1)[None]
    bb2 = jnp.stack([jnp.concatenate([t, t], axis=-1).reshape(N_OCT, LANES, 2 * SSM_STATE) for t in (bt_re, bt_im)])
    bs_re = r_re * bt_re - r_im * bt_im
    bs_im = r_re * bt_im + r_im * bt_re
    bc = jnp.concatenate([bs_re, bs_re, bs_im, bs_im], axis=-1).reshape(CHUNK, N_OCT, LANES, 4 * SSM_STATE)

    rows = jnp.arange(SUBLANES)
    strides = 2 ** jnp.arange(LOCAL_STEPS)
    in_block = (rows[None, :] >= strides[:, None])[:, :, None, None]
    blk_steps = (n_chunks // SUBLANES).bit_length() - 1

    def pack(m):
        m = m.reshape(m.shape[:-2] + (N_OCT, OCT_STATE // 2))
        return jnp.moveaxis(m, -2, 0)

    loc, blk = [], []
    for carry, block_stride in zip(power(CHUNK * (rows + 1.0)), power(CHUNK * SUBLANES * 2.0 ** jnp.arange(blk_steps))):
        steps = jnp.where(in_block, carry[strides - 1][:, None], 0.0)
        loc.append(pack(jnp.concatenate([steps, carry[None]], axis=0)))
        blk.append(pack(block_stride))
    loc = jnp.stack(loc, axis=2)
    blk = jnp.concatenate(blk, axis=1)

    dp = d_skip.reshape(N_OCT, 1, LANES)
    return bb2, bc, cc_re, cc_im, loc, blk, dp


def _out_ffn_kernel(x_ref, o_ref, y_ref, wglu_ref, bglu_ref, sg_ref, wout_ref, pmg_ref, pfg_ref,
                    wg_ref, wu_ref, wd_ref, pog_ref, out_ref):
    y = y_ref[...]
    gate = jnp.dot(y.astype(BF16), wglu_ref[...], preferred_element_type=F32) + bglu_ref[...]
    y = _rms(y * jax.nn.sigmoid(gate), sg_ref[...]).astype(BF16)
    mix = (jnp.dot(o_ref[...], wout_ref[:ATTN_WIDTH, :], preferred_element_type=F32)
           + jnp.dot(y, wout_ref[ATTN_WIDTH:, :], preferred_element_type=F32))
    h1 = x_ref[...] + _rms(mix, pmg_ref[...])
    h2 = _rms(h1, pfg_ref[...]).astype(BF16)
    g = jnp.dot(h2, wg_ref[...], preferred_element_type=F32)
    up = jnp.dot(h2, wu_ref[...], preferred_element_type=F32)
    f = (g * jax.nn.sigmoid(g) * up).astype(BF16)
    f = jnp.dot(f, wd_ref[...], preferred_element_type=F32)
    out_ref[...] = h1 + _rms(f, pog_ref[...])


def _out_ffn(x2d, o, y, wglu, bglu, sg, wout, pmg, pfg, wg, wu, wd, pog, tm):
    n = x2d.shape[0]
    row = lambda i: (i, 0)
    const = lambda i: (0, 0)

    def resident(shape):
        return pl.BlockSpec(shape, const, pipeline_mode=pl.Buffered(1))

    return pl.pallas_call(
        _out_ffn_kernel,
        grid=(n // tm,),
        in_specs=[
            pl.BlockSpec((tm, D_MODEL), row),
            pl.BlockSpec((tm, ATTN_WIDTH), row),
            pl.BlockSpec((tm, SSM_WIDTH), row),
            resident((SSM_WIDTH, SSM_WIDTH)),
            resident((1, SSM_WIDTH)),
            resident((1, SSM_WIDTH)),
            resident((D_MODEL, D_MODEL)),
            resident((1, D_MODEL)),
            resident((1, D_MODEL)),
            resident((D_MODEL, D_FF)),
            resident((D_MODEL, D_FF)),
            resident((D_FF, D_MODEL)),
            resident((1, D_MODEL)),
        ],
        out_specs=pl.BlockSpec((tm, D_MODEL), row),
        out_shape=jax.ShapeDtypeStruct((n, D_MODEL), F32),
        compiler_params=pltpu.CompilerParams(
            dimension_semantics=("arbitrary",), vmem_limit_bytes=VMEM_LIMIT),
        name="out_ffn",
    )(x2d, o, y, wglu, bglu, sg, wout, pmg, pfg, wg, wu, wd, pog)


def _rope_angles(pos):
    d = jnp.arange(LANES) % QK_DIM
    inv_freq = ROPE_THETA ** (-(2 * (d % (ROT_DIM // 2))).astype(F32) / ROT_DIM)
    ang = pos[:, None] * jnp.where(d < ROT_DIM, inv_freq, 0.0)[None, :]
    return jnp.cos(ang), jnp.sin(ang)


def _scale_w_in(w):
    scale = math.log2(math.e) / math.sqrt(QK_DIM)
    col_scale = jnp.where(jnp.arange(w.shape[1]) < ATTN_WIDTH, scale, 1.0).astype(F32)
    return (w * col_scale[None, :]).astype(BF16)


def kernel(x, meta, pre_mix_g, w_in, lambda_q1, lambda_k1, lambda_q2, lambda_k2, subln_g, a_re, a_im, log_dt,
           b_re, b_im, c_re, c_im, d_skip, w_glu, b_glu, ssm_out_g, w_out, post_mix_g, pre_ffn_g, w_gate,
           w_up, w_down, post_ffn_g):
    bsz, seq, _ = x.shape
    n = bsz * seq
    x2d = x.reshape(n, D_MODEL)
    row = lambda t: t[0].reshape(1, -1)

    w_in_p = _scale_w_in(w_in[0])
    g0 = row(pre_mix_g)
    q, k, v, u = _in_proj(x2d, g0, w_in_p, N_META, seq, IN_ROW_TILE)
    _, k_m, v_m, u_m = _in_proj(meta, g0, w_in_p, 0, N_META, N_META)

    pad_meta = lambda t: jnp.pad(t, ((0, LANES - N_META), (0, 0)))
    o = _attention(q, k, v, pad_meta(k_m), pad_meta(v_m), lambda_q1, lambda_k1, lambda_q2, lambda_k2,
                   row(subln_g), bsz, seq)

    ssm_w = _ssm_weights(a_re[0], a_im[0], log_dt[0], b_re[0], b_im[0], c_re[0], c_im[0], d_skip[0],
                         seq // CHUNK)
    y = _ssm(u, u_m, *ssm_w, bsz, seq)

    out = _out_ffn(x2d, o, y, w_glu[0].astype(BF16), row(b_glu), row(ssm_out_g), w_out[0].astype(BF16),
                   row(post_mix_g), row(pre_ffn_g), w_gate[0].astype(BF16), w_up[0].astype(BF16),
                   w_down[0].astype(BF16), row(post_ffn_g), ROW_TILE)
    return out.reshape(bsz, seq, D_MODEL)
```

```python
import functools
import math

import jax
import jax.numpy as jnp
from jax import lax
from jax.experimental import pallas as pl
from jax.experimental.pallas import tpu as pltpu

D_MODEL = 1024
N_META = 16
N_HEADS = 4
QK_DIM = 64
V_DIM = 128
ROT_DIM = 16
ROPE_THETA = 500000.0
SSM_GROUP = 16
N_GROUPS = 32
SSM_STATE = 64
SSM_WIDTH = 512
ATTN_WIDTH = 512
D_FF = 2816
EPS = 1e-6
LAM_INIT = 0.8 - 0.6 * math.exp(-0.3 * 0)

LANES = 128
SUBLANES = 8
MXU_DIM = 256
CHUNK = 16
OCT = LANES // SSM_GROUP
N_OCT = N_GROUPS // OCT
OCT_IN = CHUNK * LANES
OCT_STATE = OCT * 2 * SSM_STATE
SCAN_PAD = SUBLANES
LOCAL_STEPS = 3

ROW_TILE = 512
IN_ROW_TILE = 1024
Q_TILE = 512
KV_TILE = 512
ATTN_UNROLL = 4
VMEM_LIMIT = 56 * 1024 * 1024

F32 = jnp.float32
BF16 = jnp.bfloat16


def _rms(x, g):
    return x * lax.rsqrt(jnp.mean(x * x, axis=-1, keepdims=True) + EPS) * g


def _in_proj_kernel(x_ref, g_ref, w_ref, cr_ref, sr_ref, cb_ref, sb_ref, q_ref, k_ref, v_ref, u_ref,
                    *, k_transposed):
    h = _rms(x_ref[...], g_ref[...]).astype(BF16)
    proj = jnp.dot(h, w_ref[...], preferred_element_type=F32)
    cos = cb_ref[0] * cr_ref[...] - sb_ref[0] * sr_ref[...]
    sin = sb_ref[0] * cr_ref[...] + cb_ref[0] * sr_ref[...]
    d = lax.broadcasted_iota(jnp.int32, (1, LANES), 1) % QK_DIM
    sa = sin * jnp.where(d < ROT_DIM // 2, -1.0, 0.0)
    sb = sin * jnp.where((d >= ROT_DIM // 2) & (d < ROT_DIM), 1.0, 0.0)

    def rope(t):
        return (t * cos + pltpu.roll(t, LANES - ROT_DIM // 2, axis=1) * sa
                + pltpu.roll(t, ROT_DIM // 2, axis=1) * sb)

    for c in range(ATTN_WIDTH // LANES):
        sl = slice(c * LANES, (c + 1) * LANES)
        q_ref[:, sl] = rope(proj[:, c * LANES:(c + 1) * LANES]).astype(BF16)
        k_c = rope(proj[:, ATTN_WIDTH + c * LANES:ATTN_WIDTH + (c + 1) * LANES])
        if k_transposed:
            k_ref[sl, :] = k_c.T.astype(BF16)
        else:
            k_ref[:, sl] = k_c.astype(BF16)
    v_ref[...] = proj[:, 2 * ATTN_WIDTH:3 * ATTN_WIDTH].astype(BF16)
    u_ref[...] = proj[:, 3 * ATTN_WIDTH:]


def _in_proj(x2d, g, w, first_pos, seq, tm, k_transposed):
    n = x2d.shape[0]
    token_major = pl.BlockSpec((tm, ATTN_WIDTH), lambda i: (i, 0))
    k_spec = pl.BlockSpec((ATTN_WIDTH, tm), lambda i: (0, i)) if k_transposed else token_major
    k_shape = (ATTN_WIDTH, n) if k_transposed else (n, ATTN_WIDTH)
    tiles_per_seq = seq // tm
    cos_r, sin_r = _rope_angles(jnp.arange(tm, dtype=F32))
    cos_b, sin_b = _rope_angles(first_pos + tm * jnp.arange(tiles_per_seq, dtype=F32))
    row = lambda i: (i, 0)
    const = lambda i: (0, 0)
    base = pl.BlockSpec((1, 1, LANES), lambda i: (i % tiles_per_seq, 0, 0))
    return pl.pallas_call(
        functools.partial(_in_proj_kernel, k_transposed=k_transposed),
        grid=(n // tm,),
        in_specs=[
            pl.BlockSpec((tm, D_MODEL), row),
            pl.BlockSpec((1, D_MODEL), const),
            pl.BlockSpec((D_MODEL, 4 * ATTN_WIDTH), const),
            pl.BlockSpec((tm, LANES), const),
            pl.BlockSpec((tm, LANES), const),
            base, base,
        ],
        out_specs=[token_major, k_spec, token_major, token_major],
        out_shape=[jax.ShapeDtypeStruct((n, ATTN_WIDTH), BF16), jax.ShapeDtypeStruct(k_shape, BF16),
                   jax.ShapeDtypeStruct((n, ATTN_WIDTH), BF16), jax.ShapeDtypeStruct((n, SSM_WIDTH), F32)],
        compiler_params=pltpu.CompilerParams(
            dimension_semantics=("arbitrary",), vmem_limit_bytes=VMEM_LIMIT),
        name="in_proj",
    )(x2d, g, w, cos_r, sin_r, cos_b[:, None, :], sin_b[:, None, :])


def _attn_kernel(*refs):
    n_pairs = refs[0].shape[0] // (2 * Q_TILE)

    def q_pair(sp, carry):
        _attn_q_pair(sp, *refs)
        return carry

    lax.fori_loop(0, n_pairs, q_pair, 0)


def _attn_q_pair(sp, q1_ref, q2_ref, k1_ref, k2_ref, v_ref, km1_ref, km2_ref, vm_ref,
                 lq1_ref, lk1_ref, lq2_ref, lk2_ref, sg_ref,
                 o_ref, m_ref, l_ref, acc_ref, p_ref):
    own = lax.broadcasted_iota(jnp.int32, (Q_TILE, LANES), 1) // QK_DIM == pl.program_id(1) % 2
    q_rows = [pl.ds(pl.multiple_of((2 * sp + c) * Q_TILE, Q_TILE), Q_TILE) for c in range(2)]
    qs = [tuple(jnp.where(own, r[rows, :], jnp.zeros((Q_TILE, LANES), BF16)) for r in (q1_ref, q2_ref))
          for rows in q_rows]
    k_refs, km_refs = (k1_ref, k2_ref), (km1_ref, km2_ref)
    nt = (((1,), (1,)), ((), ()))
    wide = (Q_TILE, LANES)

    def block(c, k, mask, slot, prev):
        n_col = k[0].shape[1] // LANES
        for i in range(2):
            s = jnp.dot(qs[c][i], k[i], preferred_element_type=F32)
            if mask is not None:
                s = jnp.where(mask, s, -jnp.inf)
            cols = [s[:, j * LANES:(j + 1) * LANES] for j in range(n_col)]
            m_tile = functools.reduce(jnp.maximum, cols)
            m_new = jnp.broadcast_to(jnp.max(m_tile, axis=1, keepdims=True), wide)
            if prev is not None:
                m_old = m_ref[c, i]
                m_new = jnp.maximum(m_old, m_new)
                alpha = jnp.exp2(m_old - m_new)
            m_ref[c, i] = m_new
            ps = [jnp.exp2(col - m_new) for col in cols]
            l_tile = functools.reduce(jnp.add, ps)
            l_ref[c, i] = l_tile if prev is None else alpha * l_ref[c, i] + l_tile
            for j in range(n_col):
                p_ref[c, slot, i, :, j * LANES:(j + 1) * LANES] = ps[j].astype(BF16)
            if prev is not None:
                slot_prev, v_prev = prev
                pv = jnp.dot(p_ref[c, slot_prev, i], v_prev, preferred_element_type=F32)
                acc_ref[c, i] = (acc_ref[c, i] + pv) * alpha

    def tile(ref, t):
        return ref[pl.ds(pl.multiple_of(t * KV_TILE, KV_TILE), KV_TILE), :]

    def ktile(t):
        return tuple(r[:, pl.ds(pl.multiple_of(t * KV_TILE, KV_TILE), KV_TILE)] for r in k_refs)

    acc_ref[...] = jnp.zeros(acc_ref.shape, F32)
    d0 = 2 * sp
    causal = (lax.broadcasted_iota(jnp.int32, (Q_TILE, KV_TILE), 1)
              <= lax.broadcasted_iota(jnp.int32, (Q_TILE, KV_TILE), 0))
    k_d0 = ktile(d0)
    block(0, k_d0, causal, 0, None)
    block(1, k_d0, None, 0, None)
    block(1, ktile(d0 + 1), causal, 1, (0, tile(v_ref, d0)))
    pending = (0, 1)

    def earlier(t0, n):
        for d in range(n):
            k_t = ktile(t0 + d)
            for c in range(2):
                before = jnp.where(t0 == 0, d0 + c, t0 - 1) if d == 0 else t0 + d - 1
                src = pending[c] if d % 2 == 0 else 1 - pending[c]
                block(c, k_t, None, 1 - src, (src, tile(v_ref, before)))

    def unrolled(jj, carry):
        earlier(ATTN_UNROLL * jj, ATTN_UNROLL)
        return carry

    lax.fori_loop(0, d0 // ATTN_UNROLL, unrolled, 0)
    for n in (ATTN_UNROLL // 2, ATTN_UNROLL // 4):
        if n >= 2:
            done = (d0 // (2 * n)) * (2 * n)
            pl.when(d0 - done >= n)(functools.partial(earlier, done, n))

    meta_mask = lax.broadcasted_iota(jnp.int32, wide, 1) < N_META
    k_meta = tuple(r[...] for r in km_refs)
    for c in range(2):
        before = jnp.where(sp == 0, d0 + c, d0 - 1)
        block(c, k_meta, meta_mask, 1 - pending[c], (pending[c], tile(v_ref, before)))
    for c in range(2):
        for i in range(2):
            acc_ref[c, i] = acc_ref[c, i] + jnp.dot(p_ref[c, 1 - pending[c], i, :, :LANES], vm_ref[...],
                                                    preferred_element_type=F32)

    lam = (jnp.exp(jnp.sum(lq1_ref[...] * lk1_ref[...], axis=1, keepdims=True))
           - jnp.exp(jnp.sum(lq2_ref[...] * lk2_ref[...], axis=1, keepdims=True)) + LAM_INIT)
    for c in range(2):
        l1 = jnp.sum(l_ref[c, 0], axis=1, keepdims=True)
        l2 = jnp.sum(l_ref[c, 1], axis=1, keepdims=True)
        o = acc_ref[c, 0] / l1 - lam * (acc_ref[c, 1] / l2)
        o_ref[q_rows[c], :] = (_rms(o, sg_ref[...]) * (1.0 - LAM_INIT)).astype(BF16)


def _attention(q, k, v, km, vm, lq1, lk1, lq2, lk2, sg, bsz, seq):
    assert Q_TILE == KV_TILE and seq % (2 * Q_TILE) == 0
    pairs = N_HEADS // 2
    kvmap = lambda b, h: (b, h)
    map1 = pl.BlockSpec((seq, LANES), lambda b, h: (b, h // 2))
    map2 = pl.BlockSpec((seq, LANES), lambda b, h: (b, pairs + h // 2))
    kmap1 = pl.BlockSpec((LANES, seq), lambda b, h: (h // 2, b))
    kmap2 = pl.BlockSpec((LANES, seq), lambda b, h: (pairs + h // 2, b))
    meta1 = pl.BlockSpec((LANES, LANES), lambda b, h: (h // 2, 0))
    meta2 = pl.BlockSpec((LANES, LANES), lambda b, h: (pairs + h // 2, 0))
    const = lambda b, h: (0, 0)
    vec = pl.BlockSpec((1, QK_DIM), const)
    return pl.pallas_call(
        _attn_kernel,
        grid=(bsz, N_HEADS),
        in_specs=[
            map1, map2, kmap1, kmap2,
            pl.BlockSpec((seq, LANES), kvmap),
            meta1, meta2,
            pl.BlockSpec((LANES, LANES), lambda b, h: (0, h)),
            vec, vec, vec, vec,
            pl.BlockSpec((1, V_DIM), const),
        ],
        out_specs=pl.BlockSpec((seq, LANES), kvmap),
        out_shape=jax.ShapeDtypeStruct((bsz * seq, ATTN_WIDTH), BF16),
        scratch_shapes=[
            pltpu.VMEM((2, 2, Q_TILE, LANES), F32),
            pltpu.VMEM((2, 2, Q_TILE, LANES), F32),
            pltpu.VMEM((2, 2, Q_TILE, V_DIM), F32),
            pltpu.VMEM((2, 2, 2, Q_TILE, KV_TILE), BF16),
        ],
        compiler_params=pltpu.CompilerParams(
            dimension_semantics=("arbitrary", "arbitrary"), vmem_limit_bytes=VMEM_LIMIT),
        name="diff_attention",
    )(q, q, k, k, v, km, km, vm, lq1, lk1, lq2, lk2, sg)


def _ssm_kernel(u_ref, um_ref, bb_ref, bc_ref, ccr_ref, cci_ref, loc_ref, blk_ref, d_ref, y_ref,
                tz_ref, bs_ref, cs_ref, ub_ref, buf_a, buf_b, cb_a, cb_b, rep_ref, yt_ref, *, n_chunks):
    n_tiles = OCT_STATE // LANES
    n_blocks = n_chunks // SUBLANES
    blk_steps = n_blocks.bit_length() - 1
    blk_pad = cb_a.shape[1] - n_blocks

    @pl.when((pl.program_id(0) == 0) & (pl.program_id(1) == 0))
    def _():
        for buf, pad in ((buf_a, SCAN_PAD), (buf_b, SCAN_PAD), (cb_a, blk_pad), (cb_b, blk_pad)):
            buf[:, :pad, :] = jnp.zeros((n_tiles, pad, LANES), F32)

    @pl.when(pl.program_id(1) == 0)
    def _():
        def group_of(shape, axis):
            return (lax.broadcasted_iota(jnp.int32, shape, axis) // SSM_GROUP) % OCT

        def split(x):
            head = x.astype(BF16)
            return head, (x - head.astype(F32)).astype(BF16)

        wide_shape = (LANES, OCT * SSM_STATE)
        own_cols = group_of(wide_shape, 0) == lax.broadcasted_iota(jnp.int32, wide_shape, 1) // SSM_STATE
        bbar = [split(jnp.where(own_cols, jnp.concatenate([bb_ref[part, 0]] * (OCT // 2), axis=1), 0.0))
                for part in range(2)]
        lag_blk = []
        for n in range(0, CHUNK, 2):
            terms = []
            for part, c_ref in enumerate((ccr_ref, cci_ref)):
                c_n = jnp.concatenate([c_ref[n], c_ref[n + 1]], axis=1)
                g_lane = group_of(c_n.shape, 1)
                ca_head, ca_rest = split(jnp.concatenate(
                    [jnp.where(g_lane == g, c_n, 0.0) for g in range(OCT)], axis=0))
                b_head, b_rest = bbar[part]
                terms.append(jnp.dot(b_head, ca_head, preferred_element_type=F32)
                             + jnp.dot(b_head, ca_rest, preferred_element_type=F32)
                             + jnp.dot(b_rest, ca_head, preferred_element_type=F32))
            both = (terms[0] - terms[1]).astype(BF16)
            lag_blk += [both[:, :LANES], both[:, LANES:]]
        for m in range(CHUNK // 2):
            tz_ref[m, :LANES, :LANES] = lag_blk[2 * m]
            tz_ref[m, :LANES, LANES:] = lag_blk[2 * m + 1]
            tz_ref[m, LANES:, :LANES] = lag_blk[2 * m - 1] if m else jnp.zeros((LANES, LANES), BF16)
            tz_ref[m, LANES:, LANES:] = lag_blk[2 * m]
        bc = bc_ref[:, 0].reshape(OCT_IN, 4 * SSM_STATE)
        g_row = group_of(bc.shape, 0)
        w_col = (lax.broadcasted_iota(jnp.int32, bc.shape, 1) // SSM_STATE) % 2
        for qq in range(OCT // 2):
            bs_ref[:, qq * MXU_DIM:(qq + 1) * MXU_DIM] = jnp.where(g_row == 2 * qq + w_col, bc, 0.0).astype(BF16)
        g_col = group_of((SSM_STATE, LANES), 1)
        for s in range(CHUNK):
            c_s = (ccr_ref[s + 1], -cci_ref[s + 1])
            for qq in range(OCT // 2):
                for part in range(2):
                    for w in range(2):
                        r0 = qq * MXU_DIM + part * LANES + w * SSM_STATE
                        cs_ref[r0:r0 + SSM_STATE, s * LANES:(s + 1) * LANES] = jnp.where(
                            g_col == 2 * qq + w, c_s[part], 0.0).astype(BF16)

    def slab(ref, t, rows):
        return ref[pl.ds(t, rows, stride=CHUNK), :]

    for t in range(CHUNK):
        ub_ref[t // 2, :, (t % 2) * LANES:(t % 2 + 1) * LANES] = slab(u_ref, t, n_chunks).astype(BF16)

    z = jnp.dot(ub_ref[0], bs_ref[:MXU_DIM, :], preferred_element_type=F32)
    for tt in range(1, CHUNK // 2):
        z = z + jnp.dot(ub_ref[tt], bs_ref[tt * MXU_DIM:(tt + 1) * MXU_DIM, :], preferred_element_type=F32)
    um = jnp.concatenate([jnp.broadcast_to(um_ref[t:t + 1, :], (SUBLANES, LANES)) for t in range(CHUNK)], axis=1)
    zm = jnp.dot(um.astype(BF16), bs_ref[...], preferred_element_type=F32)[0:1, :]
    for ct in range(n_tiles):
        cols = slice(ct * LANES, (ct + 1) * LANES)
        buf_a[ct, SCAN_PAD:, :] = z[:, cols]
        for buf, pad in ((buf_a, SCAN_PAD), (buf_b, SCAN_PAD), (cb_a, blk_pad), (cb_b, blk_pad)):
            buf[ct, pad - 1:pad, :] = zm[:, cols]

    def within_chunk(ss):
        acc = jnp.dot(ub_ref[0], tz_ref[ss], preferred_element_type=F32)
        for tt in range(1, ss + 1):
            acc = acc + jnp.dot(ub_ref[tt], tz_ref[ss - tt], preferred_element_type=F32)
        yt_ref[ss] = acc

    assert LOCAL_STEPS + blk_steps >= CHUNK // 2 and (1 << LOCAL_STEPS) == SUBLANES
    todo = list(range(CHUNK // 2 - 1, -1, -1))
    blocked = (n_blocks, SUBLANES, LANES)

    def lanes_of(pair):
        return slice(pair * LANES, (pair + 1) * LANES)

    def combine(x, p, ar, ai):
        return x[0] + ar * p[0] - ai * p[1], x[1] + ar * p[1] + ai * p[0]

    src, dst = buf_a, buf_b
    for step in range(LOCAL_STEPS):
        shift = 1 << step
        within_chunk(todo.pop(0))
        for pair in range(OCT // 2):
            ar, ai = (loc_ref[0, step, part, :, lanes_of(pair)] for part in range(2))
            x = [src[2 * pair + part, SCAN_PAD:, :].reshape(blocked) for part in range(2)]
            p = [src[2 * pair + part, SCAN_PAD - shift:SCAN_PAD - shift + n_chunks, :].reshape(blocked)
                 for part in range(2)]
            for part, val in enumerate(combine(x, p, ar, ai)):
                dst[2 * pair + part, SCAN_PAD:, :] = val.reshape(n_chunks, LANES)
        src, dst = dst, src

    for ct in range(n_tiles):
        cb_a[ct, blk_pad:, :] = src[ct, pl.ds(SCAN_PAD + SUBLANES - 1, n_blocks, stride=SUBLANES), :]
    csrc, cdst = cb_a, cb_b
    for step in range(blk_steps):
        shift = 1 << step
        if todo:
            within_chunk(todo.pop(0))
        for pair in range(OCT // 2):
            ar = blk_ref[0, step:step + 1, lanes_of(pair)]
            ai = blk_ref[0, blk_steps + step:blk_steps + step + 1, lanes_of(pair)]
            x = [csrc[2 * pair + part, blk_pad:, :] for part in range(2)]
            p = [csrc[2 * pair + part, blk_pad - shift:blk_pad - shift + n_blocks, :] for part in range(2)]
            for part, val in enumerate(combine(x, p, ar, ai)):
                cdst[2 * pair + part, blk_pad:, :] = val
        csrc, cdst = cdst, csrc

    for pair in range(OCT // 2):
        for part in range(2):
            entering = csrc[2 * pair + part, blk_pad - 1:blk_pad - 1 + n_blocks, :]
            for row in range(SUBLANES):
                rep_ref[pair, part, pl.ds(row, n_blocks, stride=SUBLANES), :] = entering
        ar, ai = (loc_ref[0, LOCAL_STEPS, part, :, lanes_of(pair)] for part in range(2))
        x = [src[2 * pair + part, SCAN_PAD:, :].reshape(blocked) for part in range(2)]
        p = [rep_ref[pair, part].reshape(blocked) for part in range(2)]
        for part, val in enumerate(combine(x, p, ar, ai)):
            dst[2 * pair + part, SCAN_PAD:, :] = val.reshape(n_chunks, LANES)

    start_state = jnp.concatenate(
        [dst[ct, SCAN_PAD - 1:SCAN_PAD - 1 + n_chunks, :] for ct in range(n_tiles)], axis=1).astype(BF16)
    for ss in range(CHUNK // 2):
        cols = slice(ss * MXU_DIM, (ss + 1) * MXU_DIM)
        acc = yt_ref[ss] + jnp.dot(start_state, cs_ref[:, cols], preferred_element_type=F32)
        for half in range(2):
            t = 2 * ss + half
            y = acc[:, half * LANES:(half + 1) * LANES] + d_ref[0] * slab(u_ref, t, n_chunks)
            y_ref[pl.ds(t, n_chunks, stride=CHUNK), :] = 0.5 * y * (1.0 + lax.erf(y * (1.0 / math.sqrt(2.0))))


def _ssm(u, u_meta, bb2, bc, cc_re, cc_im, loc, blk, dp, bsz, seq):
    n_chunks = seq // CHUNK
    n_blocks = n_chunks // SUBLANES
    assert n_blocks & (n_blocks - 1) == 0 and N_META == CHUNK
    octet = lambda o, b: (o, 0, 0)
    n_tiles = OCT_STATE // LANES
    scan_buf = pltpu.VMEM((n_tiles, SCAN_PAD + n_chunks, LANES), F32)
    block_buf = pltpu.VMEM((n_tiles, n_blocks // 2 + n_blocks, LANES), F32)
    return pl.pallas_call(
        functools.partial(_ssm_kernel, n_chunks=n_chunks),
        grid=(N_OCT, bsz),
        in_specs=[
            pl.BlockSpec((seq, LANES), lambda o, b: (b, o)),
            pl.BlockSpec((N_META, LANES), lambda o, b: (0, o)),
            pl.BlockSpec((2, 1, LANES, 2 * SSM_STATE), lambda o, b: (0, o, 0, 0)),
            pl.BlockSpec((CHUNK, 1, LANES, 4 * SSM_STATE), lambda o, b: (0, o, 0, 0)),
            pl.BlockSpec((CHUNK + 1, SSM_STATE, LANES), lambda o, b: (0, 0, o)),
            pl.BlockSpec((CHUNK + 1, SSM_STATE, LANES), lambda o, b: (0, 0, o)),
            pl.BlockSpec((1,) + loc.shape[1:], lambda o, b: (o, 0, 0, 0, 0)),
            pl.BlockSpec((1,) + blk.shape[1:], octet),
            pl.BlockSpec((1, 1, LANES), octet),
        ],
        out_specs=pl.BlockSpec((seq, LANES), lambda o, b: (b, o)),
        out_shape=jax.ShapeDtypeStruct((bsz * seq, SSM_WIDTH), F32),
        scratch_shapes=[
            pltpu.VMEM((CHUNK // 2, MXU_DIM, MXU_DIM), BF16),
            pltpu.VMEM((OCT_IN, OCT_STATE), BF16),
            pltpu.VMEM((OCT_STATE, OCT_IN), BF16),
            pltpu.VMEM((CHUNK // 2, n_chunks, MXU_DIM), BF16), scan_buf, scan_buf, block_buf, block_buf,
            pltpu.VMEM((OCT // 2, 2, n_chunks, LANES), F32),
            pltpu.VMEM((CHUNK // 2, n_chunks, MXU_DIM), F32)],
        compiler_params=pltpu.CompilerParams(
            dimension_semantics=("arbitrary", "arbitrary"), vmem_limit_bytes=VMEM_LIMIT),
        name="s5_chunked",
    )(u, u_meta, bb2, bc, cc_re, cc_im, loc, blk, dp)


def _ssm_weights(a_re, a_im, log_dt, b_re, b_im, c_re, c_im, d_skip, n_chunks):
    dt = jnp.exp(log_dt)[:, None]
    lam_re, lam_im = a_re * dt, a_im * dt

    def power(n, transposed=False):
        n = n[:, None, None]
        l_re, l_im = (lam_re.T, lam_im.T) if transposed else (lam_re, lam_im)
        mag = jnp.exp(n * l_re)
        return mag * jnp.cos(n * l_im), mag * jnp.sin(n * l_im)

    ab_re, ab_im = jnp.exp(lam_re) * jnp.cos(lam_im), jnp.exp(lam_re) * jnp.sin(lam_im)
    den = a_re * a_re + a_im * a_im
    nr, ni = ab_re - 1.0, ab_im
    f_re = (nr * a_re + ni * a_im) / den
    f_im = (ni * a_re - nr * a_im) / den
    bb_re = f_re[..., None] * b_re - f_im[..., None] * b_im
    bb_im = f_re[..., None] * b_im + f_im[..., None] * b_re

    p_re, p_im = power(jnp.arange(CHUNK + 1, dtype=F32))

    ct_re, ct_im = c_re.transpose(2, 0, 1)[None], c_im.transpose(2, 0, 1)[None]
    pt_re, pt_im = (t[..., None] for t in power(jnp.arange(CHUNK + 1, dtype=F32), transposed=True))
    ca_re = ct_re * pt_re - ct_im * pt_im
    ca_im = ct_re * pt_im + ct_im * pt_re
    cc_re = ca_re.reshape(CHUNK + 1, SSM_STATE, SSM_WIDTH)
    cc_im = ca_im.reshape(CHUNK + 1, SSM_STATE, SSM_WIDTH)

    r_re, r_im = p_re[CHUNK - 1::-1][:, :, None, :], p_im[CHUNK - 1::-1][:, :, None, :]
    bt_re, bt_im = bb_re.transpose(0, 2, 1)[None], bb_im.transpose(0, 2, 1)[None]
    bb2 = jnp.stack([jnp.concatenate([t, t], axis=-1).reshape(N_OCT, LANES, 2 * SSM_STATE) for t in (bt_re, bt_im)])
    bs_re = r_re * bt_re - r_im * bt_im
    bs_im = r_re * bt_im + r_im * bt_re
    bc = jnp.concatenate([bs_re, bs_re, bs_im, bs_im], axis=-1).reshape(CHUNK, N_OCT, LANES, 4 * SSM_STATE)

    rows = jnp.arange(SUBLANES)
    strides = 2 ** jnp.arange(LOCAL_STEPS)
    in_block = (rows[None, :] >= strides[:, None])[:, :, None, None]
    blk_steps = (n_chunks // SUBLANES).bit_length() - 1

    def pack(m):
        m = m.reshape(m.shape[:-2] + (N_OCT, OCT_STATE // 2))
        return jnp.moveaxis(m, -2, 0)

    loc, blk = [], []
    for carry, block_stride in zip(power(CHUNK * (rows + 1.0)), power(CHUNK * SUBLANES * 2.0 ** jnp.arange(blk_steps))):
        steps = jnp.where(in_block, carry[strides - 1][:, None], 0.0)
        loc.append(pack(jnp.concatenate([steps, carry[None]], axis=0)))
        blk.append(pack(block_stride))
    loc = jnp.stack(loc, axis=2)
    blk = jnp.concatenate(blk, axis=1)

    dp = d_skip.reshape(N_OCT, 1, LANES)
    return bb2, bc, cc_re, cc_im, loc, blk, dp


def _out_ffn_kernel(x_ref, o_ref, y_ref, wglu_ref, bglu_ref, sg_ref, wout_ref, pmg_ref, pfg_ref,
                    wg_ref, wu_ref, wd_ref, pog_ref, out_ref):
    y = y_ref[...]
    gate = jnp.dot(y.astype(BF16), wglu_ref[...], preferred_element_type=F32) + bglu_ref[...]
    y = _rms(y * jax.nn.sigmoid(gate), sg_ref[...]).astype(BF16)
    mix = (jnp.dot(o_ref[...], wout_ref[:ATTN_WIDTH, :], preferred_element_type=F32)
           + jnp.dot(y, wout_ref[ATTN_WIDTH:, :], preferred_element_type=F32))
    h1 = x_ref[...] + _rms(mix, pmg_ref[...])
    h2 = _rms(h1, pfg_ref[...]).astype(BF16)
    g = jnp.dot(h2, wg_ref[...], preferred_element_type=F32)
    up = jnp.dot(h2, wu_ref[...], preferred_element_type=F32)
    f = (g * jax.nn.sigmoid(g) * up).astype(BF16)
    f = jnp.dot(f, wd_ref[...], preferred_element_type=F32)
    out_ref[...] = h1 + _rms(f, pog_ref[...])


def _out_ffn(x2d, o, y, wglu, bglu, sg, wout, pmg, pfg, wg, wu, wd, pog, tm):
    n = x2d.shape[0]
    row = lambda i: (i, 0)
    const = lambda i: (0, 0)

    def resident(shape):
        return pl.BlockSpec(shape, const, pipeline_mode=pl.Buffered(1))

    return pl.pallas_call(
        _out_ffn_kernel,
        grid=(n // tm,),
        in_specs=[
            pl.BlockSpec((tm, D_MODEL), row),
            pl.BlockSpec((tm, ATTN_WIDTH), row),
            pl.BlockSpec((tm, SSM_WIDTH), row),
            resident((SSM_WIDTH, SSM_WIDTH)),
            resident((1, SSM_WIDTH)),
            resident((1, SSM_WIDTH)),
            resident((D_MODEL, D_MODEL)),
            resident((1, D_MODEL)),
            resident((1, D_MODEL)),
            resident((D_MODEL, D_FF)),
            resident((D_MODEL, D_FF)),
            resident((D_FF, D_MODEL)),
            resident((1, D_MODEL)),
        ],
        out_specs=pl.BlockSpec((tm, D_MODEL), row),
        out_shape=jax.ShapeDtypeStruct((n, D_MODEL), F32),
        compiler_params=pltpu.CompilerParams(
            dimension_semantics=("arbitrary",), vmem_limit_bytes=VMEM_LIMIT),
        name="out_ffn",
    )(x2d, o, y, wglu, bglu, sg, wout, pmg, pfg, wg, wu, wd, pog)


def _rope_angles(pos):
    d = jnp.arange(LANES) % QK_DIM
    inv_freq = ROPE_THETA ** (-(2 * (d % (ROT_DIM // 2))).astype(F32) / ROT_DIM)
    ang = pos[:, None] * jnp.where(d < ROT_DIM, inv_freq, 0.0)[None, :]
    return jnp.cos(ang), jnp.sin(ang)


def _scale_w_in(w):
    scale = math.log2(math.e) / math.sqrt(QK_DIM)
    col_scale = jnp.where(jnp.arange(w.shape[1]) < ATTN_WIDTH, scale, 1.0).astype(F32)
    return (w * col_scale[None, :]).astype(BF16)


def kernel(x, meta, pre_mix_g, w_in, lambda_q1, lambda_k1, lambda_q2, lambda_k2, subln_g, a_re, a_im, log_dt,
           b_re, b_im, c_re, c_im, d_skip, w_glu, b_glu, ssm_out_g, w_out, post_mix_g, pre_ffn_g, w_gate,
           w_up, w_down, post_ffn_g):
    bsz, seq, _ = x.shape
    n = bsz * seq
    x2d = x.reshape(n, D_MODEL)
    row = lambda t: t[0].reshape(1, -1)

    w_in_p = _scale_w_in(w_in[0])
    g0 = row(pre_mix_g)
    q, k, v, u = _in_proj(x2d, g0, w_in_p, N_META, seq, IN_ROW_TILE, True)
    _, k_m, v_m, u_m = _in_proj(meta, g0, w_in_p, 0, N_META, N_META, False)

    pad_meta = lambda t: jnp.pad(t, ((0, LANES - N_META), (0, 0)))
    o = _attention(q, k, v, pad_meta(k_m).T, pad_meta(v_m), lambda_q1, lambda_k1, lambda_q2, lambda_k2,
                   row(subln_g), bsz, seq)

    ssm_w = _ssm_weights(a_re[0], a_im[0], log_dt[0], b_re[0], b_im[0], c_re[0], c_im[0], d_skip[0],
                         seq // CHUNK)
    y = _ssm(u, u_m, *ssm_w, bsz, seq)

    out = _out_ffn(x2d, o, y, w_glu[0].astype(BF16), row(b_glu), row(ssm_out_g), w_out[0].astype(BF16),
                   row(post_mix_g), row(pre_ffn_g), w_gate[0].astype(BF16), w_up[0].astype(BF16),
                   w_down[0].astype(BF16), row(post_ffn_g), ROW_TILE)
    return out.reshape(bsz, seq, D_MODEL)
```

```python
import functools
import math

import jax
import jax.numpy as jnp
from jax import lax
from jax.experimental import pallas as pl
from jax.experimental.pallas import tpu as pltpu

D_MODEL = 1024
N_META = 16
N_HEADS = 4
QK_DIM = 64
V_DIM = 128
ROT_DIM = 16
ROPE_THETA = 500000.0
SSM_GROUP = 16
N_GROUPS = 32
SSM_STATE = 64
SSM_WIDTH = 512
ATTN_WIDTH = 512
D_FF = 2816
EPS = 1e-6
LAM_INIT = 0.8 - 0.6 * math.exp(-0.3 * 0)

LANES = 128
SUBLANES = 8
MXU_DIM = 256
CHUNK = 16
OCT = LANES // SSM_GROUP
N_OCT = N_GROUPS // OCT
OCT_IN = CHUNK * LANES
OCT_STATE = OCT * 2 * SSM_STATE
SCAN_PAD = SUBLANES
LOCAL_STEPS = 3

ROW_TILE = 512
IN_ROW_TILE = 1024
Q_TILE = 512
KV_TILE = 512
ATTN_UNROLL = 4
VMEM_LIMIT = 56 * 1024 * 1024

F32 = jnp.float32
BF16 = jnp.bfloat16


def _rms(x, g):
    return x * lax.rsqrt(jnp.mean(x * x, axis=-1, keepdims=True) + EPS) * g


def _in_proj_kernel(x_ref, g_ref, w_ref, cr_ref, sr_ref, cb_ref, sb_ref, q_ref, k_ref, v_ref, u_ref,
                    *, k_transposed):
    h = _rms(x_ref[...], g_ref[...]).astype(BF16)
    proj = jnp.dot(h, w_ref[...], preferred_element_type=F32)
    cos = cb_ref[0] * cr_ref[...] - sb_ref[0] * sr_ref[...]
    sin = sb_ref[0] * cr_ref[...] + cb_ref[0] * sr_ref[...]
    d = lax.broadcasted_iota(jnp.int32, (1, LANES), 1) % QK_DIM
    sa = sin * jnp.where(d < ROT_DIM // 2, -1.0, 0.0)
    sb = sin * jnp.where((d >= ROT_DIM // 2) & (d < ROT_DIM), 1.0, 0.0)

    def rope(t):
        return (t * cos + pltpu.roll(t, LANES - ROT_DIM // 2, axis=1) * sa
                + pltpu.roll(t, ROT_DIM // 2, axis=1) * sb)

    for c in range(ATTN_WIDTH // LANES):
        sl = slice(c * LANES, (c + 1) * LANES)
        q_ref[:, sl] = rope(proj[:, c * LANES:(c + 1) * LANES]).astype(BF16)
        k_c = rope(proj[:, ATTN_WIDTH + c * LANES:ATTN_WIDTH + (c + 1) * LANES])
        if k_transposed:
            k_ref[sl, :] = k_c.T.astype(BF16)
        else:
            k_ref[:, sl] = k_c.astype(BF16)
    v_ref[...] = proj[:, 2 * ATTN_WIDTH:3 * ATTN_WIDTH].astype(BF16)
    u_ref[...] = proj[:, 3 * ATTN_WIDTH:]


def _in_proj(x2d, g, w, first_pos, seq, tm, k_transposed):
    n = x2d.shape[0]
    token_major = pl.BlockSpec((tm, ATTN_WIDTH), lambda i: (i, 0))
    k_spec = pl.BlockSpec((ATTN_WIDTH, tm), lambda i: (0, i)) if k_transposed else token_major
    k_shape = (ATTN_WIDTH, n) if k_transposed else (n, ATTN_WIDTH)
    tiles_per_seq = seq // tm
    cos_r, sin_r = _rope_angles(jnp.arange(tm, dtype=F32))
    cos_b, sin_b = _rope_angles(first_pos + tm * jnp.arange(tiles_per_seq, dtype=F32))
    row = lambda i: (i, 0)
    const = lambda i: (0, 0)
    base = pl.BlockSpec((1, 1, LANES), lambda i: (i % tiles_per_seq, 0, 0))
    return pl.pallas_call(
        functools.partial(_in_proj_kernel, k_transposed=k_transposed),
        grid=(n // tm,),
        in_specs=[
            pl.BlockSpec((tm, D_MODEL), row),
            pl.BlockSpec((1, D_MODEL), const),
            pl.BlockSpec((D_MODEL, 4 * ATTN_WIDTH), const),
            pl.BlockSpec((tm, LANES), const),
            pl.BlockSpec((tm, LANES), const),
            base, base,
        ],
        out_specs=[token_major, k_spec, token_major, token_major],
        out_shape=[jax.ShapeDtypeStruct((n, ATTN_WIDTH), BF16), jax.ShapeDtypeStruct(k_shape, BF16),
                   jax.ShapeDtypeStruct((n, ATTN_WIDTH), BF16), jax.ShapeDtypeStruct((n, SSM_WIDTH), F32)],
        compiler_params=pltpu.CompilerParams(
            dimension_semantics=("arbitrary",), vmem_limit_bytes=VMEM_LIMIT),
        name="in_proj",
    )(x2d, g, w, cos_r, sin_r, cos_b[:, None, :], sin_b[:, None, :])


def _attn_kernel(*refs):
    n_pairs = refs[0].shape[0] // (2 * Q_TILE)

    def q_pair(sp, carry):
        _attn_q_pair(sp, *refs)
        return carry

    lax.fori_loop(0, n_pairs, q_pair, 0)


def _attn_q_pair(sp, q1_ref, q2_ref, k1_ref, k2_ref, v_ref, km1_ref, km2_ref, vm_ref,
                 lq1_ref, lk1_ref, lq2_ref, lk2_ref, sg_ref,
                 o_ref, m_ref, l_ref, acc_ref, p_ref):
    own = lax.broadcasted_iota(jnp.int32, (Q_TILE, LANES), 1) // QK_DIM == pl.program_id(1) % 2
    q_rows = [pl.ds(pl.multiple_of((2 * sp + c) * Q_TILE, Q_TILE), Q_TILE) for c in range(2)]
    qs = [tuple(jnp.where(own, r[rows, :], jnp.zeros((Q_TILE, LANES), BF16)) for r in (q1_ref, q2_ref))
          for rows in q_rows]
    k_refs, km_refs = (k1_ref, k2_ref), (km1_ref, km2_ref)
    wide = (Q_TILE, LANES)

    def block(c, k, mask, slot, prev):
        n_col = k[0].shape[1] // LANES
        for i in range(2):
            s = jnp.dot(qs[c][i], k[i], preferred_element_type=F32)
            if mask is not None:
                s = jnp.where(mask, s, -jnp.inf)
            cols = [s[:, j * LANES:(j + 1) * LANES] for j in range(n_col)]
            m_tile = functools.reduce(jnp.maximum, cols)
            m_new = jnp.broadcast_to(jnp.max(m_tile, axis=1, keepdims=True), wide)
            if prev is not None:
                m_old = m_ref[c, i]
                m_new = jnp.maximum(m_old, m_new)
                alpha = jnp.exp2(m_old - m_new)
            m_ref[c, i] = m_new
            ps = [jnp.exp2(col - m_new) for col in cols]
            l_tile = functools.reduce(jnp.add, ps)
            l_ref[c, i] = l_tile if prev is None else alpha * l_ref[c, i] + l_tile
            for j in range(n_col):
                p_ref[c, slot, i, :, j * LANES:(j + 1) * LANES] = ps[j].astype(BF16)
            if prev is not None:
                slot_prev, v_prev = prev
                pv = jnp.dot(p_ref[c, slot_prev, i], v_prev, preferred_element_type=F32)
                acc_ref[c, i] = (acc_ref[c, i] + pv) * alpha

    def tile(ref, t):
        return ref[pl.ds(pl.multiple_of(t * KV_TILE, KV_TILE), KV_TILE), :]

    def ktile(t):
        return tuple(r[:, pl.ds(pl.multiple_of(t * KV_TILE, KV_TILE), KV_TILE)] for r in k_refs)

    acc_ref[...] = jnp.zeros(acc_ref.shape, F32)
    d0 = 2 * sp
    causal = (lax.broadcasted_iota(jnp.int32, (Q_TILE, KV_TILE), 1)
              <= lax.broadcasted_iota(jnp.int32, (Q_TILE, KV_TILE), 0))
    k_d0 = ktile(d0)
    block(0, k_d0, causal, 0, None)
    block(1, k_d0, None, 0, None)
    block(1, ktile(d0 + 1), causal, 1, (0, tile(v_ref, d0)))
    pending = (0, 1)

    def earlier(t0, n):
        for d in range(n):
            k_t = ktile(t0 + d)
            for c in range(2):
                before = jnp.where(t0 == 0, d0 + c, t0 - 1) if d == 0 else t0 + d - 1
                src = pending[c] if d % 2 == 0 else 1 - pending[c]
                block(c, k_t, None, 1 - src, (src, tile(v_ref, before)))

    def unrolled(jj, carry):
        earlier(ATTN_UNROLL * jj, ATTN_UNROLL)
        return carry

    lax.fori_loop(0, d0 // ATTN_UNROLL, unrolled, 0)
    for n in (ATTN_UNROLL // 2, ATTN_UNROLL // 4):
        if n >= 2:
            done = (d0 // (2 * n)) * (2 * n)
            pl.when(d0 - done >= n)(functools.partial(earlier, done, n))

    meta_mask = lax.broadcasted_iota(jnp.int32, wide, 1) < N_META
    k_meta = tuple(r[...] for r in km_refs)
    for c in range(2):
        before = jnp.where(sp == 0, d0 + c, d0 - 1)
        block(c, k_meta, meta_mask, 1 - pending[c], (pending[c], tile(v_ref, before)))
    for c in range(2):
        for i in range(2):
            acc_ref[c, i] = acc_ref[c, i] + jnp.dot(p_ref[c, 1 - pending[c], i, :, :LANES], vm_ref[...],
                                                    preferred_element_type=F32)

    lam = (jnp.exp(jnp.sum(lq1_ref[...] * lk1_ref[...], axis=1, keepdims=True))
           - jnp.exp(jnp.sum(lq2_ref[...] * lk2_ref[...], axis=1, keepdims=True)) + LAM_INIT)
    for c in range(2):
        l1 = jnp.sum(l_ref[c, 0], axis=1, keepdims=True)
        l2 = jnp.sum(l_ref[c, 1], axis=1, keepdims=True)
        o = acc_ref[c, 0] / l1 - lam * (acc_ref[c, 1] / l2)
        o_ref[q_rows[c], :] = (_rms(o, sg_ref[...]) * (1.0 - LAM_INIT)).astype(BF16)


def _attention(q, k, v, km, vm, lq1, lk1, lq2, lk2, sg, bsz, seq):
    assert Q_TILE == KV_TILE and seq % (2 * Q_TILE) == 0
    pairs = N_HEADS // 2
    kvmap = lambda b, h: (b, h)
    map1 = pl.BlockSpec((seq, LANES), lambda b, h: (b, h // 2))
    map2 = pl.BlockSpec((seq, LANES), lambda b, h: (b, pairs + h // 2))
    kmap1 = pl.BlockSpec((LANES, seq), lambda b, h: (h // 2, b))
    kmap2 = pl.BlockSpec((LANES, seq), lambda b, h: (pairs + h // 2, b))
    meta1 = pl.BlockSpec((LANES, LANES), lambda b, h: (h // 2, 0))
    meta2 = pl.BlockSpec((LANES, LANES), lambda b, h: (pairs + h // 2, 0))
    const = lambda b, h: (0, 0)
    vec = pl.BlockSpec((1, QK_DIM), const)
    return pl.pallas_call(
        _attn_kernel,
        grid=(bsz, N_HEADS),
        in_specs=[
            map1, map2, kmap1, kmap2,
            pl.BlockSpec((seq, LANES), kvmap),
            meta1, meta2,
            pl.BlockSpec((LANES, LANES), lambda b, h: (0, h)),
            vec, vec, vec, vec,
            pl.BlockSpec((1, V_DIM), const),
        ],
        out_specs=pl.BlockSpec((seq, LANES), kvmap),
        out_shape=jax.ShapeDtypeStruct((bsz * seq, ATTN_WIDTH), BF16),
        scratch_shapes=[
            pltpu.VMEM((2, 2, Q_TILE, LANES), F32),
            pltpu.VMEM((2, 2, Q_TILE, LANES), F32),
            pltpu.VMEM((2, 2, Q_TILE, V_DIM), F32),
            pltpu.VMEM((2, 2, 2, Q_TILE, KV_TILE), BF16),
        ],
        compiler_params=pltpu.CompilerParams(
            dimension_semantics=("arbitrary", "arbitrary"), vmem_limit_bytes=VMEM_LIMIT),
        name="diff_attention",
    )(q, q, k, k, v, km, km, vm, lq1, lk1, lq2, lk2, sg)


def _ssm_kernel(u_ref, um_ref, bb_ref, bc_ref, ccr_ref, cci_ref, loc_ref, blk_ref, d_ref, y_ref,
                tz_ref, bs_ref, cs_ref, ub_ref, uf_ref, buf_a, buf_b, cb_a, cb_b, rep_ref, yt_ref, *, n_chunks):
    n_tiles = OCT_STATE // LANES
    n_blocks = n_chunks // SUBLANES
    blk_steps = n_blocks.bit_length() - 1
    blk_pad = cb_a.shape[1] - n_blocks

    @pl.when((pl.program_id(0) == 0) & (pl.program_id(1) == 0))
    def _():
        for buf, pad in ((buf_a, SCAN_PAD), (buf_b, SCAN_PAD), (cb_a, blk_pad), (cb_b, blk_pad)):
            buf[:, :pad, :] = jnp.zeros((n_tiles, pad, LANES), F32)

    @pl.when(pl.program_id(1) == 0)
    def _():
        def group_of(shape, axis):
            return (lax.broadcasted_iota(jnp.int32, shape, axis) // SSM_GROUP) % OCT

        def split(x):
            head = x.astype(BF16)
            return head, (x - head.astype(F32)).astype(BF16)

        wide_shape = (LANES, OCT * SSM_STATE)
        own_cols = group_of(wide_shape, 0) == lax.broadcasted_iota(jnp.int32, wide_shape, 1) // SSM_STATE
        bbar = [split(jnp.where(own_cols, jnp.concatenate([bb_ref[part, 0]] * (OCT // 2), axis=1), 0.0))
                for part in range(2)]
        lag_blk = []
        for n in range(0, CHUNK, 2):
            terms = []
            for part, c_ref in enumerate((ccr_ref, cci_ref)):
                c_n = jnp.concatenate([c_ref[n], c_ref[n + 1]], axis=1)
                g_lane = group_of(c_n.shape, 1)
                ca_head, ca_rest = split(jnp.concatenate(
                    [jnp.where(g_lane == g, c_n, 0.0) for g in range(OCT)], axis=0))
                b_head, b_rest = bbar[part]
                terms.append(jnp.dot(b_head, ca_head, preferred_element_type=F32)
                             + jnp.dot(b_head, ca_rest, preferred_element_type=F32)
                             + jnp.dot(b_rest, ca_head, preferred_element_type=F32))
            both = (terms[0] - terms[1]).astype(BF16)
            lag_blk += [both[:, :LANES], both[:, LANES:]]
        for m in range(CHUNK // 2):
            tz_ref[m, :LANES, :LANES] = lag_blk[2 * m]
            tz_ref[m, :LANES, LANES:] = lag_blk[2 * m + 1]
            tz_ref[m, LANES:, :LANES] = lag_blk[2 * m - 1] if m else jnp.zeros((LANES, LANES), BF16)
            tz_ref[m, LANES:, LANES:] = lag_blk[2 * m]
        bc = bc_ref[:, 0].reshape(OCT_IN, 4 * SSM_STATE)
        g_row = group_of(bc.shape, 0)
        w_col = (lax.broadcasted_iota(jnp.int32, bc.shape, 1) // SSM_STATE) % 2
        for qq in range(OCT // 2):
            bs_ref[:, qq * MXU_DIM:(qq + 1) * MXU_DIM] = jnp.where(g_row == 2 * qq + w_col, bc, 0.0).astype(BF16)
        g_col = group_of((SSM_STATE, LANES), 1)
        for s in range(CHUNK):
            c_s = (ccr_ref[s + 1], -cci_ref[s + 1])
            for qq in range(OCT // 2):
                for part in range(2):
                    for w in range(2):
                        r0 = qq * MXU_DIM + part * LANES + w * SSM_STATE
                        cs_ref[r0:r0 + SSM_STATE, s * LANES:(s + 1) * LANES] = jnp.where(
                            g_col == 2 * qq + w, c_s[part], 0.0).astype(BF16)

    def slab(ref, t, rows):
        return ref[pl.ds(t, rows, stride=CHUNK), :]

    for t in range(CHUNK):
        u_t = slab(u_ref, t, n_chunks)
        uf_ref[t] = u_t
        ub_ref[t // 2, :, (t % 2) * LANES:(t % 2 + 1) * LANES] = u_t.astype(BF16)

    z = jnp.dot(ub_ref[0], bs_ref[:MXU_DIM, :], preferred_element_type=F32)
    for tt in range(1, CHUNK // 2):
        z = z + jnp.dot(ub_ref[tt], bs_ref[tt * MXU_DIM:(tt + 1) * MXU_DIM, :], preferred_element_type=F32)
    um = jnp.concatenate([jnp.broadcast_to(um_ref[t:t + 1, :], (SUBLANES, LANES)) for t in range(CHUNK)], axis=1)
    zm = jnp.dot(um.astype(BF16), bs_ref[...], preferred_element_type=F32)[0:1, :]
    for ct in range(n_tiles):
        cols = slice(ct * LANES, (ct + 1) * LANES)
        buf_a[ct, SCAN_PAD:, :] = z[:, cols]
        for buf, pad in ((buf_a, SCAN_PAD), (buf_b, SCAN_PAD), (cb_a, blk_pad), (cb_b, blk_pad)):
            buf[ct, pad - 1:pad, :] = zm[:, cols]

    def within_chunk(ss):
        acc = jnp.dot(ub_ref[0], tz_ref[ss], preferred_element_type=F32)
        for tt in range(1, ss + 1):
            acc = acc + jnp.dot(ub_ref[tt], tz_ref[ss - tt], preferred_element_type=F32)
        yt_ref[ss] = acc

    assert LOCAL_STEPS + blk_steps >= CHUNK // 2 and (1 << LOCAL_STEPS) == SUBLANES
    todo = list(range(CHUNK // 2 - 1, -1, -1))
    blocked = (n_blocks, SUBLANES, LANES)

    def lanes_of(pair):
        return slice(pair * LANES, (pair + 1) * LANES)

    def combine(x, p, ar, ai):
        return x[0] + ar * p[0] - ai * p[1], x[1] + ar * p[1] + ai * p[0]

    src, dst = buf_a, buf_b
    for step in range(LOCAL_STEPS):
        shift = 1 << step
        within_chunk(todo.pop(0))
        for pair in range(OCT // 2):
            ar, ai = (loc_ref[0, step, part, :, lanes_of(pair)] for part in range(2))
            x = [src[2 * pair + part, SCAN_PAD:, :].reshape(blocked) for part in range(2)]
            p = [src[2 * pair + part, SCAN_PAD - shift:SCAN_PAD - shift + n_chunks, :].reshape(blocked)
                 for part in range(2)]
            for part, val in enumerate(combine(x, p, ar, ai)):
                dst[2 * pair + part, SCAN_PAD:, :] = val.reshape(n_chunks, LANES)
        src, dst = dst, src

    for ct in range(n_tiles):
        cb_a[ct, blk_pad:, :] = src[ct, pl.ds(SCAN_PAD + SUBLANES - 1, n_blocks, stride=SUBLANES), :]
    csrc, cdst = cb_a, cb_b
    for step in range(blk_steps):
        shift = 1 << step
        if todo:
            within_chunk(todo.pop(0))
        for pair in range(OCT // 2):
            ar = blk_ref[0, step:step + 1, lanes_of(pair)]
            ai = blk_ref[0, blk_steps + step:blk_steps + step + 1, lanes_of(pair)]
            x = [csrc[2 * pair + part, blk_pad:, :] for part in range(2)]
            p = [csrc[2 * pair + part, blk_pad - shift:blk_pad - shift + n_blocks, :] for part in range(2)]
            for part, val in enumerate(combine(x, p, ar, ai)):
                cdst[2 * pair + part, blk_pad:, :] = val
        csrc, cdst = cdst, csrc

    for pair in range(OCT // 2):
        for part in range(2):
            entering = csrc[2 * pair + part, blk_pad - 1:blk_pad - 1 + n_blocks, :]
            for row in range(SUBLANES):
                rep_ref[pair, part, pl.ds(row, n_blocks, stride=SUBLANES), :] = entering
        ar, ai = (loc_ref[0, LOCAL_STEPS, part, :, lanes_of(pair)] for part in range(2))
        x = [src[2 * pair + part, SCAN_PAD:, :].reshape(blocked) for part in range(2)]
        p = [rep_ref[pair, part].reshape(blocked) for part in range(2)]
        for part, val in enumerate(combine(x, p, ar, ai)):
            dst[2 * pair + part, SCAN_PAD:, :] = val.reshape(n_chunks, LANES)

    start_state = jnp.concatenate(
        [dst[ct, SCAN_PAD - 1:SCAN_PAD - 1 + n_chunks, :] for ct in range(n_tiles)], axis=1).astype(BF16)
    for ss in range(CHUNK // 2):
        cols = slice(ss * MXU_DIM, (ss + 1) * MXU_DIM)
        acc = yt_ref[ss] + jnp.dot(start_state, cs_ref[:, cols], preferred_element_type=F32)
        for half in range(2):
            t = 2 * ss + half
            y = acc[:, half * LANES:(half + 1) * LANES] + d_ref[0] * uf_ref[t]
            y_ref[pl.ds(t, n_chunks, stride=CHUNK), :] = 0.5 * y * (1.0 + lax.erf(y * (1.0 / math.sqrt(2.0))))


def _ssm(u, u_meta, bb2, bc, cc_re, cc_im, loc, blk, dp, bsz, seq):
    n_chunks = seq // CHUNK
    n_blocks = n_chunks // SUBLANES
    assert n_blocks & (n_blocks - 1) == 0 and N_META == CHUNK
    octet = lambda o, b: (o, 0, 0)
    n_tiles = OCT_STATE // LANES
    scan_buf = pltpu.VMEM((n_tiles, SCAN_PAD + n_chunks, LANES), F32)
    block_buf = pltpu.VMEM((n_tiles, n_blocks // 2 + n_blocks, LANES), F32)
    return pl.pallas_call(
        functools.partial(_ssm_kernel, n_chunks=n_chunks),
        grid=(N_OCT, bsz),
        in_specs=[
            pl.BlockSpec((seq, LANES), lambda o, b: (b, o)),
            pl.BlockSpec((N_META, LANES), lambda o, b: (0, o)),
            pl.BlockSpec((2, 1, LANES, 2 * SSM_STATE), lambda o, b: (0, o, 0, 0)),
            pl.BlockSpec((CHUNK, 1, LANES, 4 * SSM_STATE), lambda o, b: (0, o, 0, 0)),
            pl.BlockSpec((CHUNK + 1, SSM_STATE, LANES), lambda o, b: (0, 0, o)),
            pl.BlockSpec((CHUNK + 1, SSM_STATE, LANES), lambda o, b: (0, 0, o)),
            pl.BlockSpec((1,) + loc.shape[1:], lambda o, b: (o, 0, 0, 0, 0)),
            pl.BlockSpec((1,) + blk.shape[1:], octet),
            pl.BlockSpec((1, 1, LANES), octet),
        ],
        out_specs=pl.BlockSpec((seq, LANES), lambda o, b: (b, o)),
        out_shape=jax.ShapeDtypeStruct((bsz * seq, SSM_WIDTH), F32),
        scratch_shapes=[
            pltpu.VMEM((CHUNK // 2, MXU_DIM, MXU_DIM), BF16),
            pltpu.VMEM((OCT_IN, OCT_STATE), BF16),
            pltpu.VMEM((OCT_STATE, OCT_IN), BF16),
            pltpu.VMEM((CHUNK // 2, n_chunks, MXU_DIM), BF16), pltpu.VMEM((CHUNK, n_chunks, LANES), F32),
            scan_buf, scan_buf, block_buf, block_buf,
            pltpu.VMEM((OCT // 2, 2, n_chunks, LANES), F32),
            pltpu.VMEM((CHUNK // 2, n_chunks, MXU_DIM), F32)],
        compiler_params=pltpu.CompilerParams(
            dimension_semantics=("arbitrary", "arbitrary"), vmem_limit_bytes=VMEM_LIMIT),
        name="s5_chunked",
    )(u, u_meta, bb2, bc, cc_re, cc_im, loc, blk, dp)


def _ssm_weights(a_re, a_im, log_dt, b_re, b_im, c_re, c_im, d_skip, n_chunks):
    dt = jnp.exp(log_dt)[:, None]
    lam_re, lam_im = a_re * dt, a_im * dt

    def power(n, transposed=False):
        n = n[:, None, None]
        l_re, l_im = (lam_re.T, lam_im.T) if transposed else (lam_re, lam_im)
        mag = jnp.exp(n * l_re)
        return mag * jnp.cos(n * l_im), mag * jnp.sin(n * l_im)

    ab_re, ab_im = jnp.exp(lam_re) * jnp.cos(lam_im), jnp.exp(lam_re) * jnp.sin(lam_im)
    den = a_re * a_re + a_im * a_im
    nr, ni = ab_re - 1.0, ab_im
    f_re = (nr * a_re + ni * a_im) / den
    f_im = (ni * a_re - nr * a_im) / den
    bb_re = f_re[..., None] * b_re - f_im[..., None] * b_im
    bb_im = f_re[..., None] * b_im + f_im[..., None] * b_re

    p_re, p_im = power(jnp.arange(CHUNK + 1, dtype=F32))

    ct_re, ct_im = c_re.transpose(2, 0, 1)[None], c_im.transpose(2, 0, 1)[None]
    pt_re, pt_im = (t[..., None] for t in power(jnp.arange(CHUNK + 1, dtype=F32), transposed=True))
    ca_re = ct_re * pt_re - ct_im * pt_im
    ca_im = ct_re * pt_im + ct_im * pt_re
    cc_re = ca_re.reshape(CHUNK + 1, SSM_STATE, SSM_WIDTH)
    cc_im = ca_im.reshape(CHUNK + 1, SSM_STATE, SSM_WIDTH)

    r_re, r_im = p_re[CHUNK - 1::-1][:, :, None, :], p_im[CHUNK - 1::-1][:, :, None, :]
    bt_re, bt_im = bb_re.transpose(0, 2, 1)[None], bb_im.transpose(0, 2, 1)[None]
    bb2 = jnp.stack([jnp.concatenate([t, t], axis=-1).reshape(N_OCT, LANES, 2 * SSM_STATE) for t in (bt_re, bt_im)])
    bs_re = r_re * bt_re - r_im * bt_im
    bs_im = r_re * bt_im + r_im * bt_re
    bc = jnp.concatenate([bs_re, bs_re, bs_im, bs_im], axis=-1).reshape(CHUNK, N_OCT, LANES, 4 * SSM_STATE)

    rows = jnp.arange(SUBLANES)
    strides = 2 ** jnp.arange(LOCAL_STEPS)
    in_block = (rows[None, :] >= strides[:, None])[:, :, None, None]
    blk_steps = (n_chunks // SUBLANES).bit_length() - 1

    def pack(m):
        m = m.reshape(m.shape[:-2] + (N_OCT, OCT_STATE // 2))
        return jnp.moveaxis(m, -2, 0)

    loc, blk = [], []
    for carry, block_stride in zip(power(CHUNK * (rows + 1.0)), power(CHUNK * SUBLANES * 2.0 ** jnp.arange(blk_steps))):
        steps = jnp.where(in_block, carry[strides - 1][:, None], 0.0)
        loc.append(pack(jnp.concatenate([steps, carry[None]], axis=0)))
        blk.append(pack(block_stride))
    loc = jnp.stack(loc, axis=2)
    blk = jnp.concatenate(blk, axis=1)

    dp = d_skip.reshape(N_OCT, 1, LANES)
    return bb2, bc, cc_re, cc_im, loc, blk, dp


def _out_ffn_kernel(x_ref, o_ref, y_ref, wglu_ref, bglu_ref, sg_ref, wout_ref, pmg_ref, pfg_ref,
                    wg_ref, wu_ref, wd_ref, pog_ref, out_ref):
    y = y_ref[...]
    gate = jnp.dot(y.astype(BF16), wglu_ref[...], preferred_element_type=F32) + bglu_ref[...]
    y = _rms(y * jax.nn.sigmoid(gate), sg_ref[...]).astype(BF16)
    mix = (jnp.dot(o_ref[...], wout_ref[:ATTN_WIDTH, :], preferred_element_type=F32)
           + jnp.dot(y, wout_ref[ATTN_WIDTH:, :], preferred_element_type=F32))
    h1 = x_ref[...] + _rms(mix, pmg_ref[...])
    h2 = _rms(h1, pfg_ref[...]).astype(BF16)
    g = jnp.dot(h2, wg_ref[...], preferred_element_type=F32)
    up = jnp.dot(h2, wu_ref[...], preferred_element_type=F32)
    f = (g * jax.nn.sigmoid(g) * up).astype(BF16)
    f = jnp.dot(f, wd_ref[...], preferred_element_type=F32)
    out_ref[...] = h1 + _rms(f, pog_ref[...])


def _out_ffn(x2d, o, y, wglu, bglu, sg, wout, pmg, pfg, wg, wu, wd, pog, tm):
    n = x2d.shape[0]
    row = lambda i: (i, 0)
    const = lambda i: (0, 0)

    def resident(shape):
        return pl.BlockSpec(shape, const, pipeline_mode=pl.Buffered(1))

    return pl.pallas_call(
        _out_ffn_kernel,
        grid=(n // tm,),
        in_specs=[
            pl.BlockSpec((tm, D_MODEL), row),
            pl.BlockSpec((tm, ATTN_WIDTH), row),
            pl.BlockSpec((tm, SSM_WIDTH), row),
            resident((SSM_WIDTH, SSM_WIDTH)),
            resident((1, SSM_WIDTH)),
            resident((1, SSM_WIDTH)),
            resident((D_MODEL, D_MODEL)),
            resident((1, D_MODEL)),
            resident((1, D_MODEL)),
            resident((D_MODEL, D_FF)),
            resident((D_MODEL, D_FF)),
            resident((D_FF, D_MODEL)),
            resident((1, D_MODEL)),
        ],
        out_specs=pl.BlockSpec((tm, D_MODEL), row),
        out_shape=jax.ShapeDtypeStruct((n, D_MODEL), F32),
        compiler_params=pltpu.CompilerParams(
            dimension_semantics=("arbitrary",), vmem_limit_bytes=VMEM_LIMIT),
        name="out_ffn",
    )(x2d, o, y, wglu, bglu, sg, wout, pmg, pfg, wg, wu, wd, pog)


def _rope_angles(pos):
    d = jnp.arange(LANES) % QK_DIM
    inv_freq = ROPE_THETA ** (-(2 * (d % (ROT_DIM // 2))).astype(F32) / ROT_DIM)
    ang = pos[:, None] * jnp.where(d < ROT_DIM, inv_freq, 0.0)[None, :]
    return jnp.cos(ang), jnp.sin(ang)


def _scale_w_in(w):
    scale = math.log2(math.e) / math.sqrt(QK_DIM)
    col_scale = jnp.where(jnp.arange(w.shape[1]) < ATTN_WIDTH, scale, 1.0).astype(F32)
    return (w * col_scale[None, :]).astype(BF16)


def kernel(x, meta, pre_mix_g, w_in, lambda_q1, lambda_k1, lambda_q2, lambda_k2, subln_g, a_re, a_im, log_dt,
           b_re, b_im, c_re, c_im, d_skip, w_glu, b_glu, ssm_out_g, w_out, post_mix_g, pre_ffn_g, w_gate,
           w_up, w_down, post_ffn_g):
    bsz, seq, _ = x.shape
    n = bsz * seq
    x2d = x.reshape(n, D_MODEL)
    row = lambda t: t[0].reshape(1, -1)

    w_in_p = _scale_w_in(w_in[0])
    g0 = row(pre_mix_g)
    q, k, v, u = _in_proj(x2d, g0, w_in_p, N_META, seq, IN_ROW_TILE, True)
    _, k_m, v_m, u_m = _in_proj(meta, g0, w_in_p, 0, N_META, N_META, False)

    pad_meta = lambda t: jnp.pad(t, ((0, LANES - N_META), (0, 0)))
    o = _attention(q, k, v, pad_meta(k_m).T, pad_meta(v_m), lambda_q1, lambda_k1, lambda_q2, lambda_k2,
                   row(subln_g), bsz, seq)

    ssm_w = _ssm_weights(a_re[0], a_im[0], log_dt[0], b_re[0], b_im[0], c_re[0], c_im[0], d_skip[0],
                         seq // CHUNK)
    y = _ssm(u, u_m, *ssm_w, bsz, seq)

    out = _out_ffn(x2d, o, y, w_glu[0].astype(BF16), row(b_glu), row(ssm_out_g), w_out[0].astype(BF16),
                   row(post_mix_g), row(pre_ffn_g), w_gate[0].astype(BF16), w_up[0].astype(BF16),
                   w_down[0].astype(BF16), row(post_ffn_g), ROW_TILE)
    return out.reshape(bsz, seq, D_MODEL)
```

```python
import functools
import math

import jax
import jax.numpy as jnp
from jax import lax
from jax.experimental import pallas as pl
from jax.experimental.pallas import tpu as pltpu

D_MODEL = 1024
N_META = 16
N_HEADS = 4
QK_DIM = 64
V_DIM = 128
ROT_DIM = 16
ROPE_THETA = 500000.0
SSM_GROUP = 16
N_GROUPS = 32
SSM_STATE = 64
SSM_WIDTH = 512
ATTN_WIDTH = 512
D_FF = 2816
EPS = 1e-6
LAM_INIT = 0.8 - 0.6 * math.exp(-0.3 * 0)

LANES = 128
SUBLANES = 8
MXU_DIM = 256
CHUNK = 16
OCT = LANES // SSM_GROUP
N_OCT = N_GROUPS // OCT
OCT_IN = CHUNK * LANES
OCT_STATE = OCT * 2 * SSM_STATE
SCAN_PAD = SUBLANES
LOCAL_STEPS = 3

ROW_TILE = 512
IN_ROW_TILE = 1024
Q_TILE = 512
KV_TILE = 512
ATTN_UNROLL = 4
VMEM_LIMIT = 56 * 1024 * 1024

F32 = jnp.float32
BF16 = jnp.bfloat16


def _rms(x, g):
    return x * lax.rsqrt(jnp.mean(x * x, axis=-1, keepdims=True) + EPS) * g


def _in_proj_kernel(x_ref, g_ref, w_ref, cr_ref, sr_ref, cb_ref, sb_ref, q_ref, k_ref, v_ref, u_ref,
                    *, k_transposed):
    h = _rms(x_ref[...], g_ref[...]).astype(BF16)
    proj = jnp.dot(h, w_ref[...], preferred_element_type=F32)
    cos = cb_ref[0] * cr_ref[...] - sb_ref[0] * sr_ref[...]
    sin = sb_ref[0] * cr_ref[...] + cb_ref[0] * sr_ref[...]
    d = lax.broadcasted_iota(jnp.int32, (1, LANES), 1) % QK_DIM
    sa = sin * jnp.where(d < ROT_DIM // 2, -1.0, 0.0)
    sb = sin * jnp.where((d >= ROT_DIM // 2) & (d < ROT_DIM), 1.0, 0.0)

    def rope(t):
        return (t * cos + pltpu.roll(t, LANES - ROT_DIM // 2, axis=1) * sa
                + pltpu.roll(t, ROT_DIM // 2, axis=1) * sb)

    for c in range(ATTN_WIDTH // LANES):
        sl = slice(c * LANES, (c + 1) * LANES)
        q_ref[:, sl] = rope(proj[:, c * LANES:(c + 1) * LANES]).astype(BF16)
        k_c = rope(proj[:, ATTN_WIDTH + c * LANES:ATTN_WIDTH + (c + 1) * LANES])
        if k_transposed:
            k_ref[sl, :] = k_c.T.astype(BF16)
        else:
            k_ref[:, sl] = k_c.astype(BF16)
    v_ref[...] = proj[:, 2 * ATTN_WIDTH:3 * ATTN_WIDTH].astype(BF16)
    u_ref[...] = proj[:, 3 * ATTN_WIDTH:]


def _in_proj(x2d, g, w, first_pos, seq, tm, k_transposed):
    n = x2d.shape[0]
    token_major = pl.BlockSpec((tm, ATTN_WIDTH), lambda i: (i, 0))
    k_spec = pl.BlockSpec((ATTN_WIDTH, tm), lambda i: (0, i)) if k_transposed else token_major
    k_shape = (ATTN_WIDTH, n) if k_transposed else (n, ATTN_WIDTH)
    tiles_per_seq = seq // tm
    cos_r, sin_r = _rope_angles(jnp.arange(tm, dtype=F32))
    cos_b, sin_b = _rope_angles(first_pos + tm * jnp.arange(tiles_per_seq, dtype=F32))
    row = lambda i: (i, 0)
    const = lambda i: (0, 0)
    base = pl.BlockSpec((1, 1, LANES), lambda i: (i % tiles_per_seq, 0, 0))
    return pl.pallas_call(
        functools.partial(_in_proj_kernel, k_transposed=k_transposed),
        grid=(n // tm,),
        in_specs=[
            pl.BlockSpec((tm, D_MODEL), row),
            pl.BlockSpec((1, D_MODEL), const),
            pl.BlockSpec((D_MODEL, 4 * ATTN_WIDTH), const),
            pl.BlockSpec((tm, LANES), const),
            pl.BlockSpec((tm, LANES), const),
            base, base,
        ],
        out_specs=[token_major, k_spec, token_major, token_major],
        out_shape=[jax.ShapeDtypeStruct((n, ATTN_WIDTH), BF16), jax.ShapeDtypeStruct(k_shape, BF16),
                   jax.ShapeDtypeStruct((n, ATTN_WIDTH), BF16), jax.ShapeDtypeStruct((n, SSM_WIDTH), F32)],
        compiler_params=pltpu.CompilerParams(
            dimension_semantics=("arbitrary",), vmem_limit_bytes=VMEM_LIMIT),
        name="in_proj",
    )(x2d, g, w, cos_r, sin_r, cos_b[:, None, :], sin_b[:, None, :])


def _attn_kernel(*refs):
    n_pairs = refs[0].shape[0] // (2 * Q_TILE)

    def q_pair(sp, carry):
        _attn_q_pair(sp, *refs)
        return carry

    lax.fori_loop(0, n_pairs, q_pair, 0)


def _attn_q_pair(sp, q1_ref, q2_ref, k1_ref, k2_ref, v_ref, km1_ref, km2_ref, vm_ref,
                 lq1_ref, lk1_ref, lq2_ref, lk2_ref, sg_ref,
                 o_ref, m_ref, l_ref, acc_ref, p_ref):
    own = lax.broadcasted_iota(jnp.int32, (Q_TILE, LANES), 1) // QK_DIM == pl.program_id(1) % 2
    q_rows = [pl.ds(pl.multiple_of((2 * sp + c) * Q_TILE, Q_TILE), Q_TILE) for c in range(2)]
    qs = [tuple(jnp.where(own, r[rows, :], jnp.zeros((Q_TILE, LANES), BF16)) for r in (q1_ref, q2_ref))
          for rows in q_rows]
    k_refs, km_refs = (k1_ref, k2_ref), (km1_ref, km2_ref)
    nt = (((1,), (1,)), ((), ()))
    wide = (Q_TILE, LANES)

    def block(c, k, mask, slot, prev):
        n_col = k[0].shape[1] // LANES
        for i in range(2):
            s = jnp.dot(qs[c][i], k[i], preferred_element_type=F32)
            if mask is not None:
                s = jnp.where(mask, s, -jnp.inf)
            cols = [s[:, j * LANES:(j + 1) * LANES] for j in range(n_col)]
            m_tile = functools.reduce(jnp.maximum, cols)
            m_new = jnp.broadcast_to(jnp.max(m_tile, axis=1, keepdims=True), wide)
            if prev is not None:
                m_old = m_ref[c, i]
                m_new = jnp.maximum(m_old, m_new)
                alpha = jnp.exp2(m_old - m_new)
            m_ref[c, i] = m_new
            ps = [jnp.exp2(col - m_new) for col in cols]
            l_tile = functools.reduce(jnp.add, ps)
            l_ref[c, i] = l_tile if prev is None else alpha * l_ref[c, i] + l_tile
            for j in range(n_col):
                p_ref[c, slot, i, :, j * LANES:(j + 1) * LANES] = ps[j].astype(BF16)
            if prev is not None:
                slot_prev, v_prev = prev
                pv = jnp.dot(p_ref[c, slot_prev, i], v_prev, preferred_element_type=F32)
                acc_ref[c, i] = (acc_ref[c, i] + pv) * alpha

    def tile(ref, t):
        return ref[pl.ds(pl.multiple_of(t * KV_TILE, KV_TILE), KV_TILE), :]

    def ktile(t):
        return tuple(r[:, pl.ds(pl.multiple_of(t * KV_TILE, KV_TILE), KV_TILE)] for r in k_refs)

    acc_ref[...] = jnp.zeros(acc_ref.shape, F32)
    d0 = 2 * sp
    causal = (lax.broadcasted_iota(jnp.int32, (Q_TILE, KV_TILE), 1)
              <= lax.broadcasted_iota(jnp.int32, (Q_TILE, KV_TILE), 0))
    k_d0 = ktile(d0)
    block(0, k_d0, causal, 0, None)
    block(1, k_d0, None, 0, None)
    block(1, ktile(d0 + 1), causal, 1, (0, tile(v_ref, d0)))
    pending = (0, 1)

    def earlier(t0, n):
        for d in range(n):
            k_t = ktile(t0 + d)
            for c in range(2):
                before = jnp.where(t0 == 0, d0 + c, t0 - 1) if d == 0 else t0 + d - 1
                src = pending[c] if d % 2 == 0 else 1 - pending[c]
                block(c, k_t, None, 1 - src, (src, tile(v_ref, before)))

    def unrolled(jj, carry):
        earlier(ATTN_UNROLL * jj, ATTN_UNROLL)
        return carry

    lax.fori_loop(0, d0 // ATTN_UNROLL, unrolled, 0)
    for n in (ATTN_UNROLL // 2, ATTN_UNROLL // 4):
        if n >= 2:
            done = (d0 // (2 * n)) * (2 * n)
            pl.when(d0 - done >= n)(functools.partial(earlier, done, n))

    meta_mask = lax.broadcasted_iota(jnp.int32, wide, 1) < N_META
    k_meta = tuple(r[...] for r in km_refs)
    for c in range(2):
        before = jnp.where(sp == 0, d0 + c, d0 - 1)
        block(c, k_meta, meta_mask, 1 - pending[c], (pending[c], tile(v_ref, before)))
    for c in range(2):
        for i in range(2):
            acc_ref[c, i] = acc_ref[c, i] + jnp.dot(p_ref[c, 1 - pending[c], i, :, :LANES], vm_ref[...],
                                                    preferred_element_type=F32)

    lam = (jnp.exp(jnp.sum(lq1_ref[...] * lk1_ref[...], axis=1, keepdims=True))
           - jnp.exp(jnp.sum(lq2_ref[...] * lk2_ref[...], axis=1, keepdims=True)) + LAM_INIT)
    for c in range(2):
        l1 = jnp.sum(l_ref[c, 0], axis=1, keepdims=True)
        l2 = jnp.sum(l_ref[c, 1], axis=1, keepdims=True)
        o = acc_ref[c, 0] / l1 - lam * (acc_ref[c, 1] / l2)
        o_ref[q_rows[c], :] = (_rms(o, sg_ref[...]) * (1.0 - LAM_INIT)).astype(BF16)


def _attention(q, k, v, km, vm, lq1, lk1, lq2, lk2, sg, bsz, seq):
    assert Q_TILE == KV_TILE and seq % (2 * Q_TILE) == 0
    pairs = N_HEADS // 2
    kvmap = lambda b, h: (b, h)
    map1 = pl.BlockSpec((seq, LANES), lambda b, h: (b, h // 2))
    map2 = pl.BlockSpec((seq, LANES), lambda b, h: (b, pairs + h // 2))
    kmap1 = pl.BlockSpec((LANES, seq), lambda b, h: (h // 2, b))
    kmap2 = pl.BlockSpec((LANES, seq), lambda b, h: (pairs + h // 2, b))
    meta1 = pl.BlockSpec((LANES, LANES), lambda b, h: (h // 2, 0))
    meta2 = pl.BlockSpec((LANES, LANES), lambda b, h: (pairs + h // 2, 0))
    const = lambda b, h: (0, 0)
    vec = pl.BlockSpec((1, QK_DIM), const)
    return pl.pallas_call(
        _attn_kernel,
        grid=(bsz, N_HEADS),
        in_specs=[
            map1, map2, kmap1, kmap2,
            pl.BlockSpec((seq, LANES), kvmap),
            meta1, meta2,
            pl.BlockSpec((LANES, LANES), lambda b, h: (0, h)),
            vec, vec, vec, vec,
            pl.BlockSpec((1, V_DIM), const),
        ],
        out_specs=pl.BlockSpec((seq, LANES), kvmap),
        out_shape=jax.ShapeDtypeStruct((bsz * seq, ATTN_WIDTH), BF16),
        scratch_shapes=[
            pltpu.VMEM((2, 2, Q_TILE, LANES), F32),
            pltpu.VMEM((2, 2, Q_TILE, LANES), F32),
            pltpu.VMEM((2, 2, Q_TILE, V_DIM), F32),
            pltpu.VMEM((2, 2, 2, Q_TILE, KV_TILE), BF16),
        ],
        compiler_params=pltpu.CompilerParams(
            dimension_semantics=("arbitrary", "arbitrary"), vmem_limit_bytes=VMEM_LIMIT),
        name="diff_attention",
    )(q, q, k, k, v, km, km, vm, lq1, lk1, lq2, lk2, sg)


def _ssm_kernel(u_ref, um_ref, bb_ref, bc_ref, ccr_ref, cci_ref, loc_ref, blk_ref, d_ref, y_ref,
                tz_ref, bs_ref, cs_ref, ub_ref, buf_a, buf_b, cb_a, cb_b, rep_ref, yt_ref, *, n_chunks):
    n_tiles = OCT_STATE // LANES
    n_blocks = n_chunks // SUBLANES
    blk_steps = n_blocks.bit_length() - 1
    blk_pad = cb_a.shape[1] - n_blocks

    @pl.when((pl.program_id(0) == 0) & (pl.program_id(1) == 0))
    def _():
        for buf, pad in ((buf_a, SCAN_PAD), (buf_b, SCAN_PAD), (cb_a, blk_pad), (cb_b, blk_pad)):
            buf[:, :pad, :] = jnp.zeros((n_tiles, pad, LANES), F32)

    @pl.when(pl.program_id(1) == 0)
    def _():
        def group_of(shape, axis):
            return (lax.broadcasted_iota(jnp.int32, shape, axis) // SSM_GROUP) % OCT

        def split(x):
            head = x.astype(BF16)
            return head, (x - head.astype(F32)).astype(BF16)

        wide_shape = (LANES, OCT * SSM_STATE)
        own_cols = group_of(wide_shape, 0) == lax.broadcasted_iota(jnp.int32, wide_shape, 1) // SSM_STATE
        bbar = [split(jnp.where(own_cols, jnp.concatenate([bb_ref[part, 0]] * (OCT // 2), axis=1), 0.0))
                for part in range(2)]
        lag_blk = []
        for n in range(0, CHUNK, 2):
            terms = []
            for part, c_ref in enumerate((ccr_ref, cci_ref)):
                c_n = jnp.concatenate([c_ref[n], c_ref[n + 1]], axis=1)
                g_lane = group_of(c_n.shape, 1)
                ca_head, ca_rest = split(jnp.concatenate(
                    [jnp.where(g_lane == g, c_n, 0.0) for g in range(OCT)], axis=0))
                b_head, b_rest = bbar[part]
                terms.append(jnp.dot(b_head, ca_head, preferred_element_type=F32)
                             + jnp.dot(b_head, ca_rest, preferred_element_type=F32)
                             + jnp.dot(b_rest, ca_head, preferred_element_type=F32))
            both = (terms[0] - terms[1]).astype(BF16)
            lag_blk += [both[:, :LANES], both[:, LANES:]]
        for m in range(CHUNK // 2):
            tz_ref[m, :LANES, :LANES] = lag_blk[2 * m]
            tz_ref[m, :LANES, LANES:] = lag_blk[2 * m + 1]
            tz_ref[m, LANES:, :LANES] = lag_blk[2 * m - 1] if m else jnp.zeros((LANES, LANES), BF16)
            tz_ref[m, LANES:, LANES:] = lag_blk[2 * m]
        bc = bc_ref[:, 0].reshape(OCT_IN, 4 * SSM_STATE)
        g_row = group_of(bc.shape, 0)
        w_col = (lax.broadcasted_iota(jnp.int32, bc.shape, 1) // SSM_STATE) % 2
        for qq in range(OCT // 2):
            bs_ref[:, qq * MXU_DIM:(qq + 1) * MXU_DIM] = jnp.where(g_row == 2 * qq + w_col, bc, 0.0).astype(BF16)
        g_col = group_of((SSM_STATE, LANES), 1)
        for s in range(CHUNK):
            c_s = (ccr_ref[s + 1], -cci_ref[s + 1])
            for qq in range(OCT // 2):
                for part in range(2):
                    for w in range(2):
                        r0 = qq * MXU_DIM + part * LANES + w * SSM_STATE
                        cs_ref[r0:r0 + SSM_STATE, s * LANES:(s + 1) * LANES] = jnp.where(
                            g_col == 2 * qq + w, c_s[part], 0.0).astype(BF16)

    def slab(ref, t, rows):
        return ref[pl.ds(t, rows, stride=CHUNK), :]

    for t in range(CHUNK):
        ub_ref[t // 2, :, (t % 2) * LANES:(t % 2 + 1) * LANES] = slab(u_ref, t, n_chunks).astype(BF16)

    z = jnp.dot(ub_ref[0], bs_ref[:MXU_DIM, :], preferred_element_type=F32)
    for tt in range(1, CHUNK // 2):
        z = z + jnp.dot(ub_ref[tt], bs_ref[tt * MXU_DIM:(tt + 1) * MXU_DIM, :], preferred_element_type=F32)
    um = jnp.concatenate([jnp.broadcast_to(um_ref[t:t + 1, :], (SUBLANES, LANES)) for t in range(CHUNK)], axis=1)
    zm = jnp.dot(um.astype(BF16), bs_ref[...], preferred_element_type=F32)[0:1, :]
    for ct in range(n_tiles):
        cols = slice(ct * LANES, (ct + 1) * LANES)
        buf_a[ct, SCAN_PAD:, :] = z[:, cols]
        for buf, pad in ((buf_a, SCAN_PAD), (buf_b, SCAN_PAD), (cb_a, blk_pad), (cb_b, blk_pad)):
            buf[ct, pad - 1:pad, :] = zm[:, cols]

    def within_chunk(ss):
        acc = jnp.dot(ub_ref[0], tz_ref[ss], preferred_element_type=F32)
        for tt in range(1, ss + 1):
            acc = acc + jnp.dot(ub_ref[tt], tz_ref[ss - tt], preferred_element_type=F32)
        yt_ref[ss] = acc

    assert LOCAL_STEPS + blk_steps >= CHUNK // 2 and (1 << LOCAL_STEPS) == SUBLANES
    todo = list(range(CHUNK // 2 - 1, -1, -1))
    blocked = (n_blocks, SUBLANES, LANES)

    def lanes_of(pair):
        return slice(pair * LANES, (pair + 1) * LANES)

    def combine(x, p, ar, ai):
        return x[0] + ar * p[0] - ai * p[1], x[1] + ar * p[1] + ai * p[0]

    src, dst = buf_a, buf_b
    for step in range(LOCAL_STEPS):
        shift = 1 << step
        within_chunk(todo.pop(0))
        for pair in range(OCT // 2):
            ar, ai = (loc_ref[0, step, part, :, lanes_of(pair)] for part in range(2))
            x = [src[2 * pair + part, SCAN_PAD:, :].reshape(blocked) for part in range(2)]
            p = [src[2 * pair + part, SCAN_PAD - shift:SCAN_PAD - shift + n_chunks, :].reshape(blocked)
                 for part in range(2)]
            for part, val in enumerate(combine(x, p, ar, ai)):
                dst[2 * pair + part, SCAN_PAD:, :] = val.reshape(n_chunks, LANES)
        src, dst = dst, src

    for ct in range(n_tiles):
        cb_a[ct, blk_pad:, :] = src[ct, pl.ds(SCAN_PAD + SUBLANES - 1, n_blocks, stride=SUBLANES), :]
    csrc, cdst = cb_a, cb_b
    for step in range(blk_steps):
        shift = 1 << step
        if todo:
            within_chunk(todo.pop(0))
        for pair in range(OCT // 2):
            ar = blk_ref[0, step:step + 1, lanes_of(pair)]
            ai = blk_ref[0, blk_steps + step:blk_steps + step + 1, lanes_of(pair)]
            x = [csrc[2 * pair + part, blk_pad:, :] for part in range(2)]
            p = [csrc[2 * pair + part, blk_pad - shift:blk_pad - shift + n_blocks, :] for part in range(2)]
            for part, val in enumerate(combine(x, p, ar, ai)):
                cdst[2 * pair + part, blk_pad:, :] = val
        csrc, cdst = cdst, csrc

    for pair in range(OCT // 2):
        for part in range(2):
            entering = csrc[2 * pair + part, blk_pad - 1:blk_pad - 1 + n_blocks, :]
            for row in range(SUBLANES):
                rep_ref[pair % 2, part, pl.ds(row, n_blocks, stride=SUBLANES), :] = entering
        ar, ai = (loc_ref[0, LOCAL_STEPS, part, :, lanes_of(pair)] for part in range(2))
        x = [src[2 * pair + part, SCAN_PAD:, :].reshape(blocked) for part in range(2)]
        p = [rep_ref[pair % 2, part].reshape(blocked) for part in range(2)]
        for part, val in enumerate(combine(x, p, ar, ai)):
            dst[2 * pair + part, SCAN_PAD:, :] = val.reshape(n_chunks, LANES)

    start_state = jnp.concatenate(
        [dst[ct, SCAN_PAD - 1:SCAN_PAD - 1 + n_chunks, :] for ct in range(n_tiles)], axis=1).astype(BF16)
    for ss in range(CHUNK // 2):
        cols = slice(ss * MXU_DIM, (ss + 1) * MXU_DIM)
        acc = yt_ref[ss] + jnp.dot(start_state, cs_ref[:, cols], preferred_element_type=F32)
        for half in range(2):
            t = 2 * ss + half
            y = acc[:, half * LANES:(half + 1) * LANES] + d_ref[0] * slab(u_ref, t, n_chunks)
            y_ref[pl.ds(t, n_chunks, stride=CHUNK), :] = 0.5 * y * (1.0 + lax.erf(y * (1.0 / math.sqrt(2.0))))


def _ssm(u, u_meta, bb2, bc, cc_re, cc_im, loc, blk, dp, bsz, seq):
    n_chunks = seq // CHUNK
    n_blocks = n_chunks // SUBLANES
    assert n_blocks & (n_blocks - 1) == 0 and N_META == CHUNK
    octet = lambda o, b: (o, 0, 0)
    n_tiles = OCT_STATE // LANES
    scan_buf = pltpu.VMEM((n_tiles, SCAN_PAD + n_chunks, LANES), F32)
    block_buf = pltpu.VMEM((n_tiles, n_blocks // 2 + n_blocks, LANES), F32)
    return pl.pallas_call(
        functools.partial(_ssm_kernel, n_chunks=n_chunks),
        grid=(N_OCT, bsz),
        in_specs=[
            pl.BlockSpec((seq, LANES), lambda o, b: (b, o)),
            pl.BlockSpec((N_META, LANES), lambda o, b: (0, o)),
            pl.BlockSpec((2, 1, LANES, 2 * SSM_STATE), lambda o, b: (0, o, 0, 0)),
            pl.BlockSpec((CHUNK, 1, LANES, 4 * SSM_STATE), lambda o, b: (0, o, 0, 0)),
            pl.BlockSpec((CHUNK + 1, SSM_STATE, LANES), lambda o, b: (0, 0, o)),
            pl.BlockSpec((CHUNK + 1, SSM_STATE, LANES), lambda o, b: (0, 0, o)),
            pl.BlockSpec((1,) + loc.shape[1:], lambda o, b: (o, 0, 0, 0, 0)),
            pl.BlockSpec((1,) + blk.shape[1:], octet),
            pl.BlockSpec((1, 1, LANES), octet),
        ],
        out_specs=pl.BlockSpec((seq, LANES), lambda o, b: (b, o)),
        out_shape=jax.ShapeDtypeStruct((bsz * seq, SSM_WIDTH), F32),
        scratch_shapes=[
            pltpu.VMEM((CHUNK // 2, MXU_DIM, MXU_DIM), BF16),
            pltpu.VMEM((OCT_IN, OCT_STATE), BF16),
            pltpu.VMEM((OCT_STATE, OCT_IN), BF16),
            pltpu.VMEM((CHUNK // 2, n_chunks, MXU_DIM), BF16), scan_buf, scan_buf, block_buf, block_buf,
            pltpu.VMEM((2, 2, n_chunks, LANES), F32),
            pltpu.VMEM((CHUNK // 2, n_chunks, MXU_DIM), F32)],
        compiler_params=pltpu.CompilerParams(
            dimension_semantics=("arbitrary", "arbitrary"), vmem_limit_bytes=VMEM_LIMIT),
        name="s5_chunked",
    )(u, u_meta, bb2, bc, cc_re, cc_im, loc, blk, dp)


def _ssm_weights(a_re, a_im, log_dt, b_re, b_im, c_re, c_im, d_skip, n_chunks):
    dt = jnp.exp(log_dt)[:, None]
    lam_re, lam_im = a_re * dt, a_im * dt

    def power(n, transposed=False):
        n = n[:, None, None]
        l_re, l_im = (lam_re.T, lam_im.T) if transposed else (lam_re, lam_im)
        mag = jnp.exp(n * l_re)
        return mag * jnp.cos(n * l_im), mag * jnp.sin(n * l_im)

    ab_re, ab_im = jnp.exp(lam_re) * jnp.cos(lam_im), jnp.exp(lam_re) * jnp.sin(lam_im)
    den = a_re * a_re + a_im * a_im
    nr, ni = ab_re - 1.0, ab_im
    f_re = (nr * a_re + ni * a_im) / den
    f_im = (ni * a_re - nr * a_im) / den
    bb_re = f_re[..., None] * b_re - f_im[..., None] * b_im
    bb_im = f_re[..., None] * b_im + f_im[..., None] * b_re

    p_re, p_im = power(jnp.arange(CHUNK + 1, dtype=F32))

    ct_re, ct_im = c_re.transpose(2, 0, 1)[None], c_im.transpose(2, 0, 1)[None]
    pt_re, pt_im = (t[..., None] for t in power(jnp.arange(CHUNK + 1, dtype=F32), transposed=True))
    ca_re = ct_re * pt_re - ct_im * pt_im
    ca_im = ct_re * pt_im + ct_im * pt_re
    cc_re = ca_re.reshape(CHUNK + 1, SSM_STATE, SSM_WIDTH)
    cc_im = ca_im.reshape(CHUNK + 1, SSM_STATE, SSM_WIDTH)

    r_re, r_im = p_re[CHUNK - 1::-1][:, :, None, :], p_im[CHUNK - 1::-1][:, :, None, :]
    bt_re, bt_im = bb_re.transpose(0, 2, 1)[None], bb_im.transpose(0, 2, 1)[None]
    bb2 = jnp.stack([jnp.concatenate([t, t], axis=-1).reshape(N_OCT, LANES, 2 * SSM_STATE) for t in (bt_re, bt_im)])
    bs_re = r_re * bt_re - r_im * bt_im
    bs_im = r_re * bt_im + r_im * bt_re
    bc = jnp.concatenate([bs_re, bs_re, bs_im, bs_im], axis=-1).reshape(CHUNK, N_OCT, LANES, 4 * SSM_STATE)

    rows = jnp.arange(SUBLANES)
    strides = 2 ** jnp.arange(LOCAL_STEPS)
    in_block = (rows[None, :] >= strides[:, None])[:, :, None, None]
    blk_steps = (n_chunks // SUBLANES).bit_length() - 1

    def pack(m):
        m = m.reshape(m.shape[:-2] + (N_OCT, OCT_STATE // 2))
        return jnp.moveaxis(m, -2, 0)

    loc, blk = [], []
    for carry, block_stride in zip(power(CHUNK * (rows + 1.0)), power(CHUNK * SUBLANES * 2.0 ** jnp.arange(blk_steps))):
        steps = jnp.where(in_block, carry[strides - 1][:, None], 0.0)
        loc.append(pack(jnp.concatenate([steps, carry[None]], axis=0)))
        blk.append(pack(block_stride))
    loc = jnp.stack(loc, axis=2)
    blk = jnp.concatenate(blk, axis=1)

    dp = d_skip.reshape(N_OCT, 1, LANES)
    return bb2, bc, cc_re, cc_im, loc, blk, dp


def _out_ffn_kernel(x_ref, o_ref, y_ref, wglu_ref, bglu_ref, sg_ref, wout_ref, pmg_ref, pfg_ref,
                    wg_ref, wu_ref, wd_ref, pog_ref, out_ref):
    y = y_ref[...]
    gate = jnp.dot(y.astype(BF16), wglu_ref[...], preferred_element_type=F32) + bglu_ref[...]
    y = _rms(y * jax.nn.sigmoid(gate), sg_ref[...]).astype(BF16)
    mix = (jnp.dot(o_ref[...], wout_ref[:ATTN_WIDTH, :], preferred_element_type=F32)
           + jnp.dot(y, wout_ref[ATTN_WIDTH:, :], preferred_element_type=F32))
    h1 = x_ref[...] + _rms(mix, pmg_ref[...])
    h2 = _rms(h1, pfg_ref[...]).astype(BF16)
    g = jnp.dot(h2, wg_ref[...], preferred_element_type=F32)
    up = jnp.dot(h2, wu_ref[...], preferred_element_type=F32)
    f = (g * jax.nn.sigmoid(g) * up).astype(BF16)
    f = jnp.dot(f, wd_ref[...], preferred_element_type=F32)
    out_ref[...] = h1 + _rms(f, pog_ref[...])


def _out_ffn(x2d, o, y, wglu, bglu, sg, wout, pmg, pfg, wg, wu, wd, pog, tm):
    n = x2d.shape[0]
    row = lambda i: (i, 0)
    const = lambda i: (0, 0)

    def resident(shape):
        return pl.BlockSpec(shape, const, pipeline_mode=pl.Buffered(1))

    return pl.pallas_call(
        _out_ffn_kernel,
        grid=(n // tm,),
        in_specs=[
            pl.BlockSpec((tm, D_MODEL), row),
            pl.BlockSpec((tm, ATTN_WIDTH), row),
            pl.BlockSpec((tm, SSM_WIDTH), row),
            resident((SSM_WIDTH, SSM_WIDTH)),
            resident((1, SSM_WIDTH)),
            resident((1, SSM_WIDTH)),
            resident((D_MODEL, D_MODEL)),
            resident((1, D_MODEL)),
            resident((1, D_MODEL)),
            resident((D_MODEL, D_FF)),
            resident((D_MODEL, D_FF)),
            resident((D_FF, D_MODEL)),
            resident((1, D_MODEL)),
        ],
        out_specs=pl.BlockSpec((tm, D_MODEL), row),
        out_shape=jax.ShapeDtypeStruct((n, D_MODEL), F32),
        compiler_params=pltpu.CompilerParams(
            dimension_semantics=("arbitrary",), vmem_limit_bytes=VMEM_LIMIT),
        name="out_ffn",
    )(x2d, o, y, wglu, bglu, sg, wout, pmg, pfg, wg, wu, wd, pog)


def _rope_angles(pos):
    d = jnp.arange(LANES) % QK_DIM
    inv_freq = ROPE_THETA ** (-(2 * (d % (ROT_DIM // 2))).astype(F32) / ROT_DIM)
    ang = pos[:, None] * jnp.where(d < ROT_DIM, inv_freq, 0.0)[None, :]
    return jnp.cos(ang), jnp.sin(ang)


def _scale_w_in(w):
    scale = math.log2(math.e) / math.sqrt(QK_DIM)
    col_scale = jnp.where(jnp.arange(w.shape[1]) < ATTN_WIDTH, scale, 1.0).astype(F32)
    return (w * col_scale[None, :]).astype(BF16)


def kernel(x, meta, pre_mix_g, w_in, lambda_q1, lambda_k1, lambda_q2, lambda_k2, subln_g, a_re, a_im, log_dt,
           b_re, b_im, c_re, c_im, d_skip, w_glu, b_glu, ssm_out_g, w_out, post_mix_g, pre_ffn_g, w_gate,
           w_up, w_down, post_ffn_g):
    bsz, seq, _ = x.shape
    n = bsz * seq
    x2d = x.reshape(n, D_MODEL)
    row = lambda t: t[0].reshape(1, -1)

    w_in_p = _scale_w_in(w_in[0])
    g0 = row(pre_mix_g)
    q, k, v, u = _in_proj(x2d, g0, w_in_p, N_META, seq, IN_ROW_TILE, True)
    _, k_m, v_m, u_m = _in_proj(meta, g0, w_in_p, 0, N_META, N_META, False)

    pad_meta = lambda t: jnp.pad(t, ((0, LANES - N_META), (0, 0)))
    o = _attention(q, k, v, pad_meta(k_m).T, pad_meta(v_m), lambda_q1, lambda_k1, lambda_q2, lambda_k2,
                   row(subln_g), bsz, seq)

    ssm_w = _ssm_weights(a_re[0], a_im[0], log_dt[0], b_re[0], b_im[0], c_re[0], c_im[0], d_skip[0],
                         seq // CHUNK)
    y = _ssm(u, u_m, *ssm_w, bsz, seq)

    out = _out_ffn(x2d, o, y, w_glu[0].astype(BF16), row(b_glu), row(ssm_out_g), w_out[0].astype(BF16),
                   row(post_mix_g), row(pre_ffn_g), w_gate[0].astype(BF16), w_up[0].astype(BF16),
                   w_down[0].astype(BF16), row(post_ffn_g), ROW_TILE)
    return out.reshape(bsz, seq, D_MODEL)
```

```python
import functools
import math

import jax
import jax.numpy as jnp
from jax import lax
from jax.experimental import pallas as pl
from jax.experimental.pallas import tpu as pltpu

D_MODEL = 1024
N_META = 16
N_HEADS = 4
QK_DIM = 64
V_DIM = 128
ROT_DIM = 16
ROPE_THETA = 500000.0
SSM_GROUP = 16
N_GROUPS = 32
SSM_STATE = 64
SSM_WIDTH = 512
ATTN_WIDTH = 512
D_FF = 2816
EPS = 1e-6
LAM_INIT = 0.8 - 0.6 * math.exp(-0.3 * 0)

LANES = 128
SUBLANES = 8
MXU_DIM = 256
CHUNK = 16
OCT = LANES // SSM_GROUP
N_OCT = N_GROUPS // OCT
OCT_IN = CHUNK * LANES
OCT_STATE = OCT * 2 * SSM_STATE
SCAN_PAD = SUBLANES
LOCAL_STEPS = 3

ROW_TILE = 512
IN_ROW_TILE = 1024
X_BUFFERS = 3
Q_TILE = 512
KV_TILE = 512
ATTN_UNROLL = 4
VMEM_LIMIT = 56 * 1024 * 1024

F32 = jnp.float32
BF16 = jnp.bfloat16


def _rms(x, g):
    return x * lax.rsqrt(jnp.mean(x * x, axis=-1, keepdims=True) + EPS) * g


def _in_proj_kernel(x_hbm, g_ref, w_ref, cr_ref, sr_ref, cb_ref, sb_ref, q_ref, k_ref, v_ref, u_ref,
                    x_buf, x_sem, *, k_transposed, n_steps):
    tm = x_buf.shape[1]
    step = pl.program_id(0)

    def x_copy(j):
        return pltpu.make_async_copy(x_hbm.at[pl.ds(j * tm, tm), :], x_buf.at[j % X_BUFFERS], x_sem.at[j % X_BUFFERS])

    @pl.when(step == 0)
    def _():
        for j in range(min(X_BUFFERS - 1, n_steps)):
            x_copy(j).start()

    @pl.when(step + X_BUFFERS - 1 < n_steps)
    def _():
        x_copy(step + X_BUFFERS - 1).start()

    x_copy(step).wait()
    h = _rms(x_buf[step % X_BUFFERS], g_ref[...]).astype(BF16)
    proj = jnp.dot(h, w_ref[...], preferred_element_type=F32)
    cos = cb_ref[0] * cr_ref[...] - sb_ref[0] * sr_ref[...]
    sin = sb_ref[0] * cr_ref[...] + cb_ref[0] * sr_ref[...]
    d = lax.broadcasted_iota(jnp.int32, (1, LANES), 1) % QK_DIM
    sa = sin * jnp.where(d < ROT_DIM // 2, -1.0, 0.0)
    sb = sin * jnp.where((d >= ROT_DIM // 2) & (d < ROT_DIM), 1.0, 0.0)

    def rope(t):
        return (t * cos + pltpu.roll(t, LANES - ROT_DIM // 2, axis=1) * sa
                + pltpu.roll(t, ROT_DIM // 2, axis=1) * sb)

    for c in range(ATTN_WIDTH // LANES):
        sl = slice(c * LANES, (c + 1) * LANES)
        q_ref[:, sl] = rope(proj[:, c * LANES:(c + 1) * LANES]).astype(BF16)
        k_c = rope(proj[:, ATTN_WIDTH + c * LANES:ATTN_WIDTH + (c + 1) * LANES])
        if k_transposed:
            k_ref[sl, :] = k_c.T.astype(BF16)
        else:
            k_ref[:, sl] = k_c.astype(BF16)
    v_ref[...] = proj[:, 2 * ATTN_WIDTH:3 * ATTN_WIDTH].astype(BF16)
    u_ref[...] = proj[:, 3 * ATTN_WIDTH:]


def _in_proj(x2d, g, w, first_pos, seq, tm, k_transposed):
    n = x2d.shape[0]
    token_major = pl.BlockSpec((tm, ATTN_WIDTH), lambda i: (i, 0))
    k_spec = pl.BlockSpec((ATTN_WIDTH, tm), lambda i: (0, i)) if k_transposed else token_major
    k_shape = (ATTN_WIDTH, n) if k_transposed else (n, ATTN_WIDTH)
    tiles_per_seq = seq // tm
    cos_r, sin_r = _rope_angles(jnp.arange(tm, dtype=F32))
    cos_b, sin_b = _rope_angles(first_pos + tm * jnp.arange(tiles_per_seq, dtype=F32))
    row = lambda i: (i, 0)
    const = lambda i: (0, 0)
    base = pl.BlockSpec((1, 1, LANES), lambda i: (i % tiles_per_seq, 0, 0))
    return pl.pallas_call(
        functools.partial(_in_proj_kernel, k_transposed=k_transposed, n_steps=n // tm),
        grid=(n // tm,),
        in_specs=[
            pl.BlockSpec(memory_space=pl.ANY),
            pl.BlockSpec((1, D_MODEL), const),
            pl.BlockSpec((D_MODEL, 4 * ATTN_WIDTH), const),
            pl.BlockSpec((tm, LANES), const),
            pl.BlockSpec((tm, LANES), const),
            base, base,
        ],
        out_specs=[token_major, k_spec, token_major, token_major],
        out_shape=[jax.ShapeDtypeStruct((n, ATTN_WIDTH), BF16), jax.ShapeDtypeStruct(k_shape, BF16),
                   jax.ShapeDtypeStruct((n, ATTN_WIDTH), BF16), jax.ShapeDtypeStruct((n, SSM_WIDTH), F32)],
        compiler_params=pltpu.CompilerParams(
            dimension_semantics=("arbitrary",), vmem_limit_bytes=VMEM_LIMIT),
        scratch_shapes=[pltpu.VMEM((X_BUFFERS, tm, D_MODEL), F32), pltpu.SemaphoreType.DMA((X_BUFFERS,))],
        name="in_proj",
    )(x2d, g, w, cos_r, sin_r, cos_b[:, None, :], sin_b[:, None, :])


def _attn_kernel(*refs):
    n_pairs = refs[0].shape[0] // (2 * Q_TILE)

    def q_pair(sp, carry):
        _attn_q_pair(sp, *refs)
        return carry

    lax.fori_loop(0, n_pairs, q_pair, 0)


def _attn_q_pair(sp, q1_ref, q2_ref, k1_ref, k2_ref, v_ref, km1_ref, km2_ref, vm_ref,
                 lq1_ref, lk1_ref, lq2_ref, lk2_ref, sg_ref,
                 o_ref, m_ref, l_ref, acc_ref, p_ref):
    own = lax.broadcasted_iota(jnp.int32, (Q_TILE, LANES), 1) // QK_DIM == pl.program_id(1) % 2
    q_rows = [pl.ds(pl.multiple_of((2 * sp + c) * Q_TILE, Q_TILE), Q_TILE) for c in range(2)]
    qs = [tuple(jnp.where(own, r[rows, :], jnp.zeros((Q_TILE, LANES), BF16)) for r in (q1_ref, q2_ref))
          for rows in q_rows]
    k_refs, km_refs = (k1_ref, k2_ref), (km1_ref, km2_ref)
    nt = (((1,), (1,)), ((), ()))
    wide = (Q_TILE, LANES)

    def block(c, k, mask, slot, prev):
        n_col = k[0].shape[1] // LANES
        for i in range(2):
            s = jnp.dot(qs[c][i], k[i], preferred_element_type=F32)
            if mask is not None:
                s = jnp.where(mask, s, -jnp.inf)
            cols = [s[:, j * LANES:(j + 1) * LANES] for j in range(n_col)]
            m_tile = functools.reduce(jnp.maximum, cols)
            m_new = jnp.broadcast_to(jnp.max(m_tile, axis=1, keepdims=True), wide)
            if prev is not None:
                m_old = m_ref[c, i]
                m_new = jnp.maximum(m_old, m_new)
                alpha = jnp.exp2(m_old - m_new)
            m_ref[c, i] = m_new
            ps = [jnp.exp2(col - m_new) for col in cols]
            l_tile = functools.reduce(jnp.add, ps)
            l_ref[c, i] = l_tile if prev is None else alpha * l_ref[c, i] + l_tile
            for j in range(n_col):
                p_ref[c, slot, i, :, j * LANES:(j + 1) * LANES] = ps[j].astype(BF16)
            if prev is not None:
                slot_prev, v_prev = prev
                pv = jnp.dot(p_ref[c, slot_prev, i], v_prev, preferred_element_type=F32)
                acc_ref[c, i] = (acc_ref[c, i] + pv) * alpha

    def tile(ref, t):
        return ref[pl.ds(pl.multiple_of(t * KV_TILE, KV_TILE), KV_TILE), :]

    def ktile(t):
        return tuple(r[:, pl.ds(pl.multiple_of(t * KV_TILE, KV_TILE), KV_TILE)] for r in k_refs)

    acc_ref[...] = jnp.zeros(acc_ref.shape, F32)
    d0 = 2 * sp
    causal = (lax.broadcasted_iota(jnp.int32, (Q_TILE, KV_TILE), 1)
              <= lax.broadcasted_iota(jnp.int32, (Q_TILE, KV_TILE), 0))
    k_d0 = ktile(d0)
    block(0, k_d0, causal, 0, None)
    block(1, k_d0, None, 0, None)
    block(1, ktile(d0 + 1), causal, 1, (0, tile(v_ref, d0)))
    pending = (0, 1)

    def earlier(t0, n):
        for d in range(n):
            k_t = ktile(t0 + d)
            for c in range(2):
                before = jnp.where(t0 == 0, d0 + c, t0 - 1) if d == 0 else t0 + d - 1
                src = pending[c] if d % 2 == 0 else 1 - pending[c]
                block(c, k_t, None, 1 - src, (src, tile(v_ref, before)))

    def unrolled(jj, carry):
        earlier(ATTN_UNROLL * jj, ATTN_UNROLL)
        return carry

    lax.fori_loop(0, d0 // ATTN_UNROLL, unrolled, 0)
    for n in (ATTN_UNROLL // 2, ATTN_UNROLL // 4):
        if n >= 2:
            done = (d0 // (2 * n)) * (2 * n)
            pl.when(d0 - done >= n)(functools.partial(earlier, done, n))

    meta_mask = lax.broadcasted_iota(jnp.int32, wide, 1) < N_META
    k_meta = tuple(r[...] for r in km_refs)
    for c in range(2):
        before = jnp.where(sp == 0, d0 + c, d0 - 1)
        block(c, k_meta, meta_mask, 1 - pending[c], (pending[c], tile(v_ref, before)))
    for c in range(2):
        for i in range(2):
            acc_ref[c, i] = acc_ref[c, i] + jnp.dot(p_ref[c, 1 - pending[c], i, :, :LANES], vm_ref[...],
                                                    preferred_element_type=F32)

    lam = (jnp.exp(jnp.sum(lq1_ref[...] * lk1_ref[...], axis=1, keepdims=True))
           - jnp.exp(jnp.sum(lq2_ref[...] * lk2_ref[...], axis=1, keepdims=True)) + LAM_INIT)
    for c in range(2):
        l1 = jnp.sum(l_ref[c, 0], axis=1, keepdims=True)
        l2 = jnp.sum(l_ref[c, 1], axis=1, keepdims=True)
        o = acc_ref[c, 0] / l1 - lam * (acc_ref[c, 1] / l2)
        o_ref[q_rows[c], :] = (_rms(o, sg_ref[...]) * (1.0 - LAM_INIT)).astype(BF16)


def _attention(q, k, v, km, vm, lq1, lk1, lq2, lk2, sg, bsz, seq):
    assert Q_TILE == KV_TILE and seq % (2 * Q_TILE) == 0
    pairs = N_HEADS // 2
    kvmap = lambda b, h: (b, h)
    map1 = pl.BlockSpec((seq, LANES), lambda b, h: (b, h // 2))
    map2 = pl.BlockSpec((seq, LANES), lambda b, h: (b, pairs + h // 2))
    kmap1 = pl.BlockSpec((LANES, seq), lambda b, h: (h // 2, b))
    kmap2 = pl.BlockSpec((LANES, seq), lambda b, h: (pairs + h // 2, b))
    meta1 = pl.BlockSpec((LANES, LANES), lambda b, h: (h // 2, 0))
    meta2 = pl.BlockSpec((LANES, LANES), lambda b, h: (pairs + h // 2, 0))
    const = lambda b, h: (0, 0)
    vec = pl.BlockSpec((1, QK_DIM), const)
    return pl.pallas_call(
        _attn_kernel,
        grid=(bsz, N_HEADS),
        in_specs=[
            map1, map2, kmap1, kmap2,
            pl.BlockSpec((seq, LANES), kvmap),
            meta1, meta2,
            pl.BlockSpec((LANES, LANES), lambda b, h: (0, h)),
            vec, vec, vec, vec,
            pl.BlockSpec((1, V_DIM), const),
        ],
        out_specs=pl.BlockSpec((seq, LANES), kvmap),
        out_shape=jax.ShapeDtypeStruct((bsz * seq, ATTN_WIDTH), BF16),
        scratch_shapes=[
            pltpu.VMEM((2, 2, Q_TILE, LANES), F32),
            pltpu.VMEM((2, 2, Q_TILE, LANES), F32),
            pltpu.VMEM((2, 2, Q_TILE, V_DIM), F32),
            pltpu.VMEM((2, 2, 2, Q_TILE, KV_TILE), BF16),
        ],
        compiler_params=pltpu.CompilerParams(
            dimension_semantics=("arbitrary", "arbitrary"), vmem_limit_bytes=VMEM_LIMIT),
        name="diff_attention",
    )(q, q, k, k, v, km, km, vm, lq1, lk1, lq2, lk2, sg)


def _ssm_kernel(u_ref, um_ref, bb_ref, bc_ref, ccr_ref, cci_ref, loc_ref, blk_ref, d_ref, y_ref,
                tz_ref, bs_ref, cs_ref, ub_ref, buf_a, buf_b, cb_a, cb_b, rep_ref, yt_ref, *, n_chunks):
    n_tiles = OCT_STATE // LANES
    n_blocks = n_chunks // SUBLANES
    blk_steps = n_blocks.bit_length() - 1
    blk_pad = cb_a.shape[1] - n_blocks

    @pl.when((pl.program_id(0) == 0) & (pl.program_id(1) == 0))
    def _():
        for buf, pad in ((buf_a, SCAN_PAD), (buf_b, SCAN_PAD), (cb_a, blk_pad), (cb_b, blk_pad)):
            buf[:, :pad, :] = jnp.zeros((n_tiles, pad, LANES), F32)

    @pl.when(pl.program_id(1) == 0)
    def _():
        def group_of(shape, axis):
            return (lax.broadcasted_iota(jnp.int32, shape, axis) // SSM_GROUP) % OCT

        def split(x):
            head = x.astype(BF16)
            return head, (x - head.astype(F32)).astype(BF16)

        wide_shape = (LANES, OCT * SSM_STATE)
        own_cols = group_of(wide_shape, 0) == lax.broadcasted_iota(jnp.int32, wide_shape, 1) // SSM_STATE
        bbar = [split(jnp.where(own_cols, jnp.concatenate([bb_ref[part, 0]] * (OCT // 2), axis=1), 0.0))
                for part in range(2)]
        lag_blk = []
        for n in range(0, CHUNK, 2):
            terms = []
            for part, c_ref in enumerate((ccr_ref, cci_ref)):
                c_n = jnp.concatenate([c_ref[n], c_ref[n + 1]], axis=1)
                g_lane = group_of(c_n.shape, 1)
                ca_head, ca_rest = split(jnp.concatenate(
                    [jnp.where(g_lane == g, c_n, 0.0) for g in range(OCT)], axis=0))
                b_head, b_rest = bbar[part]
                terms.append(jnp.dot(b_head, ca_head, preferred_element_type=F32)
                             + jnp.dot(b_head, ca_rest, preferred_element_type=F32)
                             + jnp.dot(b_rest, ca_head, preferred_element_type=F32))
            both = (terms[0] - terms[1]).astype(BF16)
            lag_blk += [both[:, :LANES], both[:, LANES:]]
        for m in range(CHUNK // 2):
            tz_ref[m, :LANES, :LANES] = lag_blk[2 * m]
            tz_ref[m, :LANES, LANES:] = lag_blk[2 * m + 1]
            tz_ref[m, LANES:, :LANES] = lag_blk[2 * m - 1] if m else jnp.zeros((LANES, LANES), BF16)
            tz_ref[m, LANES:, LANES:] = lag_blk[2 * m]
        bc = bc_ref[:, 0].reshape(OCT_IN, 4 * SSM_STATE)
        g_row = group_of(bc.shape, 0)
        w_col = (lax.broadcasted_iota(jnp.int32, bc.shape, 1) // SSM_STATE) % 2
        for qq in range(OCT // 2):
            bs_ref[:, qq * MXU_DIM:(qq + 1) * MXU_DIM] = jnp.where(g_row == 2 * qq + w_col, bc, 0.0).astype(BF16)
        g_col = group_of((SSM_STATE, LANES), 1)
        for s in range(CHUNK):
            c_s = (ccr_ref[s + 1], -cci_ref[s + 1])
            for qq in range(OCT // 2):
                for part in range(2):
                    for w in range(2):
                        r0 = qq * MXU_DIM + part * LANES + w * SSM_STATE
                        cs_ref[r0:r0 + SSM_STATE, s * LANES:(s + 1) * LANES] = jnp.where(
                            g_col == 2 * qq + w, c_s[part], 0.0).astype(BF16)

    def slab(ref, t, rows):
        return ref[pl.ds(t, rows, stride=CHUNK), :]

    for t in range(CHUNK):
        ub_ref[t // 2, :, (t % 2) * LANES:(t % 2 + 1) * LANES] = slab(u_ref, t, n_chunks).astype(BF16)

    z = jnp.dot(ub_ref[0], bs_ref[:MXU_DIM, :], preferred_element_type=F32)
    for tt in range(1, CHUNK // 2):
        z = z + jnp.dot(ub_ref[tt], bs_ref[tt * MXU_DIM:(tt + 1) * MXU_DIM, :], preferred_element_type=F32)
    um = jnp.concatenate([jnp.broadcast_to(um_ref[t:t + 1, :], (SUBLANES, LANES)) for t in range(CHUNK)], axis=1)
    zm = jnp.dot(um.astype(BF16), bs_ref[...], preferred_element_type=F32)[0:1, :]
    for ct in range(n_tiles):
        cols = slice(ct * LANES, (ct + 1) * LANES)
        buf_a[ct, SCAN_PAD:, :] = z[:, cols]
        for buf, pad in ((buf_a, SCAN_PAD), (buf_b, SCAN_PAD), (cb_a, blk_pad), (cb_b, blk_pad)):
            buf[ct, pad - 1:pad, :] = zm[:, cols]

    def within_chunk(ss):
        acc = jnp.dot(ub_ref[0], tz_ref[ss], preferred_element_type=F32)
        for tt in range(1, ss + 1):
            acc = acc + jnp.dot(ub_ref[tt], tz_ref[ss - tt], preferred_element_type=F32)
        yt_ref[ss] = acc

    assert LOCAL_STEPS + blk_steps >= CHUNK // 2 and (1 << LOCAL_STEPS) == SUBLANES
    todo = list(range(CHUNK // 2 - 1, -1, -1))
    blocked = (n_blocks, SUBLANES, LANES)

    def lanes_of(pair):
        return slice(pair * LANES, (pair + 1) * LANES)

    def combine(x, p, ar, ai):
        return x[0] + ar * p[0] - ai * p[1], x[1] + ar * p[1] + ai * p[0]

    src, dst = buf_a, buf_b
    for step in range(LOCAL_STEPS):
        shift = 1 << step
        within_chunk(todo.pop(0))
        for pair in range(OCT // 2):
            ar, ai = (loc_ref[0, step, part, :, lanes_of(pair)] for part in range(2))
            x = [src[2 * pair + part, SCAN_PAD:, :].reshape(blocked) for part in range(2)]
            p = [src[2 * pair + part, SCAN_PAD - shift:SCAN_PAD - shift + n_chunks, :].reshape(blocked)
                 for part in range(2)]
            for part, val in enumerate(combine(x, p, ar, ai)):
                dst[2 * pair + part, SCAN_PAD:, :] = val.reshape(n_chunks, LANES)
        src, dst = dst, src

    for ct in range(n_tiles):
        cb_a[ct, blk_pad:, :] = src[ct, pl.ds(SCAN_PAD + SUBLANES - 1, n_blocks, stride=SUBLANES), :]
    csrc, cdst = cb_a, cb_b
    for step in range(blk_steps):
        shift = 1 << step
        if todo:
            within_chunk(todo.pop(0))
        for pair in range(OCT // 2):
            ar = blk_ref[0, step:step + 1, lanes_of(pair)]
            ai = blk_ref[0, blk_steps + step:blk_steps + step + 1, lanes_of(pair)]
            x = [csrc[2 * pair + part, blk_pad:, :] for part in range(2)]
            p = [csrc[2 * pair + part, blk_pad - shift:blk_pad - shift + n_blocks, :] for part in range(2)]
            for part, val in enumerate(combine(x, p, ar, ai)):
                cdst[2 * pair + part, blk_pad:, :] = val
        csrc, cdst = cdst, csrc

    for pair in range(OCT // 2):
        for part in range(2):
            entering = csrc[2 * pair + part, blk_pad - 1:blk_pad - 1 + n_blocks, :]
            for row in range(SUBLANES):
                rep_ref[pair % 2, part, pl.ds(row, n_blocks, stride=SUBLANES), :] = entering
        ar, ai = (loc_ref[0, LOCAL_STEPS, part, :, lanes_of(pair)] for part in range(2))
        x = [src[2 * pair + part, SCAN_PAD:, :].reshape(blocked) for part in range(2)]
        p = [rep_ref[pair % 2, part].reshape(blocked) for part in range(2)]
        for part, val in enumerate(combine(x, p, ar, ai)):
            dst[2 * pair + part, SCAN_PAD:, :] = val.reshape(n_chunks, LANES)

    start_state = jnp.concatenate(
        [dst[ct, SCAN_PAD - 1:SCAN_PAD - 1 + n_chunks, :] for ct in range(n_tiles)], axis=1).astype(BF16)
    for ss in range(CHUNK // 2):
        cols = slice(ss * MXU_DIM, (ss + 1) * MXU_DIM)
        acc = yt_ref[ss] + jnp.dot(start_state, cs_ref[:, cols], preferred_element_type=F32)
        for half in range(2):
            t = 2 * ss + half
            y = acc[:, half * LANES:(half + 1) * LANES] + d_ref[0] * slab(u_ref, t, n_chunks)
            y_ref[pl.ds(t, n_chunks, stride=CHUNK), :] = 0.5 * y * (1.0 + lax.erf(y * (1.0 / math.sqrt(2.0))))


def _ssm(u, u_meta, bb2, bc, cc_re, cc_im, loc, blk, dp, bsz, seq):
    n_chunks = seq // CHUNK
    n_blocks = n_chunks // SUBLANES
    assert n_blocks & (n_blocks - 1) == 0 and N_META == CHUNK
    octet = lambda o, b: (o, 0, 0)
    n_tiles = OCT_STATE // LANES
    scan_buf = pltpu.VMEM((n_tiles, SCAN_PAD + n_chunks, LANES), F32)
    block_buf = pltpu.VMEM((n_tiles, n_blocks // 2 + n_blocks, LANES), F32)
    return pl.pallas_call(
        functools.partial(_ssm_kernel, n_chunks=n_chunks),
        grid=(N_OCT, bsz),
        in_specs=[
            pl.BlockSpec((seq, LANES), lambda o, b: (b, o)),
            pl.BlockSpec((N_META, LANES), lambda o, b: (0, o)),
            pl.BlockSpec((2, 1, LANES, 2 * SSM_STATE), lambda o, b: (0, o, 0, 0)),
            pl.BlockSpec((CHUNK, 1, LANES, 4 * SSM_STATE), lambda o, b: (0, o, 0, 0)),
            pl.BlockSpec((CHUNK + 1, SSM_STATE, LANES), lambda o, b: (0, 0, o)),
            pl.BlockSpec((CHUNK + 1, SSM_STATE, LANES), lambda o, b: (0, 0, o)),
            pl.BlockSpec((1,) + loc.shape[1:], lambda o, b: (o, 0, 0, 0, 0)),
            pl.BlockSpec((1,) + blk.shape[1:], octet),
            pl.BlockSpec((1, 1, LANES), octet),
        ],
        out_specs=pl.BlockSpec((seq, LANES), lambda o, b: (b, o)),
        out_shape=jax.ShapeDtypeStruct((bsz * seq, SSM_WIDTH), F32),
        scratch_shapes=[
            pltpu.VMEM((CHUNK // 2, MXU_DIM, MXU_DIM), BF16),
            pltpu.VMEM((OCT_IN, OCT_STATE), BF16),
            pltpu.VMEM((OCT_STATE, OCT_IN), BF16),
            pltpu.VMEM((CHUNK // 2, n_chunks, MXU_DIM), BF16), scan_buf, scan_buf, block_buf, block_buf,
            pltpu.VMEM((2, 2, n_chunks, LANES), F32),
            pltpu.VMEM((CHUNK // 2, n_chunks, MXU_DIM), F32)],
        compiler_params=pltpu.CompilerParams(
            dimension_semantics=("arbitrary", "arbitrary"), vmem_limit_bytes=VMEM_LIMIT),
        name="s5_chunked",
    )(u, u_meta, bb2, bc, cc_re, cc_im, loc, blk, dp)


def _ssm_weights(a_re, a_im, log_dt, b_re, b_im, c_re, c_im, d_skip, n_chunks):
    dt = jnp.exp(log_dt)[:, None]
    lam_re, lam_im = a_re * dt, a_im * dt

    def power(n, transposed=False):
        n = n[:, None, None]
        l_re, l_im = (lam_re.T, lam_im.T) if transposed else (lam_re, lam_im)
        mag = jnp.exp(n * l_re)
        return mag * jnp.cos(n * l_im), mag * jnp.sin(n * l_im)

    ab_re, ab_im = jnp.exp(lam_re) * jnp.cos(lam_im), jnp.exp(lam_re) * jnp.sin(lam_im)
    den = a_re * a_re + a_im * a_im
    nr, ni = ab_re - 1.0, ab_im
    f_re = (nr * a_re + ni * a_im) / den
    f_im = (ni * a_re - nr * a_im) / den
    bb_re = f_re[..., None] * b_re - f_im[..., None] * b_im
    bb_im = f_re[..., None] * b_im + f_im[..., None] * b_re

    p_re, p_im = power(jnp.arange(CHUNK + 1, dtype=F32))

    ct_re, ct_im = c_re.transpose(2, 0, 1)[None], c_im.transpose(2, 0, 1)[None]
    pt_re, pt_im = (t[..., None] for t in power(jnp.arange(CHUNK + 1, dtype=F32), transposed=True))
    ca_re = ct_re * pt_re - ct_im * pt_im
    ca_im = ct_re * pt_im + ct_im * pt_re
    cc_re = ca_re.reshape(CHUNK + 1, SSM_STATE, SSM_WIDTH)
    cc_im = ca_im.reshape(CHUNK + 1, SSM_STATE, SSM_WIDTH)

    r_re, r_im = p_re[CHUNK - 1::-1][:, :, None, :], p_im[CHUNK - 1::-1][:, :, None, :]
    bt_re, bt_im = bb_re.transpose(0, 2, 1)[None], bb_im.transpose(0, 2, 1)[None]
    bb2 = jnp.stack([jnp.concatenate([t, t], axis=-1).reshape(N_OCT, LANES, 2 * SSM_STATE) for t in (bt_re, bt_im)])
    bs_re = r_re * bt_re - r_im * bt_im
    bs_im = r_re * bt_im + r_im * bt_re
    bc = jnp.concatenate([bs_re, bs_re, bs_im, bs_im], axis=-1).reshape(CHUNK, N_OCT, LANES, 4 * SSM_STATE)

    rows = jnp.arange(SUBLANES)
    strides = 2 ** jnp.arange(LOCAL_STEPS)
    in_block = (rows[None, :] >= strides[:, None])[:, :, None, None]
    blk_steps = (n_chunks // SUBLANES).bit_length() - 1

    def pack(m):
        m = m.reshape(m.shape[:-2] + (N_OCT, OCT_STATE // 2))
        return jnp.moveaxis(m, -2, 0)

    loc, blk = [], []
    for carry, block_stride in zip(power(CHUNK * (rows + 1.0)), power(CHUNK * SUBLANES * 2.0 ** jnp.arange(blk_steps))):
        steps = jnp.where(in_block, carry[strides - 1][:, None], 0.0)
        loc.append(pack(jnp.concatenate([steps, carry[None]], axis=0)))
        blk.append(pack(block_stride))
    loc = jnp.stack(loc, axis=2)
    blk = jnp.concatenate(blk, axis=1)

    dp = d_skip.reshape(N_OCT, 1, LANES)
    return bb2, bc, cc_re, cc_im, loc, blk, dp


def _out_ffn_kernel(x_ref, o_ref, y_ref, wglu_ref, bglu_ref, sg_ref, wout_ref, pmg_ref, pfg_ref,
                    wg_ref, wu_ref, wd_ref, pog_ref, out_ref):
    y = y_ref[...]
    gate = jnp.dot(y.astype(BF16), wglu_ref[...], preferred_element_type=F32) + bglu_ref[...]
    y = _rms(y * jax.nn.sigmoid(gate), sg_ref[...]).astype(BF16)
    mix = (jnp.dot(o_ref[...], wout_ref[:ATTN_WIDTH, :], preferred_element_type=F32)
           + jnp.dot(y, wout_ref[ATTN_WIDTH:, :], preferred_element_type=F32))
    h1 = x_ref[...] + _rms(mix, pmg_ref[...])
    h2 = _rms(h1, pfg_ref[...]).astype(BF16)
    g = jnp.dot(h2, wg_ref[...], preferred_element_type=F32)
    up = jnp.dot(h2, wu_ref[...], preferred_element_type=F32)
    f = (g * jax.nn.sigmoid(g) * up).astype(BF16)
    f = jnp.dot(f, wd_ref[...], preferred_element_type=F32)
    out_ref[...] = h1 + _rms(f, pog_ref[...])


def _out_ffn(x2d, o, y, wglu, bglu, sg, wout, pmg, pfg, wg, wu, wd, pog, tm):
    n = x2d.shape[0]
    row = lambda i: (i, 0)
    const = lambda i: (0, 0)

    def resident(shape):
        return pl.BlockSpec(shape, const, pipeline_mode=pl.Buffered(1))

    return pl.pallas_call(
        _out_ffn_kernel,
        grid=(n // tm,),
        in_specs=[
            pl.BlockSpec((tm, D_MODEL), row),
            pl.BlockSpec((tm, ATTN_WIDTH), row),
            pl.BlockSpec((tm, SSM_WIDTH), row),
            resident((SSM_WIDTH, SSM_WIDTH)),
            resident((1, SSM_WIDTH)),
            resident((1, SSM_WIDTH)),
            resident((D_MODEL, D_MODEL)),
            resident((1, D_MODEL)),
            resident((1, D_MODEL)),
            resident((D_MODEL, D_FF)),
            resident((D_MODEL, D_FF)),
            resident((D_FF, D_MODEL)),
            resident((1, D_MODEL)),
        ],
        out_specs=pl.BlockSpec((tm, D_MODEL), row),
        out_shape=jax.ShapeDtypeStruct((n, D_MODEL), F32),
        compiler_params=pltpu.CompilerParams(
            dimension_semantics=("arbitrary",), vmem_limit_bytes=VMEM_LIMIT),
        name="out_ffn",
    )(x2d, o, y, wglu, bglu, sg, wout, pmg, pfg, wg, wu, wd, pog)


def _rope_angles(pos):
    d = jnp.arange(LANES) % QK_DIM
    inv_freq = ROPE_THETA ** (-(2 * (d % (ROT_DIM // 2))).astype(F32) / ROT_DIM)
    ang = pos[:, None] * jnp.where(d < ROT_DIM, inv_freq, 0.0)[None, :]
    return jnp.cos(ang), jnp.sin(ang)


def _scale_w_in(w):
    scale = math.log2(math.e) / math.sqrt(QK_DIM)
    col_scale = jnp.where(jnp.arange(w.shape[1]) < ATTN_WIDTH, scale, 1.0).astype(F32)
    return (w * col_scale[None, :]).astype(BF16)


def kernel(x, meta, pre_mix_g, w_in, lambda_q1, lambda_k1, lambda_q2, lambda_k2, subln_g, a_re, a_im, log_dt,
           b_re, b_im, c_re, c_im, d_skip, w_glu, b_glu, ssm_out_g, w_out, post_mix_g, pre_ffn_g, w_gate,
           w_up, w_down, post_ffn_g):
    bsz, seq, _ = x.shape
    n = bsz * seq
    x2d = x.reshape(n, D_MODEL)
    row = lambda t: t[0].reshape(1, -1)

    w_in_p = _scale_w_in(w_in[0])
    g0 = row(pre_mix_g)
    q, k, v, u = _in_proj(x2d, g0, w_in_p, N_META, seq, IN_ROW_TILE, True)
    _, k_m, v_m, u_m = _in_proj(meta, g0, w_in_p, 0, N_META, N_META, False)

    pad_meta = lambda t: jnp.pad(t, ((0, LANES - N_META), (0, 0)))
    o = _attention(q, k, v, pad_meta(k_m).T, pad_meta(v_m), lambda_q1, lambda_k1, lambda_q2, lambda_k2,
                   row(subln_g), bsz, seq)

    ssm_w = _ssm_weights(a_re[0], a_im[0], log_dt[0], b_re[0], b_im[0], c_re[0], c_im[0], d_skip[0],
                         seq // CHUNK)
    y = _ssm(u, u_m, *ssm_w, bsz, seq)

    out = _out_ffn(x2d, o, y, w_glu[0].astype(BF16), row(b_glu), row(ssm_out_g), w_out[0].astype(BF16),
                   row(post_mix_g), row(pre_ffn_g), w_gate[0].astype(BF16), w_up[0].astype(BF16),
                   w_down[0].astype(BF16), row(post_ffn_g), ROW_TILE)
    return out.reshape(bsz, seq, D_MODEL)
```
